```python
import jax, jax.numpy as jnp
from jax import lax
import numpy as np

D_MODEL = 1024
BATCH = 16
SEQ = 2048
DEPTH = 4

N_MIXERS = 3
NORM_EPS = 1e-6
POOL_WINDOWS = (2, 4, 8, 16)
POOL_GROUPS = len(POOL_WINDOWS)
POOL_CH = D_MODEL // POOL_GROUPS
SCONV_WIDTH = 3
SSM_EXPAND = 2
SSM_D_INNER = SSM_EXPAND * D_MODEL
SSM_HEAD_DIM = 64
SSM_HEADS = SSM_D_INNER // SSM_HEAD_DIM
SSM_GROUPS = 4
SSM_STATE = 128
SSM_CONV_WIDTH = 4
SSM_CHUNK = 128
SSM_CONV_DIM = SSM_D_INNER + 2 * SSM_GROUPS * SSM_STATE
SSM_IN_DIM = SSM_D_INNER + SSM_CONV_DIM + SSM_HEADS
MOE_GROUPS = 4
MOE_EXPERTS_PER_GROUP = 8
MOE_EXPERTS = MOE_GROUPS * MOE_EXPERTS_PER_GROUP
MOE_TOP_K = 2
MOE_D_FF = 512
MOE_BLOCK = 128
N_POOL_LAYERS = len(range(0, DEPTH, N_MIXERS))
N_SCONV_LAYERS = len(range(1, DEPTH, N_MIXERS))
N_SSM_LAYERS = len(range(2, DEPTH, N_MIXERS))

kernel_name = "hybrid_pool_conv_ssd_hmoe"


def rms_norm(x, w):
    xf = x.astype(jnp.float32)
    y = xf * lax.rsqrt(jnp.mean(xf * xf, axis=-1, keepdims=True) + NORM_EPS)
    return (y * w.astype(jnp.float32)).astype(x.dtype)


def causal_depthwise_conv(u, w):
    k, c = w.shape
    return lax.conv_general_dilated(
        u, w[:, None, :].astype(u.dtype), window_strides=(1,),
        padding=[(k - 1, 0)], dimension_numbers=("NWC", "WIO", "NWC"),
        feature_group_count=c)


def pool_mixer(h, w_grp, scale):
    bsz, s, d = h.shape
    hg = h.astype(jnp.float32).reshape(bsz, s, POOL_GROUPS, POOL_CH)
    csum = jnp.cumsum(hg, axis=1)
    outs = []
    for g, win in enumerate(POOL_WINDOWS):
        c = csum[:, :, g]
        lagged = jnp.pad(c, ((0, 0), (win, 0), (0, 0)))[:, :s]
        count = jnp.minimum(jnp.arange(1, s + 1), win).astype(jnp.float32)[None, :, None]
        outs.append((c - lagged) / count - hg[:, :, g])
    pooled = jnp.stack(outs, axis=2).astype(h.dtype)
    y = jnp.einsum("bsgc,gcd->bsgd", pooled, w_grp).reshape(bsz, s, d)
    return y * scale


def short_conv_mixer(h, w_in, taps, w_out):
    b_gate, c_gate, v = jnp.split(h @ w_in, 3, axis=-1)
    u = causal_depthwise_conv(c_gate * v, taps)
    return (b_gate * u) @ w_out


def ssd_chunked(x, a, b, c):
    bsz, s, nh, p = x.shape
    g, n = b.shape[2], b.shape[3]
    r = nh // g
    q = SSM_CHUNK
    nc = s // q
    x = x.reshape(bsz, nc, q, g, r, p)
    a = a.reshape(bsz, nc, q, g, r).transpose(0, 1, 3, 4, 2)
    b = b.reshape(bsz, nc, q, g, n)
    c = c.reshape(bsz, nc, q, g, n)
    a_cs = jnp.cumsum(a, axis=-1)
    causal = jnp.tril(jnp.ones((q, q), dtype=bool))
    seg = a_cs[..., :, None] - a_cs[..., None, :]
    decay = jnp.exp(jnp.where(causal, seg, -jnp.inf))
    cb = jnp.einsum("bclgn,bcsgn->bcgls", c, b)
    y_diag = jnp.einsum("bcgrls,bcsgrp->bclgrp", cb[:, :, :, None] * decay, x)
    decay_states = jnp.exp(a_cs[..., -1:] - a_cs).transpose(0, 1, 4, 2, 3)[..., None]
    states = jnp.einsum("bclgn,bclgrp->bcgrpn", b, x * decay_states)
    chunk_decay = jnp.exp(a_cs[..., -1])

    def step(carry, inp):
        st, dec = inp
        return carry * dec[..., None, None] + st, carry

    init = jnp.zeros((bsz, g, r, p, n), dtype=x.dtype)
    _, prev = lax.scan(step, init, (states.transpose(1, 0, 2, 3, 4, 5),
                                    chunk_decay.transpose(1, 0, 2, 3)))
    prev = prev.transpose(1, 0, 2, 3, 4, 5)
    out_decay = jnp.exp(a_cs).transpose(0, 1, 4, 2, 3)[..., None]
    y_off = jnp.einsum("bclgn,bcgrpn->bclgrp", c, prev) * out_decay
    return (y_diag + y_off).reshape(bsz, s, nh, p)


def mamba2_mixer(h, w_in, conv_w, conv_b, dt_bias, a_log, d_skip, norm_w, w_out):
    bsz, s, _ = h.shape
    f32 = jnp.float32
    z, xbc, dt = jnp.split(h @ w_in, [SSM_D_INNER, SSM_D_INNER + SSM_CONV_DIM], axis=-1)
    xbc = jax.nn.silu(causal_depthwise_conv(xbc, conv_w) + conv_b)
    xs, bm, cm = jnp.split(xbc, [SSM_D_INNER, SSM_D_INNER + SSM_GROUPS * SSM_STATE], axis=-1)
    dt = jax.nn.softplus(dt.astype(f32) + dt_bias.astype(f32))
    a = -jnp.exp(a_log.astype(f32))
    xh = xs.astype(f32).reshape(bsz, s, SSM_HEADS, SSM_HEAD_DIM)
    y = ssd_chunked(xh * dt[..., None], dt * a,
                    bm.astype(f32).reshape(bsz, s, SSM_GROUPS, SSM_STATE),
                    cm.astype(f32).reshape(bsz, s, SSM_GROUPS, SSM_STATE))
    y = y + xh * d_skip.astype(f32)[:, None]
    y = y.reshape(bsz, s, SSM_D_INNER) * jax.nn.silu(z.astype(f32))
    yg = y.reshape(bsz, s, SSM_GROUPS, SSM_D_INNER // SSM_GROUPS)
    yg = yg * lax.rsqrt(jnp.mean(yg * yg, axis=-1, keepdims=True) + NORM_EPS)
    y = (yg.reshape(bsz, s, SSM_D_INNER) * norm_w.astype(f32)).astype(h.dtype)
    return y @ w_out


def grouped_expert_ffn(xt, expert, gate, w_gu, w_dn):
    t, d = xt.shape
    k = expert.shape[1]
    n_exp = w_gu.shape[0]
    n_assign = t * k
    flat_e = expert.reshape(-1)
    flat_tok = jnp.arange(n_assign, dtype=jnp.int32) // k
    flat_w = gate.reshape(-1)
    order = jnp.argsort(flat_e)
    sorted_e = flat_e[order]
    counts = jnp.bincount(flat_e, length=n_exp)
    padded = (counts + MOE_BLOCK - 1) // MOE_BLOCK * MOE_BLOCK
    pad_end = jnp.cumsum(padded)
    pad_start = pad_end - padded
    start = jnp.cumsum(counts) - counts
    dest = pad_start[sorted_e] + (jnp.arange(n_assign, dtype=jnp.int32) - start[sorted_e])
    n_rows = (n_assign + MOE_BLOCK - 1) // MOE_BLOCK * MOE_BLOCK + n_exp * MOE_BLOCK
    n_blocks = n_rows // MOE_BLOCK
    buf_tok = jnp.full((n_rows,), t, dtype=jnp.int32).at[dest].set(flat_tok[order])
    buf_w = jnp.zeros((n_rows,), dtype=xt.dtype).at[dest].set(flat_w[order])
    block_e = jnp.minimum(
        jnp.searchsorted(pad_end, jnp.arange(n_blocks) * MOE_BLOCK, side="right"), n_exp - 1)
    x_pad = jnp.concatenate([xt, jnp.zeros((1, d), dtype=xt.dtype)], axis=0)
    xb = x_pad[buf_tok].reshape(n_blocks, MOE_BLOCK, d)

    def expert_block(args):
        xblk, e = args
        g_part, up = jnp.split(xblk @ w_gu[e], 2, axis=-1)
        return (jax.nn.silu(g_part) * up) @ w_dn[e]

    yb = lax.map(expert_block, (xb, block_e)).reshape(n_rows, d)
    out = jnp.zeros((t + 1, d), dtype=xt.dtype).at[buf_tok].add(yb * buf_w[:, None])
    return out[:t]


def hier_moe(h, w_rg, b_rg, w_re, b_re, w_gu, w_dn):
    bsz, s, d = h.shape
    t = bsz * s
    xt = h.reshape(t, d)
    g_logits = (xt @ w_rg).astype(jnp.float32) + b_rg.astype(jnp.float32)
    g_prob = jax.nn.softmax(g_logits, axis=-1)
    g_sel = jnp.argmax(g_logits, axis=-1)
    p_group = jnp.take_along_axis(g_prob, g_sel[:, None], axis=1)
    e_logits = ((xt @ w_re).astype(jnp.float32) + b_re.astype(jnp.float32)).reshape(
        t, MOE_GROUPS, MOE_EXPERTS_PER_GROUP)
    e_logits = jnp.take_along_axis(e_logits, g_sel[:, None, None], axis=1)[:, 0]
    e_prob = jax.nn.softmax(e_logits, axis=-1)
    top_p, top_i = lax.top_k(e_prob, MOE_TOP_K)
    gate = (p_group * top_p / jnp.sum(top_p, axis=-1, keepdims=True)).astype(h.dtype)
    expert = (g_sel[:, None] * MOE_EXPERTS_PER_GROUP + top_i).astype(jnp.int32)
    return grouped_expert_ffn(xt, expert, gate, w_gu, w_dn).reshape(bsz, s, d)


def setup_inputs(seed: int = 0) -> dict:
    key = jax.random.key(seed)
    ks = jax.random.split(key, 24)
    f32 = jnp.float32
    d = D_MODEL

    def nrm(k, shape, fan_in):
        return jax.random.normal(k, shape, f32) * (fan_in ** -0.5)

    def gain(k, shape):
        return 1.0 + 0.05 * jax.random.normal(k, shape, f32)

    dt0 = jnp.exp(jax.random.uniform(ks[13], (N_SSM_LAYERS, SSM_HEADS), f32)
                  * (np.log(0.1) - np.log(0.001)) + np.log(0.001))
    dt_bias = dt0 + jnp.log(-jnp.expm1(-dt0))
    a_log = jnp.log(jax.random.uniform(ks[14], (N_SSM_LAYERS, SSM_HEADS), f32, 1.0, 16.0))
    return {
        "x": jax.random.normal(ks[0], (BATCH, SEQ, d), f32),
        "norm_mix": gain(ks[1], (DEPTH, d)),
        "norm_ffn": gain(ks[2], (DEPTH, d)),
        "norm_final": gain(ks[3], (d,)),
        "pool_w": nrm(ks[4], (N_POOL_LAYERS, POOL_GROUPS, POOL_CH, POOL_CH), POOL_CH),
        "pool_scale": gain(ks[5], (N_POOL_LAYERS, d)),
        "sconv_in_w": nrm(ks[6], (N_SCONV_LAYERS, d, 3 * d), d),
        "sconv_taps": nrm(ks[7], (N_SCONV_LAYERS, SCONV_WIDTH, d), SCONV_WIDTH),
        "sconv_out_w": nrm(ks[8], (N_SCONV_LAYERS, d, d), d),
        "ssm_in_w": nrm(ks[9], (N_SSM_LAYERS, d, SSM_IN_DIM), d),
        "ssm_conv_w": nrm(ks[10], (N_SSM_LAYERS, SSM_CONV_WIDTH, SSM_CONV_DIM), SSM_CONV_WIDTH),
        "ssm_conv_b": 0.01 * jax.random.normal(ks[11], (N_SSM_LAYERS, SSM_CONV_DIM), f32),
        "ssm_dt_bias": dt_bias,
        "ssm_a_log": a_log,
        "ssm_d": gain(ks[12], (N_SSM_LAYERS, SSM_HEADS)),
        "ssm_norm_w": gain(ks[15], (N_SSM_LAYERS, SSM_D_INNER)),
        "ssm_out_w": nrm(ks[16], (N_SSM_LAYERS, SSM_D_INNER, d), SSM_D_INNER),
        "router_group_w": nrm(ks[17], (DEPTH, d, MOE_GROUPS), d),
        "router_group_b": 0.01 * jax.random.normal(ks[18], (DEPTH, MOE_GROUPS), f32),
        "router_expert_w": nrm(ks[19], (DEPTH, d, MOE_EXPERTS), d),
        "router_expert_b": 0.01 * jax.random.normal(ks[20], (DEPTH, MOE_EXPERTS), f32),
        "expert_w_gu": nrm(ks[21], (DEPTH, MOE_EXPERTS, d, 2 * MOE_D_FF), d),
        "expert_w_down": nrm(ks[22], (DEPTH, MOE_EXPERTS, MOE_D_FF, d), MOE_D_FF),
    }


def reference(x, norm_mix, norm_ffn, norm_final, pool_w, pool_scale,
              sconv_in_w, sconv_taps, sconv_out_w,
              ssm_in_w, ssm_conv_w, ssm_conv_b, ssm_dt_bias, ssm_a_log, ssm_d,
              ssm_norm_w, ssm_out_w,
              router_group_w, router_group_b, router_expert_w, router_expert_b,
              expert_w_gu, expert_w_down):
    for i in range(DEPTH):
        kind, j = i % N_MIXERS, i // N_MIXERS
        h = rms_norm(x, norm_mix[i])
        if kind == 0:
            mix = pool_mixer(h, pool_w[j], pool_scale[j])
        elif kind == 1:
            mix = short_conv_mixer(h, sconv_in_w[j], sconv_taps[j], sconv_out_w[j])
        else:
            mix = mamba2_mixer(h, ssm_in_w[j], ssm_conv_w[j], ssm_conv_b[j], ssm_dt_bias[j],
                               ssm_a_log[j], ssm_d[j], ssm_norm_w[j], ssm_out_w[j])
        x = x + mix
        x = x + hier_moe(rms_norm(x, norm_ffn[i]), router_group_w[i], router_group_b[i],
                         router_expert_w[i], router_expert_b[i],
                         expert_w_gu[i], expert_w_down[i])
    return rms_norm(x, norm_final)
```

```python
import functools

import jax
import jax.numpy as jnp
from jax import lax
from jax.experimental import pallas as pl
from jax.experimental.pallas import tpu as pltpu

F32 = jnp.float32
BF16 = jnp.bfloat16
I32 = jnp.int32

NORM_EPS = 1e-6
N_MIXERS = 3
POOL_WINDOWS = (2, 4, 8, 16)
POOL_HALO = 16
CONV_HALO = 8
SSM_GROUPS = 4
SSM_CHUNK = 128
EXPERTS_PER_GROUP = 8
ROUTER_ROWS = 48

SUBLANES = 8
LANES = 128
VMEM_LIMIT_BYTES = 56 * 1024 * 1024

TS_POOL = 512
TS_SCONV = 512
TS_SSM = 256
TS_DISPATCH = 256
TS_COMBINE = 256
FFN_BLOCK = 256
DMA_UNROLL = 8

NT_DIMS = (((1,), (1,)), ((), ()))
TN_DIMS = (((0,), (0,)), ((), ()))


def _dot(a, b):
    return jnp.dot(a, b, preferred_element_type=F32)


def _dot_nt(a, b):
    return lax.dot_general(a, b, NT_DIMS, preferred_element_type=F32)


def _split3(a):
    a1 = a.astype(BF16)
    r1 = a - a1.astype(F32)
    a2 = r1.astype(BF16)
    a3 = (r1 - a2.astype(F32)).astype(BF16)
    return a1, a2, a3


def _dot3(a, b):
    a1, a2, a3 = _split3(a)
    return _dot(a1, b) + _dot(a2, b) + _dot(a3, b)


def _hilo(a):
    hi = a.astype(BF16)
    lo = (a - hi.astype(F32)).astype(BF16)
    return hi, lo


def _dot_nt_hilo(w_hi, w_lo, a_hi, a_lo):
    return _dot_nt(w_hi, a_hi) + _dot_nt(w_hi, a_lo) + _dot_nt(w_lo, a_hi)


def _rms(x, w):
    return x * lax.rsqrt(jnp.mean(x * x, axis=-1, keepdims=True) + NORM_EPS) * w


def _silu(x):
    return x * (1.0 / (1.0 + jnp.exp(-x)))


def _shift_rows(ext, k, halo):
    return pltpu.roll(ext, k, axis=0)[halo:]


def _route(xm, first, nffn_ref, wr_hi_ref, wr_lo_ref, br_ref, tri_ref, base_ref, ids_ref, gate_ref, cnt_ref,
           n_groups, n_experts):
    ts = xm.shape[0]

    @pl.when(first)
    def _():
        base_ref[...] = jnp.zeros_like(base_ref)

    h2 = _rms(xm, nffn_ref[...])
    h_hi, h_lo = _hilo(h2)
    lt = _dot_nt_hilo(wr_hi_ref[...], wr_lo_ref[...], h_hi, h_lo) + br_ref[...]

    row8 = lax.broadcasted_iota(I32, (SUBLANES, ts), 0)
    neg_inf = jnp.float32(-jnp.inf)
    g = jnp.where(row8 < n_groups, lt[0:SUBLANES], neg_inf)
    gmax = jnp.max(g, axis=0, keepdims=True)
    gsel = jnp.min(jnp.where(g == gmax, row8, SUBLANES), axis=0, keepdims=True)
    pgrp = 1.0 / jnp.sum(jnp.exp(g - gmax), axis=0, keepdims=True)

    sel = lt[SUBLANES:2 * SUBLANES]
    for j in range(1, n_groups):
        sel = jnp.where(gsel == j, lt[(j + 1) * SUBLANES:(j + 2) * SUBLANES], sel)
    m1 = jnp.max(sel, axis=0, keepdims=True)
    i1 = jnp.min(jnp.where(sel == m1, row8, SUBLANES), axis=0, keepdims=True)
    sel2 = jnp.where(row8 == i1, neg_inf, sel)
    m2 = jnp.max(sel2, axis=0, keepdims=True)
    i2 = jnp.min(jnp.where(sel2 == m2, row8, SUBLANES), axis=0, keepdims=True)
    r = jnp.exp(m2 - m1)
    den = 1.0 + r
    g0 = pgrp / den
    g1 = pgrp * r / den
    e0 = gsel * EXPERTS_PER_GROUP + i1
    e1 = gsel * EXPERTS_PER_GROUP + i2

    row_e = lax.broadcasted_iota(I32, (n_experts, ts), 0)
    oh0 = row_e == e0
    oh1 = row_e == e1
    oh = jnp.where(oh0 | oh1, 1.0, 0.0).astype(F32)
    prefix = _dot(oh.astype(BF16), tri_ref[...])
    tot = prefix + base_ref[:, 0:1]
    rank0 = jnp.sum(jnp.where(oh0, tot, 0.0), axis=0, keepdims=True).astype(I32)
    rank1 = jnp.sum(jnp.where(oh1, tot, 0.0), axis=0, keepdims=True).astype(I32)
    base_ref[...] = base_ref[...] + jnp.sum(oh, axis=1, keepdims=True)

    ids_ref[0] = jnp.where(row8 == 0, e0, jnp.where(row8 == 1, e1, jnp.where(row8 == 2, rank0, rank1)))
    gate_ref[0] = jnp.where(row8 == 0, g0, g1)
    cnt_ref[...] = base_ref[...].astype(I32)


def _router_operands(w_rg, b_rg, w_re, b_re, ts):
    d, n_groups = w_rg.shape
    n_experts = w_re.shape[1]
    assert n_experts == n_groups * EXPERTS_PER_GROUP and n_groups <= SUBLANES
    assert SUBLANES + n_experts <= ROUTER_ROWS
    wt = jnp.zeros((ROUTER_ROWS, d), F32)
    wt = wt.at[0:n_groups].set(w_rg.T).at[SUBLANES:SUBLANES + n_experts].set(w_re.T)
    bt = jnp.zeros((ROUTER_ROWS, 1), F32)
    bt = bt.at[0:n_groups, 0].set(b_rg).at[SUBLANES:SUBLANES + n_experts, 0].set(b_re)
    w_hi = wt.astype(BF16)
    w_lo = (wt - w_hi.astype(F32)).astype(BF16)
    idx = jnp.arange(ts)
    tri = (idx[:, None] < idx[None, :]).astype(BF16)
    return w_hi, w_lo, bt, tri


def _const_spec(shape):
    zeros = (0,) * len(shape)
    return pl.BlockSpec(shape, lambda b, s: zeros)


def _router_specs(d, ts, n_experts):
    in_specs = [
        _const_spec((1, d)),
        _const_spec((ROUTER_ROWS, d)),
        _const_spec((ROUTER_ROWS, d)),
        _const_spec((ROUTER_ROWS, 1)),
        _const_spec((ts, ts)),
    ]
    return in_specs


def _mixer_out(t, d, ts, n_s, n_experts):
    n_tiles = t // ts
    out_shape = [
        jax.ShapeDtypeStruct((t, d), F32),
        jax.ShapeDtypeStruct((n_tiles, SUBLANES, ts), I32),
        jax.ShapeDtypeStruct((n_tiles, SUBLANES, ts), F32),
        jax.ShapeDtypeStruct((n_experts, LANES), I32),
    ]
    out_specs = [
        pl.BlockSpec((ts, d), lambda b, s: (b * n_s + s, 0)),
        pl.BlockSpec((1, SUBLANES, ts), lambda b, s: (b * n_s + s, 0, 0)),
        pl.BlockSpec((1, SUBLANES, ts), lambda b, s: (b * n_s + s, 0, 0)),
        pl.BlockSpec((n_experts, LANES), lambda b, s: (0, 0)),
    ]
    return out_shape, out_specs


def _mixer_params():
    return pltpu.CompilerParams(dimension_semantics=("arbitrary", "arbitrary"),
                                vmem_limit_bytes=VMEM_LIMIT_BYTES)


def _pool_kernel(x_ref, nmix_ref, pw_ref, scale_ref,
                 nffn_ref, wr_hi_ref, wr_lo_ref, br_ref, tri_ref,
                 xm_ref, ids_ref, gate_ref, cnt_ref,
                 carry_ref, base_ref, *, n_groups, n_experts):
    b = pl.program_id(0)
    s = pl.program_id(1)
    ts, d = x_ref.shape
    ch = d // len(POOL_WINDOWS)

    @pl.when(s == 0)
    def _():
        carry_ref[...] = jnp.zeros_like(carry_ref)

    x = x_ref[...]
    h = _rms(x, nmix_ref[...])
    ext = jnp.concatenate([carry_ref[...], h], axis=0)
    carry_ref[...] = h[ts - POOL_HALO:]

    pos = s * ts + lax.broadcasted_iota(I32, (ts, 1), 0) + 1
    ys = []
    for g, win in enumerate(POOL_WINDOWS):
        acc = ext[:, g * ch:(g + 1) * ch]
        k = 1
        while k < win:
            acc = acc + pltpu.roll(acc, k, axis=0)
            k *= 2
        inv_cnt = 1.0 / jnp.minimum(pos, win).astype(F32)
        pooled = acc[POOL_HALO:] * inv_cnt - h[:, g * ch:(g + 1) * ch]
        ys.append(_dot(pooled.astype(BF16), pw_ref[g]))
    y = jnp.concatenate(ys, axis=1) * scale_ref[...]
    xm = x + y
    xm_ref[...] = xm
    _route(xm, (b == 0) & (s == 0), nffn_ref, wr_hi_ref, wr_lo_ref, br_ref, tri_ref, base_ref,
           ids_ref, gate_ref, cnt_ref, n_groups, n_experts)


def _pool_layer(x2d, bsz, seq, nmix, pool_w, pool_scale, nffn, router):
    t, d = x2d.shape
    ts = min(TS_POOL, seq)
    n_s = seq // ts
    w_hi, w_lo, bt, tri = router
    n_groups_pool, ch, _ = pool_w.shape
    n_experts = w_hi.shape[0]
    kern = functools.partial(_pool_kernel, n_groups=ROUTER_GROUPS[0], n_experts=ROUTER_GROUPS[1])
    out_shape, out_specs = _mixer_out(t, d, ts, n_s, ROUTER_GROUPS[1])
    return pl.pallas_call(
        kern,
        grid=(bsz, n_s),
        in_specs=[
            pl.BlockSpec((ts, d), lambda b, s: (b * n_s + s, 0)),
            _const_spec((1, d)),
            _const_spec((n_groups_pool, ch, ch)),
            _const_spec((1, d)),
        ] + _router_specs(d, ts, ROUTER_GROUPS[1]),
        out_specs=out_specs,
        out_shape=out_shape,
        scratch_shapes=[pltpu.VMEM((POOL_HALO, d), F32), pltpu.VMEM((ROUTER_GROUPS[1], LANES), F32)],
        compiler_params=_mixer_params(),
        name="pool_mixer",
    )(x2d, nmix.reshape(1, d), pool_w.astype(BF16), pool_scale.reshape(1, d),
      nffn.reshape(1, d), w_hi, w_lo, bt, tri)


ROUTER_GROUPS = (4, 32)


def _sconv_kernel(x_ref, nmix_ref, win_ref, taps_ref, wout_ref,
                  nffn_ref, wr_hi_ref, wr_lo_ref, br_ref, tri_ref,
                  xm_ref, ids_ref, gate_ref, cnt_ref,
                  carry_ref, base_ref, *, n_groups, n_experts):
    b = pl.program_id(0)
    s = pl.program_id(1)
    ts, d = x_ref.shape

    @pl.when(s == 0)
    def _():
        carry_ref[...] = jnp.zeros_like(carry_ref)

    x = x_ref[...]
    h = _rms(x, nmix_ref[...]).astype(BF16)
    b_gate = _dot(h, win_ref[:, 0:d])
    c_gate = _dot(h, win_ref[:, d:2 * d])
    v = _dot(h, win_ref[:, 2 * d:3 * d])
    u_pre = c_gate * v
    ext = jnp.concatenate([carry_ref[...], u_pre], axis=0)
    carry_ref[...] = u_pre[ts - CONV_HALO:]
    width = taps_ref.shape[0]
    u = taps_ref[width - 1:width, :] * u_pre
    for k in range(1, width):
        u = u + taps_ref[width - 1 - k:width - k, :] * _shift_rows(ext, k, CONV_HALO)
    mix = _dot((b_gate * u).astype(BF16), wout_ref[...])
    xm = x + mix
    xm_ref[...] = xm
    _route(xm, (b == 0) & (s == 0), nffn_ref, wr_hi_ref, wr_lo_ref, br_ref, tri_ref, base_ref,
           ids_ref, gate_ref, cnt_ref, n_groups, n_experts)


def _sconv_layer(x2d, bsz, seq, nmix, w_in, taps, w_out, nffn, router):
    t, d = x2d.shape
    ts = min(TS_SCONV, seq)
    n_s = seq // ts
    w_hi, w_lo, bt, tri = router
    width = taps.shape[0]
    assert width - 1 <= CONV_HALO
    kern = functools.partial(_sconv_kernel, n_groups=ROUTER_GROUPS[0], n_experts=ROUTER_GROUPS[1])
    out_shape, out_specs = _mixer_out(t, d, ts, n_s, ROUTER_GROUPS[1])
    return pl.pallas_call(
        kern,
        grid=(bsz, n_s),
        in_specs=[
            pl.BlockSpec((ts, d), lambda b, s: (b * n_s + s, 0)),
            _const_spec((1, d)),
            _const_spec((d, 3 * d)),
            _const_spec((width, d)),
            _const_spec((d, d)),
        ] + _router_specs(d, ts, ROUTER_GROUPS[1]),
        out_specs=out_specs,
        out_shape=out_shape,
        scratch_shapes=[pltpu.VMEM((CONV_HALO, d), F32), pltpu.VMEM((ROUTER_GROUPS[1], LANES), F32)],
        compiler_params=_mixer_params(),
        name="sconv_mixer",
    )(x2d, nmix.reshape(1, d), w_in.astype(BF16), taps, w_out.astype(BF16),
      nffn.reshape(1, d), w_hi, w_lo, bt, tri)


def _ssm_kernel(x_ref, nmix_ref, wz_ref, wxbc_ref, wdt_hi_ref, wdt_lo_ref, convw_ref, convb_ref,
                dtb_ref, alog_ref, expand_ref, dskip_ref, normw_ref, wout_ref, triq_ref,
                nffn_ref, wr_hi_ref, wr_lo_ref, br_ref, tri_ref,
                xm_ref, ids_ref, gate_ref, cnt_ref,
                carry_ref, z_ref, xs_ref, bm_ref, cm_ref, y_ref, dtt_ref, at_ref, state_ref, base_ref,
                *, n_groups, n_experts):
    b = pl.program_id(0)
    s = pl.program_id(1)
    ts, d = x_ref.shape
    di = z_ref.shape[1]
    gn = bm_ref.shape[1]
    cd = di + 2 * gn
    nh = alog_ref.shape[0]
    hp = di // nh
    grp = SSM_GROUPS
    st = gn // grp
    hpg = nh // grp
    gw = hpg * hp
    q = SSM_CHUNK
    n_chunks = ts // q

    @pl.when(s == 0)
    def _():
        carry_ref[...] = jnp.zeros_like(carry_ref)
        state_ref[...] = jnp.zeros_like(state_ref)

    x = x_ref[...]
    h = _rms(x, nmix_ref[...])
    h_hi, h_lo = _hilo(h)

    z_ref[...] = _dot(h_hi, wz_ref[...])
    width = convw_ref.shape[0]
    col_blk = gn
    for c0 in range(0, cd, col_blk):
        cols = slice(c0, c0 + col_blk)
        pre = _dot(h_hi, wxbc_ref[:, cols])
        ext = jnp.concatenate([carry_ref[:, cols], pre], axis=0)
        carry_ref[:, cols] = pre[ts - CONV_HALO:]
        acc = convw_ref[width - 1:width, cols] * pre + convb_ref[:, cols]
        for k in range(1, width):
            acc = acc + convw_ref[width - 1 - k:width - k, cols] * _shift_rows(ext, k, CONV_HALO)
        act = _silu(acc)
        if c0 < di:
            xs_ref[:, cols] = act
        elif c0 < di + gn:
            bm_ref[...] = act
        else:
            cm_ref[...] = act

    dt_t = _dot_nt_hilo(wdt_hi_ref[...], wdt_lo_ref[...], h_hi, h_lo) + dtb_ref[...]
    dt_t = jnp.maximum(dt_t, 0.0) + jnp.log(1.0 + jnp.exp(-jnp.abs(dt_t)))
    a_t = dt_t * (-jnp.exp(alog_ref[...]))
    for c in range(n_chunks):
        dtt_ref[c] = dt_t[:, c * q:(c + 1) * q]
        at_ref[c] = a_t[:, c * q:(c + 1) * q]

    row_q = lax.broadcasted_iota(I32, (q, q), 0)
    col_q = lax.broadcasted_iota(I32, (q, q), 1)
    causal = row_q >= col_q
    lane_lo = lax.broadcasted_iota(I32, (q, 2 * hp), 1) < hp
    neg_inf = jnp.float32(-jnp.inf)
    expand = expand_ref[...]

    def chunk_body(c, carry):
        r0 = pl.multiple_of(c * q, q)
        rows = pl.ds(r0, q)
        acs_t = _dot3(at_ref[c], triq_ref[...])
        acs = acs_t.T
        dt_c = dtt_ref[c].T
        last = acs[q - 1:q, :]
        dt_e = _dot3(dt_c, expand)
        out_decay_e = _dot3(jnp.exp(acs), expand)
        st_decay_e = _dot3(jnp.exp(last - acs), expand)
        xc = xs_ref[rows, :]
        xdt = xc * dt_e
        bc = bm_ref[rows, :]
        cc = cm_ref[rows, :]
        for g in range(grp):
            gcols = slice(g * gw, (g + 1) * gw)
            bg = bc[:, g * st:(g + 1) * st].astype(BF16)
            cg = cc[:, g * st:(g + 1) * st].astype(BF16)
            xg = xdt[:, gcols]
            cb = _dot_nt(cg, bg)
            hprev = state_ref[g]
            y_off = _dot(cg, hprev.astype(BF16)) * out_decay_e[:, gcols]
            new_states = lax.dot_general(bg, (xg * st_decay_e[:, gcols]).astype(BF16), TN_DIMS,
                                         preferred_element_type=F32)
            state_ref[g] = hprev * out_decay_e[q - 1:q, gcols] + new_states
            for j in range(hpg // 2):
                h0 = g * hpg + 2 * j
                ms = []
                for hh in (h0, h0 + 1):
                    seg = acs[:, hh:hh + 1] - acs_t[hh:hh + 1, :]
                    ms.append((cb * jnp.exp(jnp.where(causal, seg, neg_inf))).astype(BF16))
                lhs = jnp.concatenate(ms, axis=1)
                xp = xg[:, 2 * j * hp:(2 * j + 2) * hp]
                rhs = jnp.concatenate([jnp.where(lane_lo, xp, 0.0), jnp.where(lane_lo, 0.0, xp)],
                                      axis=0).astype(BF16)
                pc = slice(g * gw + 2 * j * hp, g * gw + (2 * j + 2) * hp)
                y_ref[rows, pc] = _dot(lhs, rhs) + y_off[:, 2 * j * hp:(2 * j + 2) * hp]
        return carry

    lax.fori_loop(0, n_chunks, chunk_body, 0)

    out = None
    for g in range(grp):
        gcols = slice(g * gw, (g + 1) * gw)
        yg = (y_ref[:, gcols] + xs_ref[:, gcols] * dskip_ref[:, gcols]) * _silu(z_ref[:, gcols])
        yn = yg * lax.rsqrt(jnp.mean(yg * yg, axis=-1, keepdims=True) + NORM_EPS) * normw_ref[:, gcols]
        part = _dot(yn.astype(BF16), wout_ref[gcols, :])
        out = part if out is None else out + part
    xm = x + out
    xm_ref[...] = xm
    _route(xm, (b == 0) & (s == 0), nffn_ref, wr_hi_ref, wr_lo_ref, br_ref, tri_ref, base_ref,
           ids_ref, gate_ref, cnt_ref, n_groups, n_experts)


def _ssm_layer(x2d, bsz, seq, nmix, w_in, conv_w, conv_b, dt_bias, a_log, d_skip, norm_w, w_out, nffn, router):
    t, d = x2d.shape
    ts = min(TS_SSM, seq)
    n_s = seq // ts
    assert ts % SSM_CHUNK == 0
    w_hi, w_lo, bt, tri = router
    nh = a_log.shape[0]
    di = norm_w.shape[0]
    cd = conv_w.shape[1]
    gn = (cd - di) // 2
    hp = di // nh
    width = conv_w.shape[0]
    assert width - 1 <= CONV_HALO and (nh // SSM_GROUPS) % 2 == 0 and cd % gn == 0
    st = gn // SSM_GROUPS
    gw = di // SSM_GROUPS
    q = SSM_CHUNK
    n_chunks = ts // q

    wz = w_in[:, :di].astype(BF16)
    wxbc = w_in[:, di:di + cd].astype(BF16)
    wdt_t = w_in[:, di + cd:].T
    wdt_hi = wdt_t.astype(BF16)
    wdt_lo = (wdt_t - wdt_hi.astype(F32)).astype(BF16)
    expand = (jnp.arange(di)[None, :] // hp == jnp.arange(nh)[:, None]).astype(BF16)
    dskip_e = jnp.repeat(d_skip, hp).reshape(1, di)
    idx = jnp.arange(q)
    triq = (idx[:, None] <= idx[None, :]).astype(BF16)

    kern = functools.partial(_ssm_kernel, n_groups=ROUTER_GROUPS[0], n_experts=ROUTER_GROUPS[1])
    out_shape, out_specs = _mixer_out(t, d, ts, n_s, ROUTER_GROUPS[1])
    return pl.pallas_call(
        kern,
        grid=(bsz, n_s),
        in_specs=[
            pl.BlockSpec((ts, d), lambda b, s: (b * n_s + s, 0)),
            _const_spec((1, d)),
            _const_spec((d, di)),
            _const_spec((d, cd)),
            _const_spec((nh, d)),
            _const_spec((nh, d)),
            _const_spec((width, cd)),
            _const_spec((1, cd)),
            _const_spec((nh, 1)),
            _const_spec((nh, 1)),
            _const_spec((nh, di)),
            _const_spec((1, di)),
            _const_spec((1, di)),
            _const_spec((di, d)),
            _const_spec((q, q)),
        ] + _router_specs(d, ts, ROUTER_GROUPS[1]),
        out_specs=out_specs,
        out_shape=out_shape,
        scratch_shapes=[
            pltpu.VMEM((CONV_HALO, cd), F32),
            pltpu.VMEM((ts, di), F32),
            pltpu.VMEM((ts, di), F32),
            pltpu.VMEM((ts, gn), F32),
            pltpu.VMEM((ts, gn), F32),
            pltpu.VMEM((ts, di), F32),
            pltpu.VMEM((n_chunks, nh, q), F32),
            pltpu.VMEM((n_chunks, nh, q), F32),
            pltpu.VMEM((SSM_GROUPS, st, gw), F32),
            pltpu.VMEM((ROUTER_GROUPS[1], LANES), F32),
        ],
        compiler_params=_mixer_params(),
        name="ssm_mixer",
    )(x2d, nmix.reshape(1, d), wz, wxbc, wdt_hi, wdt_lo, conv_w, conv_b.reshape(1, cd),
      dt_bias.reshape(nh, 1), a_log.reshape(nh, 1), expand, dskip_e, norm_w.reshape(1, di),
      w_out.astype(BF16), triq, nffn.reshape(1, d), w_hi, w_lo, bt, tri)


def _dispatch_kernel(dest_ref, xm_ref, nffn_ref, xs_init_ref, xs_ref, hbuf_ref, sem_ref, *, n_tokens):
    del xs_init_ref
    i = pl.program_id(0)
    n = pl.num_programs(0)
    td = xm_ref.shape[0]
    slot = i % 2

    def row_copy(sl, t, dst_row):
        return pltpu.make_async_copy(hbuf_ref.at[sl, pl.ds(t, 1), :], xs_ref.at[pl.ds(dst_row, 1), :],
                                     sem_ref.at[sl])

    def drain(sl):
        def body(t, carry):
            row_copy(sl, 0, 0).wait()
            return carry
        lax.fori_loop(0, MOE_TOP_K * td, body, 0)

    @pl.when(i >= 2)
    def _():
        drain(slot)

    hbuf_ref[slot] = _rms(xm_ref[...], nffn_ref[...])

    def issue(tb, carry):
        for u in range(DMA_UNROLL):
            t = tb * DMA_UNROLL + u
            for k in range(MOE_TOP_K):
                row_copy(slot, t, dest_ref[k * n_tokens + i * td + t]).start()
        return carry

    lax.fori_loop(0, td // DMA_UNROLL, issue, 0)

    @pl.when(i == n - 1)
    def _():
        drain(slot)

        @pl.when(n >= 2)
        def _():
            drain(1 - slot)


MOE_TOP_K = 2


def _dispatch(xm, nffn, dest_flat, n_rows):
    t, d = xm.shape
    td = min(TS_DISPATCH, t)
    assert t % td == 0 and td % DMA_UNROLL == 0
    kern = functools.partial(_dispatch_kernel, n_tokens=t)
    xs_init = jnp.zeros((n_rows, d), F32)
    return pl.pallas_call(
        kern,
        grid_spec=pltpu.PrefetchScalarGridSpec(
            num_scalar_prefetch=1,
            grid=(t // td,),
            in_specs=[
                pl.BlockSpec((td, d), lambda i, dest: (i, 0)),
                pl.BlockSpec((1, d), lambda i, dest: (0, 0)),
                pl.BlockSpec(memory_space=pl.ANY),
            ],
            out_specs=pl.BlockSpec(memory_space=pl.ANY),
            scratch_shapes=[pltpu.VMEM((2, td, d), F32), pltpu.SemaphoreType.DMA((2,))],
        ),
        out_shape=jax.ShapeDtypeStruct((n_rows, d), F32),
        input_output_aliases={3: 0},
        compiler_params=pltpu.CompilerParams(dimension_semantics=("arbitrary",),
                                             vmem_limit_bytes=VMEM_LIMIT_BYTES),
        name="moe_dispatch",
    )(dest_flat, xm, nffn.reshape(1, d), xs_init)


def _ffn_kernel(be_ref, nu_ref, xs_ref, wgu_ref, wdn_ref, ys_ref, wgu_bf_ref, wdn_bf_ref):
    i = pl.program_id(0)
    f = wdn_ref.shape[1]
    prev = be_ref[jnp.maximum(i - 1, 0)]

    @pl.when((i == 0) | (be_ref[i] != prev))
    def _():
        wgu_bf_ref[...] = wgu_ref[0].astype(BF16)
        wdn_bf_ref[...] = wdn_ref[0].astype(BF16)

    @pl.when(i < nu_ref[0])
    def _():
        gu = _dot(xs_ref[...].astype(BF16), wgu_bf_ref[...])
        act = _silu(gu[:, :f]) * gu[:, f:]
        ys_ref[...] = _dot(act.astype(BF16), wdn_bf_ref[...])

    @pl.when(i >= nu_ref[0])
    def _():
        ys_ref[...] = jnp.zeros_like(ys_ref)


def _expert_ffn(xs, block_e, n_used, w_gu, w_dn):
    n_rows, d = xs.shape
    bm = FFN_BLOCK
    n_blocks = n_rows // bm
    _, f, _ = w_dn.shape

    def xs_map(i, be, nu):
        return (jnp.minimum(i, jnp.maximum(nu[0] - 1, 0)), 0)

    return pl.pallas_call(
        _ffn_kernel,
        grid_spec=pltpu.PrefetchScalarGridSpec(
            num_scalar_prefetch=2,
            grid=(n_blocks,),
            in_specs=[
                pl.BlockSpec((bm, d), xs_map),
                pl.BlockSpec((1, d, 2 * f), lambda i, be, nu: (be[i], 0, 0)),
                pl.BlockSpec((1, f, d), lambda i, be, nu: (be[i], 0, 0)),
            ],
            out_specs=pl.BlockSpec((bm, d), lambda i, be, nu: (i, 0)),
            scratch_shapes=[pltpu.VMEM((d, 2 * f), BF16), pltpu.VMEM((f, d), BF16)],
        ),
        out_shape=jax.ShapeDtypeStruct((n_rows, d), F32),
        compiler_params=pltpu.CompilerParams(dimension_semantics=("arbitrary",),
                                             vmem_limit_bytes=VMEM_LIMIT_BYTES),
        name="expert_ffn",
    )(block_e, n_used, xs, w_gu, w_dn)


def _combine_kernel(dest_ref, xm_ref, gate_ref, nfin_ref, ys_ref, out_ref, ybuf_ref, sem_ref, *,
                    n_tokens, final_norm):
    i = pl.program_id(0)
    n = pl.num_programs(0)
    tc = xm_ref.shape[0]
    slot = i % 2

    def row_copy(sl, k, t, src_row):
        return pltpu.make_async_copy(ys_ref.at[pl.ds(src_row, 1), :], ybuf_ref.at[sl, k, pl.ds(t, 1), :],
                                     sem_ref.at[sl])

    def issue_tile(tile, sl):
        def body(tb, carry):
            for u in range(DMA_UNROLL):
                t = tb * DMA_UNROLL + u
                for k in range(MOE_TOP_K):
                    row_copy(sl, k, t, dest_ref[k * n_tokens + tile * tc + t]).start()
            return carry
        lax.fori_loop(0, tc // DMA_UNROLL, body, 0)

    @pl.when(i == 0)
    def _():
        issue_tile(0, 0)

    @pl.when(i + 1 < n)
    def _():
        issue_tile(i + 1, 1 - slot)

    def drain(t, carry):
        row_copy(slot, 0, 0, 0).wait()
        return carry

    lax.fori_loop(0, MOE_TOP_K * tc, drain, 0)

    gates = gate_ref[...]
    out = xm_ref[...] + gates[:, 0:1] * ybuf_ref[slot, 0] + gates[:, 1:2] * ybuf_ref[slot, 1]
    if final_norm:
        out = _rms(out, nfin_ref[...])
    out_ref[...] = out


def _combine(xm, ys, dest_flat, gates, norm_final, final_norm):
    t, d = xm.shape
    tc = min(TS_COMBINE, t)
    assert t % tc == 0 and tc % DMA_UNROLL == 0
    kern = functools.partial(_combine_kernel, n_tokens=t, final_norm=final_norm)
    return pl.pallas_call(
        kern,
        grid_spec=pltpu.PrefetchScalarGridSpec(
            num_scalar_prefetch=1,
            grid=(t // tc,),
            in_specs=[
                pl.BlockSpec((tc, d), lambda i, dest: (i, 0)),
                pl.BlockSpec((tc, MOE_TOP_K), lambda i, dest: (i, 0)),
                pl.BlockSpec((1, d), lambda i, dest: (0, 0)),
                pl.BlockSpec(memory_space=pl.ANY),
            ],
            out_specs=pl.BlockSpec((tc, d), lambda i, dest: (i, 0)),
            scratch_shapes=[pltpu.VMEM((2, MOE_TOP_K, tc, d), F32), pltpu.SemaphoreType.DMA((2,))],
        ),
        out_shape=jax.ShapeDtypeStruct((t, d), F32),
        compiler_params=pltpu.CompilerParams(dimension_semantics=("arbitrary",),
                                             vmem_limit_bytes=VMEM_LIMIT_BYTES),
        name="moe_combine",
    )(dest_flat, xm, gates, norm_final.reshape(1, d), ys)


def _moe(xm, ids, gate, counts, nffn, w_gu, w_dn, norm_final, final_norm):
    t, d = xm.shape
    n_exp = w_gu.shape[0]
    bm = FFN_BLOCK
    n_assign = t * MOE_TOP_K
    n_rows = (n_assign + bm - 1) // bm * bm + n_exp * bm
    n_blocks = n_rows // bm

    expert = jnp.transpose(ids[:, 0:2, :], (1, 0, 2)).reshape(MOE_TOP_K, t)
    rank = jnp.transpose(ids[:, 2:4, :], (1, 0, 2)).reshape(MOE_TOP_K, t)
    gates = jnp.transpose(gate[:, 0:2, :], (0, 2, 1)).reshape(t, MOE_TOP_K)
    cnt = counts[:, 0]
    padded = (cnt + bm - 1) // bm * bm
    pad_end = jnp.cumsum(padded)
    pad_start = pad_end - padded
    dest = (pad_start[expert] + rank).astype(I32).reshape(-1)
    block_e = jnp.minimum(jnp.searchsorted(pad_end, jnp.arange(n_blocks, dtype=I32) * bm, side="right"),
                          n_exp - 1).astype(I32)
    n_used = (pad_end[-1:] // bm).astype(I32)

    xs = _dispatch(xm, nffn, dest, n_rows)
    ys = _expert_ffn(xs, block_e, n_used, w_gu, w_dn)
    return _combine(xm, ys, dest, gates, norm_final, final_norm)


def kernel(x, norm_mix, norm_ffn, norm_final, pool_w, pool_scale, sconv_in_w, sconv_taps, sconv_out_w, ssm_in_w, ssm_conv_w, ssm_conv_b, ssm_dt_bias, ssm_a_log, ssm_d, ssm_norm_w, ssm_out_w, router_group_w, router_group_b, router_expert_w, router_expert_b, expert_w_gu, expert_w_down):
    bsz, seq, d = x.shape
    depth = norm_mix.shape[0]
    assert (router_group_w.shape[2], router_expert_w.shape[2]) == ROUTER_GROUPS
    x2d = x.reshape(bsz * seq, d)
    for i in range(depth):
        kind, j = i % N_MIXERS, i // N_MIXERS
        if kind == 0:
            ts = min(TS_POOL, seq)
        elif kind == 1:
            ts = min(TS_SCONV, seq)
        else:
            ts = min(TS_SSM, seq)
        router = _router_operands(router_group_w[i], router_group_b[i], router_expert_w[i],
                                  router_expert_b[i], ts)
        if kind == 0:
            xm, ids, gate, counts = _pool_layer(x2d, bsz, seq, norm_mix[i], pool_w[j], pool_scale[j],
                                                norm_ffn[i], router)
        elif kind == 1:
            xm, ids, gate, counts = _sconv_layer(x2d, bsz, seq, norm_mix[i], sconv_in_w[j], sconv_taps[j],
                                                 sconv_out_w[j], norm_ffn[i], router)
        else:
            xm, ids, gate, counts = _ssm_layer(x2d, bsz, seq, norm_mix[i], ssm_in_w[j], ssm_conv_w[j],
                                               ssm_conv_b[j], ssm_dt_bias[j], ssm_a_log[j], ssm_d[j],
                                               ssm_norm_w[j], ssm_out_w[j], norm_ffn[i], router)
        x2d = _moe(xm, ids, gate, counts, norm_ffn[i], expert_w_gu[i], expert_w_down[i], norm_final,
                   final_norm=(i == depth - 1))
    return x2d.reshape(bsz, seq, d)
```

```python
import functools

import jax
import jax.numpy as jnp
from jax import lax
from jax.experimental import pallas as pl
from jax.experimental.pallas import tpu as pltpu
from jax.experimental.pallas import tpu_sc as plsc

F32 = jnp.float32
BF16 = jnp.bfloat16
I32 = jnp.int32

NORM_EPS = 1e-6
N_MIXERS = 3
POOL_WINDOWS = (2, 4, 8, 16)
POOL_HALO = 16
CONV_HALO = 8
SSM_GROUPS = 4
SSM_CHUNK = 128
MOE_TOP_K = 2
EXPERTS_PER_GROUP = 8
ROUTER_GROUPS = (4, 32)
ROUTER_ROWS = 48

SUBLANES = 8
LANES = 128
VMEM_LIMIT_BYTES = 56 * 1024 * 1024
SC_CORES = 2
SC_WORKERS = SC_CORES * 16
SC_WINDOW = 32

TS_POOL = 512
TS_SCONV = 512
TS_SSM = 256
TS_FINAL = 512
FFN_BLOCK = 256

NT_DIMS = (((1,), (1,)), ((), ()))
TN_DIMS = (((0,), (0,)), ((), ()))


def _dot(a, b):
    return jnp.dot(a, b, preferred_element_type=F32)


def _dot_nt(a, b):
    return lax.dot_general(a, b, NT_DIMS, preferred_element_type=F32)


def _split3(a):
    a1 = a.astype(BF16)
    r1 = a - a1.astype(F32)
    a2 = r1.astype(BF16)
    a3 = (r1 - a2.astype(F32)).astype(BF16)
    return a1, a2, a3


def _dot3(a, b):
    a1, a2, a3 = _split3(a)
    return _dot(a1, b) + _dot(a2, b) + _dot(a3, b)


def _hilo(a):
    hi = a.astype(BF16)
    lo = (a - hi.astype(F32)).astype(BF16)
    return hi, lo


def _dot_nt_hilo(w_hi, w_lo, a_hi, a_lo):
    return _dot_nt(w_hi, a_hi) + _dot_nt(w_hi, a_lo) + _dot_nt(w_lo, a_hi)


def _rms(x, w):
    return x * lax.rsqrt(jnp.mean(x * x, axis=-1, keepdims=True) + NORM_EPS) * w


def _silu(x):
    return x * (1.0 / (1.0 + jnp.exp(-x)))


def _shift_rows(ext, k, halo):
    return pltpu.roll(ext, k, axis=0)[halo:]


def _combined(xm_ref, y0_ref, y1_ref, g_ref):
    g = g_ref[...]
    return xm_ref[...] + g[:, 0:1] * y0_ref[...] + g[:, 1:2] * y1_ref[...]


def _route(xm, first, nffn_ref, wr_hi_ref, wr_lo_ref, br_ref, tri_ref, base_ref, ids_ref, gate_ref, cnt_ref):
    n_groups, n_experts = ROUTER_GROUPS
    ts = xm.shape[0]

    @pl.when(first)
    def _():
        base_ref[...] = jnp.zeros_like(base_ref)

    h2 = _rms(xm, nffn_ref[...])
    h_hi, h_lo = _hilo(h2)
    lt = _dot_nt_hilo(wr_hi_ref[...], wr_lo_ref[...], h_hi, h_lo) + br_ref[...]

    row8 = lax.broadcasted_iota(I32, (SUBLANES, ts), 0)
    neg_inf = jnp.float32(-jnp.inf)
    g = jnp.where(row8 < n_groups, lt[0:SUBLANES], neg_inf)
    gmax = jnp.max(g, axis=0, keepdims=True)
    gsel = jnp.min(jnp.where(g == gmax, row8, SUBLANES), axis=0, keepdims=True)
    pgrp = 1.0 / jnp.sum(jnp.exp(g - gmax), axis=0, keepdims=True)

    sel = lt[SUBLANES:2 * SUBLANES]
    for j in range(1, n_groups):
        sel = jnp.where(gsel == j, lt[(j + 1) * SUBLANES:(j + 2) * SUBLANES], sel)
    m1 = jnp.max(sel, axis=0, keepdims=True)
    i1 = jnp.min(jnp.where(sel == m1, row8, SUBLANES), axis=0, keepdims=True)
    sel2 = jnp.where(row8 == i1, neg_inf, sel)
    m2 = jnp.max(sel2, axis=0, keepdims=True)
    i2 = jnp.min(jnp.where(sel2 == m2, row8, SUBLANES), axis=0, keepdims=True)
    r = jnp.exp(m2 - m1)
    den = 1.0 + r
    g0 = pgrp / den
    g1 = pgrp * r / den
    e0 = gsel * EXPERTS_PER_GROUP + i1
    e1 = gsel * EXPERTS_PER_GROUP + i2

    row_e = lax.broadcasted_iota(I32, (n_experts, ts), 0)
    oh0 = row_e == e0
    oh1 = row_e == e1
    oh = jnp.where(oh0 | oh1, 1.0, 0.0).astype(F32)
    prefix = _dot(oh.astype(BF16), tri_ref[...])
    tot = prefix + base_ref[:, 0:1]
    rank0 = jnp.sum(jnp.where(oh0, tot, 0.0), axis=0, keepdims=True).astype(I32)
    rank1 = jnp.sum(jnp.where(oh1, tot, 0.0), axis=0, keepdims=True).astype(I32)
    base_ref[...] = base_ref[...] + jnp.sum(oh, axis=1, keepdims=True)

    ids_ref[0] = jnp.where(row8 == 0, e0, jnp.where(row8 == 1, e1, jnp.where(row8 == 2, rank0, rank1)))
    gate_ref[0] = jnp.where(row8 == 0, g0, g1)
    cnt_ref[...] = base_ref[...].astype(I32)


def _router_operands(w_rg, b_rg, w_re, b_re, ts):
    d, n_groups = w_rg.shape
    n_experts = w_re.shape[1]
    assert n_experts == n_groups * EXPERTS_PER_GROUP and n_groups <= SUBLANES
    assert SUBLANES + n_experts <= ROUTER_ROWS
    wt = jnp.zeros((ROUTER_ROWS, d), F32)
    wt = wt.at[0:n_groups].set(w_rg.T).at[SUBLANES:SUBLANES + n_experts].set(w_re.T)
    bt = jnp.zeros((ROUTER_ROWS, 1), F32)
    bt = bt.at[0:n_groups, 0].set(b_rg).at[SUBLANES:SUBLANES + n_experts, 0].set(b_re)
    w_hi = wt.astype(BF16)
    w_lo = (wt - w_hi.astype(F32)).astype(BF16)
    idx = jnp.arange(ts)
    tri = (idx[:, None] < idx[None, :]).astype(BF16)
    return w_hi, w_lo, bt, tri


def _const_spec(shape):
    zeros = (0,) * len(shape)
    return pl.BlockSpec(shape, lambda b, s: zeros)


def _mixer_in(prev, x2d, ts, n_s, d):
    n_tiles = x2d.shape[0] // ts
    tile = pl.BlockSpec((ts, d), lambda b, s: (b * n_s + s, 0))
    if prev is None:
        return [x2d], [tile]
    yg, gates = prev
    specs = [
        tile,
        tile,
        pl.BlockSpec((ts, d), lambda b, s: (n_tiles + b * n_s + s, 0)),
        pl.BlockSpec((ts, MOE_TOP_K), lambda b, s: (b * n_s + s, 0)),
    ]
    return [x2d, yg, yg, gates], specs


def _router_in(nffn, router, d, ts):
    w_hi, w_lo, bt, tri = router
    specs = [
        _const_spec((1, d)),
        _const_spec((ROUTER_ROWS, d)),
        _const_spec((ROUTER_ROWS, d)),
        _const_spec((ROUTER_ROWS, 1)),
        _const_spec((ts, ts)),
    ]
    return [nffn.reshape(1, d), w_hi, w_lo, bt, tri], specs


def _mixer_out(t, d, ts, n_s):
    n_tiles = t // ts
    n_experts = ROUTER_GROUPS[1]
    out_shape = [
        jax.ShapeDtypeStruct((t, d), F32),
        jax.ShapeDtypeStruct((n_tiles, SUBLANES, ts), I32),
        jax.ShapeDtypeStruct((n_tiles, SUBLANES, ts), F32),
        jax.ShapeDtypeStruct((n_experts, LANES), I32),
    ]
    out_specs = [
        pl.BlockSpec((ts, d), lambda b, s: (b * n_s + s, 0)),
        pl.BlockSpec((1, SUBLANES, ts), lambda b, s: (b * n_s + s, 0, 0)),
        pl.BlockSpec((1, SUBLANES, ts), lambda b, s: (b * n_s + s, 0, 0)),
        pl.BlockSpec((n_experts, LANES), lambda b, s: (0, 0)),
    ]
    return out_shape, out_specs


def _mixer_call(kern, name, prev, x2d, bsz, seq, ts, mixer_ops, mixer_specs, nffn, router, scratch):
    t, d = x2d.shape
    n_s = seq // ts
    tok_ops, tok_specs = _mixer_in(prev, x2d, ts, n_s, d)
    r_ops, r_specs = _router_in(nffn, router, d, ts)
    out_shape, out_specs = _mixer_out(t, d, ts, n_s)
    return pl.pallas_call(
        functools.partial(kern, n_tok=len(tok_ops)),
        grid=(bsz, n_s),
        in_specs=tok_specs + mixer_specs + r_specs,
        out_specs=out_specs,
        out_shape=out_shape,
        scratch_shapes=scratch + [pltpu.VMEM((ROUTER_GROUPS[1], LANES), F32)],
        compiler_params=pltpu.CompilerParams(dimension_semantics=("arbitrary", "arbitrary"),
                                             vmem_limit_bytes=VMEM_LIMIT_BYTES),
        name=name,
    )(*tok_ops, *mixer_ops, *r_ops)


def _token_input(refs, n_tok):
    return refs[0][...] if n_tok == 1 else _combined(*refs[:n_tok])


def _pool_kernel(*refs, n_tok):
    x = _token_input(refs, n_tok)
    (nmix_ref, pw_ref, scale_ref,
     nffn_ref, wr_hi_ref, wr_lo_ref, br_ref, tri_ref,
     xm_ref, ids_ref, gate_ref, cnt_ref,
     carry_ref, base_ref) = refs[n_tok:]
    b = pl.program_id(0)
    s = pl.program_id(1)
    ts, d = x.shape
    ch = d // len(POOL_WINDOWS)

    @pl.when(s == 0)
    def _():
        carry_ref[...] = jnp.zeros_like(carry_ref)

    h = _rms(x, nmix_ref[...])
    ext = jnp.concatenate([carry_ref[...], h], axis=0)
    carry_ref[...] = h[ts - POOL_HALO:]

    pos = s * ts + lax.broadcasted_iota(I32, (ts, 1), 0) + 1
    ys = []
    for g, win in enumerate(POOL_WINDOWS):
        acc = ext[:, g * ch:(g + 1) * ch]
        k = 1
        while k < win:
            acc = acc + pltpu.roll(acc, k, axis=0)
            k *= 2
        inv_cnt = 1.0 / jnp.minimum(pos, win).astype(F32)
        pooled = acc[POOL_HALO:] * inv_cnt - h[:, g * ch:(g + 1) * ch]
        ys.append(_dot(pooled.astype(BF16), pw_ref[g]))
    y = jnp.concatenate(ys, axis=1) * scale_ref[...]
    xm = x + y
    xm_ref[...] = xm
    _route(xm, (b == 0) & (s == 0), nffn_ref, wr_hi_ref, wr_lo_ref, br_ref, tri_ref, base_ref,
           ids_ref, gate_ref, cnt_ref)


def _pool_layer(prev, x2d, bsz, seq, ts, nmix, pool_w, pool_scale, nffn, router):
    d = x2d.shape[1]
    n_win, ch, _ = pool_w.shape
    assert n_win == len(POOL_WINDOWS) and ch * n_win == d
    ops = [nmix.reshape(1, d), pool_w.astype(BF16), pool_scale.reshape(1, d)]
    specs = [_const_spec((1, d)), _const_spec((n_win, ch, ch)), _const_spec((1, d))]
    return _mixer_call(_pool_kernel, "pool_mixer", prev, x2d, bsz, seq, ts, ops, specs, nffn, router,
                       [pltpu.VMEM((POOL_HALO, d), F32)])


def _sconv_kernel(*refs, n_tok):
    x = _token_input(refs, n_tok)
    (nmix_ref, win_ref, taps_ref, wout_ref,
     nffn_ref, wr_hi_ref, wr_lo_ref, br_ref, tri_ref,
     xm_ref, ids_ref, gate_ref, cnt_ref,
     carry_ref, base_ref) = refs[n_tok:]
    b = pl.program_id(0)
    s = pl.program_id(1)
    ts, d = x.shape

    @pl.when(s == 0)
    def _():
        carry_ref[...] = jnp.zeros_like(carry_ref)

    h = _rms(x, nmix_ref[...]).astype(BF16)
    b_gate = _dot(h, win_ref[:, 0:d])
    c_gate = _dot(h, win_ref[:, d:2 * d])
    v = _dot(h, win_ref[:, 2 * d:3 * d])
    u_pre = c_gate * v
    ext = jnp.concatenate([carry_ref[...], u_pre], axis=0)
    carry_ref[...] = u_pre[ts - CONV_HALO:]
    width = taps_ref.shape[0]
    u = taps_ref[width - 1:width, :] * u_pre
    for k in range(1, width):
        u = u + taps_ref[width - 1 - k:width - k, :] * _shift_rows(ext, k, CONV_HALO)
    mix = _dot((b_gate * u).astype(BF16), wout_ref[...])
    xm = x + mix
    xm_ref[...] = xm
    _route(xm, (b == 0) & (s == 0), nffn_ref, wr_hi_ref, wr_lo_ref, br_ref, tri_ref, base_ref,
           ids_ref, gate_ref, cnt_ref)


def _sconv_layer(prev, x2d, bsz, seq, ts, nmix, w_in, taps, w_out, nffn, router):
    d = x2d.shape[1]
    width = taps.shape[0]
    assert width - 1 <= CONV_HALO
    ops = [nmix.reshape(1, d), w_in.astype(BF16), taps, w_out.astype(BF16)]
    specs = [_const_spec((1, d)), _const_spec((d, 3 * d)), _const_spec((width, d)), _const_spec((d, d))]
    return _mixer_call(_sconv_kernel, "sconv_mixer", prev, x2d, bsz, seq, ts, ops, specs, nffn, router,
                       [pltpu.VMEM((CONV_HALO, d), F32)])


def _ssm_kernel(*refs, n_tok):
    x = _token_input(refs, n_tok)
    (nmix_ref, wz_ref, wxbc_ref, wdt_hi_ref, wdt_lo_ref, convw_ref, convb_ref,
     dtb_ref, alog_ref, expand_ref, dskip_ref, normw_ref, wout_ref, triq_ref,
     nffn_ref, wr_hi_ref, wr_lo_ref, br_ref, tri_ref,
     xm_ref, ids_ref, gate_ref, cnt_ref,
     carry_ref, z_ref, xs_ref, bm_ref, cm_ref, y_ref, dtt_ref, at_ref, state_ref, base_ref) = refs[n_tok:]
    b = pl.program_id(0)
    s = pl.program_id(1)
    ts, d = x.shape
    di = z_ref.shape[1]
    gn = bm_ref.shape[1]
    cd = di + 2 * gn
    nh = alog_ref.shape[0]
    hp = di // nh
    grp = SSM_GROUPS
    st = gn // grp
    hpg = nh // grp
    gw = hpg * hp
    q = SSM_CHUNK
    n_chunks = ts // q

    @pl.when(s == 0)
    def _():
        carry_ref[...] = jnp.zeros_like(carry_ref)
        state_ref[...] = jnp.zeros_like(state_ref)

    h = _rms(x, nmix_ref[...])
    h_hi, h_lo = _hilo(h)

    z_ref[...] = _dot(h_hi, wz_ref[...])
    width = convw_ref.shape[0]
    col_blk = gn
    for c0 in range(0, cd, col_blk):
        cols = slice(c0, c0 + col_blk)
        pre = _dot(h_hi, wxbc_ref[:, cols])
        ext = jnp.concatenate([carry_ref[:, cols], pre], axis=0)
        carry_ref[:, cols] = pre[ts - CONV_HALO:]
        acc = convw_ref[width - 1:width, cols] * pre + convb_ref[:, cols]
        for k in range(1, width):
            acc = acc + convw_ref[width - 1 - k:width - k, cols] * _shift_rows(ext, k, CONV_HALO)
        act = _silu(acc)
        if c0 < di:
            xs_ref[:, cols] = act
        elif c0 < di + gn:
            bm_ref[...] = act
        else:
            cm_ref[...] = act

    dt_t = _dot_nt_hilo(wdt_hi_ref[...], wdt_lo_ref[...], h_hi, h_lo) + dtb_ref[...]
    dt_t = jnp.maximum(dt_t, 0.0) + jnp.log(1.0 + jnp.exp(-jnp.abs(dt_t)))
    a_t = dt_t * (-jnp.exp(alog_ref[...]))
    for c in range(n_chunks):
        dtt_ref[c] = dt_t[:, c * q:(c + 1) * q]
        at_ref[c] = a_t[:, c * q:(c + 1) * q]

    row_q = lax.broadcasted_iota(I32, (q, q), 0)
    col_q = lax.broadcasted_iota(I32, (q, q), 1)
    causal = row_q >= col_q
    lane_lo = lax.broadcasted_iota(I32, (q, 2 * hp), 1) < hp
    neg_inf = jnp.float32(-jnp.inf)
    expand = expand_ref[...]

    def chunk_body(c, carry):
        r0 = pl.multiple_of(c * q, q)
        rows = pl.ds(r0, q)
        acs_t = _dot3(at_ref[c], triq_ref[...])
        acs = acs_t.T
        dt_c = dtt_ref[c].T
        last = acs[q - 1:q, :]
        dt_e = _dot3(dt_c, expand)
        out_decay_e = _dot3(jnp.exp(acs), expand)
        st_decay_e = _dot3(jnp.exp(last - acs), expand)
        xc = xs_ref[rows, :]
        xdt = xc * dt_e
        bc = bm_ref[rows, :]
        cc = cm_ref[rows, :]
        for g in range(grp):
            gcols = slice(g * gw, (g + 1) * gw)
            bg = bc[:, g * st:(g + 1) * st].astype(BF16)
            cg = cc[:, g * st:(g + 1) * st].astype(BF16)
            xg = xdt[:, gcols]
            cb = _dot_nt(cg, bg)
            hprev = state_ref[g]
            y_off = _dot(cg, hprev.astype(BF16)) * out_decay_e[:, gcols]
            new_states = lax.dot_general(bg, (xg * st_decay_e[:, gcols]).astype(BF16), TN_DIMS,
                                         preferred_element_type=F32)
            state_ref[g] = hprev * out_decay_e[q - 1:q, gcols] + new_states
            for j in range(hpg // 2):
                h0 = g * hpg + 2 * j
                ms = []
                for hh in (h0, h0 + 1):
                    seg = acs[:, hh:hh + 1] - acs_t[hh:hh + 1, :]
                    ms.append((cb * jnp.exp(jnp.where(causal, seg, neg_inf))).astype(BF16))
                lhs = jnp.concatenate(ms, axis=1)
                xp = xg[:, 2 * j * hp:(2 * j + 2) * hp]
                rhs = jnp.concatenate([jnp.where(lane_lo, xp, 0.0), jnp.where(lane_lo, 0.0, xp)],
                                      axis=0).astype(BF16)
                pc = slice(g * gw + 2 * j * hp, g * gw + (2 * j + 2) * hp)
                y_ref[rows, pc] = _dot(lhs, rhs) + y_off[:, 2 * j * hp:(2 * j + 2) * hp]
        return carry

    lax.fori_loop(0, n_chunks, chunk_body, 0)

    out = None
    for g in range(grp):
        gcols = slice(g * gw, (g + 1) * gw)
        yg = (y_ref[:, gcols] + xs_ref[:, gcols] * dskip_ref[:, gcols]) * _silu(z_ref[:, gcols])
        yn = yg * lax.rsqrt(jnp.mean(yg * yg, axis=-1, keepdims=True) + NORM_EPS) * normw_ref[:, gcols]
        part = _dot(yn.astype(BF16), wout_ref[gcols, :])
        out = part if out is None else out + part
    xm = x + out
    xm_ref[...] = xm
    _route(xm, (b == 0) & (s == 0), nffn_ref, wr_hi_ref, wr_lo_ref, br_ref, tri_ref, base_ref,
           ids_ref, gate_ref, cnt_ref)


def _ssm_layer(prev, x2d, bsz, seq, ts, nmix, w_in, conv_w, conv_b, dt_bias, a_log, d_skip, norm_w, w_out,
               nffn, router):
    d = x2d.shape[1]
    assert ts % SSM_CHUNK == 0
    nh = a_log.shape[0]
    di = norm_w.shape[0]
    cd = conv_w.shape[1]
    gn = (cd - di) // 2
    hp = di // nh
    width = conv_w.shape[0]
    assert width - 1 <= CONV_HALO and (nh // SSM_GROUPS) % 2 == 0 and di % gn == 0
    st = gn // SSM_GROUPS
    gw = di // SSM_GROUPS
    q = SSM_CHUNK
    n_chunks = ts // q

    wz = w_in[:, :di].astype(BF16)
    wxbc = w_in[:, di:di + cd].astype(BF16)
    wdt_t = w_in[:, di + cd:].T
    wdt_hi = wdt_t.astype(BF16)
    wdt_lo = (wdt_t - wdt_hi.astype(F32)).astype(BF16)
    expand = (jnp.arange(di)[None, :] // hp == jnp.arange(nh)[:, None]).astype(BF16)
    dskip_e = jnp.repeat(d_skip, hp).reshape(1, di)
    idx = jnp.arange(q)
    triq = (idx[:, None] <= idx[None, :]).astype(BF16)

    ops = [nmix.reshape(1, d), wz, wxbc, wdt_hi, wdt_lo, conv_w, conv_b.reshape(1, cd),
           dt_bias.reshape(nh, 1), a_log.reshape(nh, 1), expand, dskip_e, norm_w.reshape(1, di),
           w_out.astype(BF16), triq]
    specs = [
        _const_spec((1, d)),
        _const_spec((d, di)),
        _const_spec((d, cd)),
        _const_spec((nh, d)),
        _const_spec((nh, d)),
        _const_spec((width, cd)),
        _const_spec((1, cd)),
        _const_spec((nh, 1)),
        _const_spec((nh, 1)),
        _const_spec((nh, di)),
        _const_spec((1, di)),
        _const_spec((1, di)),
        _const_spec((di, d)),
        _const_spec((q, q)),
    ]
    scratch = [
        pltpu.VMEM((CONV_HALO, cd), F32),
        pltpu.VMEM((ts, di), F32),
        pltpu.VMEM((ts, di), F32),
        pltpu.VMEM((ts, gn), F32),
        pltpu.VMEM((ts, gn), F32),
        pltpu.VMEM((ts, di), F32),
        pltpu.VMEM((n_chunks, nh, q), F32),
        pltpu.VMEM((n_chunks, nh, q), F32),
        pltpu.VMEM((SSM_GROUPS, st, gw), F32),
    ]
    return _mixer_call(_ssm_kernel, "ssm_mixer", prev, x2d, bsz, seq, ts, ops, specs, nffn, router, scratch)


def _sc_mesh():
    return plsc.VectorSubcoreMesh(core_axis_name="core", subcore_axis_name="subcore")


def _sc_worker_base(per_worker):
    return (lax.axis_index("subcore") * SC_CORES + lax.axis_index("core")) * per_worker


def _sc_scratch(d, dtype):
    win = SC_WINDOW
    return [pltpu.VMEM((win,), I32), pltpu.VMEM((win,), I32),
            pltpu.VMEM((win, d), dtype), pltpu.VMEM((win, d), dtype),
            pltpu.SemaphoreType.DMA, pltpu.SemaphoreType.DMA]


def _scatter_rows(src, idx, n_rows):
    n_src, d = src.shape
    m = idx.shape[0]
    win = SC_WINDOW
    per_worker = m // SC_WORKERS
    n_win = per_worker // win
    assert m == per_worker * SC_WORKERS and per_worker == n_win * win and n_win % 2 == 0
    assert n_src % per_worker == 0

    @functools.partial(pl.kernel, out_type=jax.ShapeDtypeStruct((n_rows, d), src.dtype), mesh=_sc_mesh(),
                       scratch_types=_sc_scratch(d, src.dtype), name="moe_scatter_rows")
    def scatter_kernel(x_hbm, i_hbm, o_hbm, idx_a, idx_b, rows_a, rows_b, sem_a, sem_b):
        base = _sc_worker_base(per_worker)
        src_base = lax.rem(base, n_src)

        def step(w, idx_v, rows_v, sem, first):
            if not first:
                pltpu.make_async_copy(rows_v, o_hbm.at[idx_v], sem).wait()
            pltpu.sync_copy(i_hbm.at[pl.ds(base + w * win, win)], idx_v)
            pltpu.sync_copy(x_hbm.at[pl.ds(src_base + w * win, win)], rows_v)
            pltpu.make_async_copy(rows_v, o_hbm.at[idx_v], sem).start()

        step(0, idx_a, rows_a, sem_a, True)
        step(1, idx_b, rows_b, sem_b, True)

        @pl.loop(2, n_win, step=2)
        def _(w):
            step(w, idx_a, rows_a, sem_a, False)
            step(w + 1, idx_b, rows_b, sem_b, False)

        pltpu.make_async_copy(rows_a, o_hbm.at[idx_a], sem_a).wait()
        pltpu.make_async_copy(rows_b, o_hbm.at[idx_b], sem_b).wait()

    return scatter_kernel(src, idx)


def _gather_rows(table, idx):
    d = table.shape[1]
    m = idx.shape[0]
    win = SC_WINDOW
    per_worker = m // SC_WORKERS
    n_win = per_worker // win
    assert m == per_worker * SC_WORKERS and per_worker == n_win * win and n_win % 2 == 0

    @functools.partial(pl.kernel, out_type=jax.ShapeDtypeStruct((m, d), table.dtype), mesh=_sc_mesh(),
                       scratch_types=_sc_scratch(d, table.dtype), name="moe_gather_rows")
    def gather_kernel(x_hbm, i_hbm, o_hbm, idx_a, idx_b, rows_a, rows_b, sem_a, sem_b):
        base = _sc_worker_base(per_worker)

        def out_copy(w, rows_v, sem):
            return pltpu.make_async_copy(rows_v, o_hbm.at[pl.ds(base + w * win, win)], sem)

        def step(w, idx_v, rows_v, sem, first):
            pltpu.sync_copy(i_hbm.at[pl.ds(base + w * win, win)], idx_v)
            if not first:
                out_copy(w, rows_v, sem).wait()
            pltpu.sync_copy(x_hbm.at[idx_v], rows_v)
            out_copy(w, rows_v, sem).start()

        step(0, idx_a, rows_a, sem_a, True)
        step(1, idx_b, rows_b, sem_b, True)

        @pl.loop(2, n_win, step=2)
        def _(w):
            step(w, idx_a, rows_a, sem_a, False)
            step(w + 1, idx_b, rows_b, sem_b, False)

        out_copy(0, rows_a, sem_a).wait()
        out_copy(0, rows_b, sem_b).wait()

    return gather_kernel(table, idx)


def _ffn_kernel(be_ref, nu_ref, nv_ref, xs_ref, nffn_ref, wgu_ref, wdn_ref, ys_ref, wgu_bf_ref, wdn_bf_ref):
    i = pl.program_id(0)
    bm = xs_ref.shape[0]
    f = wdn_ref.shape[2]
    prev = be_ref[jnp.maximum(i - 1, 0)]

    @pl.when((i == 0) | (be_ref[i] != prev))
    def _():
        wgu_bf_ref[...] = wgu_ref[0, 0].astype(BF16)
        wdn_bf_ref[...] = wdn_ref[0, 0].astype(BF16)

    @pl.when(i < nu_ref[0])
    def _():
        valid = lax.broadcasted_iota(I32, (bm, 1), 0) < nv_ref[i]
        h = _rms(jnp.where(valid, xs_ref[...], 0.0), nffn_ref[...])
        gu = _dot(h.astype(BF16), wgu_bf_ref[...])
        act = _silu(gu[:, :f]) * gu[:, f:]
        ys_ref[...] = _dot(act.astype(BF16), wdn_bf_ref[...])

    @pl.when(i >= nu_ref[0])
    def _():
        ys_ref[...] = jnp.zeros_like(ys_ref)


def _expert_ffn(xs, nffn, block_e, n_used, n_valid, w_gu_all, w_dn_all, layer):
    n_rows, d = xs.shape
    bm = FFN_BLOCK
    n_blocks = n_rows // bm
    f = w_dn_all.shape[2]

    def xs_map(i, be, nu, nv):
        return (jnp.minimum(i, jnp.maximum(nu[0] - 1, 0)), 0)

    return pl.pallas_call(
        _ffn_kernel,
        grid_spec=pltpu.PrefetchScalarGridSpec(
            num_scalar_prefetch=3,
            grid=(n_blocks,),
            in_specs=[
                pl.BlockSpec((bm, d), xs_map),
                pl.BlockSpec((1, d), lambda i, be, nu, nv: (0, 0)),
                pl.BlockSpec((1, 1, d, 2 * f), lambda i, be, nu, nv: (layer, be[i], 0, 0)),
                pl.BlockSpec((1, 1, f, d), lambda i, be, nu, nv: (layer, be[i], 0, 0)),
            ],
            out_specs=pl.BlockSpec((bm, d), lambda i, be, nu, nv: (i, 0)),
            scratch_shapes=[pltpu.VMEM((d, 2 * f), BF16), pltpu.VMEM((f, d), BF16)],
        ),
        out_shape=jax.ShapeDtypeStruct((n_rows, d), F32),
        compiler_params=pltpu.CompilerParams(dimension_semantics=("arbitrary",),
                                             vmem_limit_bytes=VMEM_LIMIT_BYTES),
        name="expert_ffn",
    )(block_e, n_used, n_valid, xs, nffn.reshape(1, d), w_gu_all, w_dn_all)


def _moe(xm, ids, gate, counts, nffn, w_gu_all, w_dn_all, layer):
    t, d = xm.shape
    n_exp = w_gu_all.shape[1]
    bm = FFN_BLOCK
    n_assign = t * MOE_TOP_K
    n_rows = (n_assign + bm - 1) // bm * bm + n_exp * bm
    n_blocks = n_rows // bm

    expert = jnp.transpose(ids[:, 0:2, :], (1, 0, 2)).reshape(MOE_TOP_K, t)
    rank = jnp.transpose(ids[:, 2:4, :], (1, 0, 2)).reshape(MOE_TOP_K, t)
    gates = jnp.transpose(gate[:, 0:2, :], (0, 2, 1)).reshape(t, MOE_TOP_K)
    cnt = counts[:, 0]
    padded = (cnt + bm - 1) // bm * bm
    pad_end = jnp.cumsum(padded)
    pad_start = pad_end - padded
    e_ids = jnp.arange(n_exp, dtype=I32)
    start_of = jnp.sum(jnp.where(expert[..., None] == e_ids, pad_start, 0), axis=-1)
    dest = (start_of + rank).astype(I32).reshape(-1)
    blk_row = jnp.arange(n_blocks, dtype=I32) * bm
    block_e = jnp.minimum(jnp.sum(blk_row[:, None] >= pad_end[None, :], axis=1), n_exp - 1).astype(I32)
    n_used = (pad_end[-1:] // bm).astype(I32)
    n_valid = jnp.clip(pad_start[block_e] + cnt[block_e] - blk_row, 0, bm).astype(I32)

    xs = _scatter_rows(xm, dest, n_rows)
    ys = _expert_ffn(xs, nffn, block_e, n_used, n_valid, w_gu_all, w_dn_all, layer)
    return _gather_rows(ys, dest), gates


def _final_kernel(xm_ref, y0_ref, y1_ref, g_ref, nfin_ref, out_ref):
    out_ref[...] = _rms(_combined(xm_ref, y0_ref, y1_ref, g_ref), nfin_ref[...])


def _final(xm, yg, gates, norm_final):
    t, d = xm.shape
    ts = min(TS_FINAL, t)
    n_tiles = t // ts
    tile = pl.BlockSpec((ts, d), lambda i: (i, 0))
    return pl.pallas_call(
        _final_kernel,
        grid=(n_tiles,),
        in_specs=[tile, tile, pl.BlockSpec((ts, d), lambda i: (n_tiles + i, 0)),
                  pl.BlockSpec((ts, MOE_TOP_K), lambda i: (i, 0)), pl.BlockSpec((1, d), lambda i: (0, 0))],
        out_specs=tile,
        out_shape=jax.ShapeDtypeStruct((t, d), F32),
        compiler_params=pltpu.CompilerParams(dimension_semantics=("arbitrary",),
                                             vmem_limit_bytes=VMEM_LIMIT_BYTES),
        name="final_combine_norm",
    )(xm, yg, yg, gates, norm_final.reshape(1, d))


def kernel(x, norm_mix, norm_ffn, norm_final, pool_w, pool_scale, sconv_in_w, sconv_taps, sconv_out_w, ssm_in_w, ssm_conv_w, ssm_conv_b, ssm_dt_bias, ssm_a_log, ssm_d, ssm_norm_w, ssm_out_w, router_group_w, router_group_b, router_expert_w, router_expert_b, expert_w_gu, expert_w_down):
    bsz, seq, d = x.shape
    depth = norm_mix.shape[0]
    assert (router_group_w.shape[2], router_expert_w.shape[2]) == ROUTER_GROUPS
    xm = x.reshape(bsz * seq, d)
    prev = None
    for i in range(depth):
        kind, j = i % N_MIXERS, i // N_MIXERS
        ts = min((TS_POOL, TS_SCONV, TS_SSM)[kind], seq)
        router = _router_operands(router_group_w[i], router_group_b[i], router_expert_w[i],
                                  router_expert_b[i], ts)
        if kind == 0:
            outs = _pool_layer(prev, xm, bsz, seq, ts, norm_mix[i], pool_w[j], pool_scale[j], norm_ffn[i], router)
        elif kind == 1:
            outs = _sconv_layer(prev, xm, bsz, seq, ts, norm_mix[i], sconv_in_w[j], sconv_taps[j],
                                sconv_out_w[j], norm_ffn[i], router)
        else:
            outs = _ssm_layer(prev, xm, bsz, seq, ts, norm_mix[i], ssm_in_w[j], ssm_conv_w[j], ssm_conv_b[j],
                              ssm_dt_bias[j], ssm_a_log[j], ssm_d[j], ssm_norm_w[j], ssm_out_w[j],
                              norm_ffn[i], router)
        xm, ids, gate, counts = outs
        prev = _moe(xm, ids, gate, counts, norm_ffn[i], expert_w_gu, expert_w_down, i)
    return _final(xm, prev[0], prev[1], norm_final).reshape(bsz, seq, d)
```

```python
import functools

import jax
import jax.numpy as jnp
from jax import lax
from jax.experimental import pallas as pl
from jax.experimental.pallas import tpu as pltpu
from jax.experimental.pallas import tpu_sc as plsc

F32 = jnp.float32
BF16 = jnp.bfloat16
I32 = jnp.int32

NORM_EPS = 1e-6
N_MIXERS = 3
POOL_WINDOWS = (2, 4, 8, 16)
POOL_HALO = 16
CONV_HALO = 8
SSM_GROUPS = 4
SSM_CHUNK = 128
MOE_TOP_K = 2
EXPERTS_PER_GROUP = 8
ROUTER_GROUPS = (4, 32)
ROUTER_ROWS = 48

SUBLANES = 8
LANES = 128
VMEM_LIMIT_BYTES = 56 * 1024 * 1024
SC_CORES = 2
SC_WORKERS = SC_CORES * 16
SC_WINDOW = 64

TS_POOL = 512
TS_SCONV = 512
TS_SSM = 256
TS_FINAL = 512
FFN_BLOCK = 256

NT_DIMS = (((1,), (1,)), ((), ()))
TN_DIMS = (((0,), (0,)), ((), ()))


def _dot(a, b):
    return jnp.dot(a, b, preferred_element_type=F32)


def _dot_nt(a, b):
    return lax.dot_general(a, b, NT_DIMS, preferred_element_type=F32)


def _split3(a):
    a1 = a.astype(BF16)
    r1 = a - a1.astype(F32)
    a2 = r1.astype(BF16)
    a3 = (r1 - a2.astype(F32)).astype(BF16)
    return a1, a2, a3


def _dot3(a, b):
    a1, a2, a3 = _split3(a)
    return _dot(a1, b) + _dot(a2, b) + _dot(a3, b)


def _hilo(a):
    hi = a.astype(BF16)
    lo = (a - hi.astype(F32)).astype(BF16)
    return hi, lo


def _dot_nt_hilo(w_hi, w_lo, a_hi, a_lo):
    return _dot_nt(w_hi, a_hi) + _dot_nt(w_hi, a_lo) + _dot_nt(w_lo, a_hi)


def _rms(x, w):
    return x * lax.rsqrt(jnp.mean(x * x, axis=-1, keepdims=True) + NORM_EPS) * w


def _silu(x):
    return x * (1.0 / (1.0 + jnp.exp(-x)))


def _shift_rows(ext, k, halo):
    return pltpu.roll(ext, k, axis=0)[halo:]


HI_HALF_MASK = -65536


def _pack_bf16(a_bf):
    half = a_bf.shape[1] // 2
    lo = lax.bitcast_convert_type(a_bf[:, :half].astype(F32), I32)
    hi = lax.bitcast_convert_type(a_bf[:, half:].astype(F32), I32)
    return lax.shift_right_logical(lo, 16) | (hi & HI_HALF_MASK)


def _unpack_bf16(w):
    lo = lax.bitcast_convert_type(lax.shift_left(w, 16), F32)
    hi = lax.bitcast_convert_type(w & HI_HALF_MASK, F32)
    return jnp.concatenate([lo, hi], axis=1)


def _combined(xm_ref, y0_ref, y1_ref, g_ref):
    g = g_ref[...]
    return xm_ref[...] + g[:, 0:1] * _unpack_bf16(y0_ref[...]) + g[:, 1:2] * _unpack_bf16(y1_ref[...])


def _route(xm, first, nffn_ref, wr_hi_ref, wr_lo_ref, br_ref, tri_ref, base_ref, hp_ref, ids_ref, gate_ref,
           cnt_ref):
    n_groups, n_experts = ROUTER_GROUPS
    ts = xm.shape[0]

    @pl.when(first)
    def _():
        base_ref[...] = jnp.zeros_like(base_ref)

    h2 = _rms(xm, nffn_ref[...])
    h_hi, h_lo = _hilo(h2)
    hp_ref[...] = _pack_bf16(h_hi)
    lt = _dot_nt_hilo(wr_hi_ref[...], wr_lo_ref[...], h_hi, h_lo) + br_ref[...]

    row8 = lax.broadcasted_iota(I32, (SUBLANES, ts), 0)
    neg_inf = jnp.float32(-jnp.inf)
    g = jnp.where(row8 < n_groups, lt[0:SUBLANES], neg_inf)
    gmax = jnp.max(g, axis=0, keepdims=True)
    gsel = jnp.min(jnp.where(g == gmax, row8, SUBLANES), axis=0, keepdims=True)
    pgrp = 1.0 / jnp.sum(jnp.exp(g - gmax), axis=0, keepdims=True)

    sel = lt[SUBLANES:2 * SUBLANES]
    for j in range(1, n_groups):
        sel = jnp.where(gsel == j, lt[(j + 1) * SUBLANES:(j + 2) * SUBLANES], sel)
    m1 = jnp.max(sel, axis=0, keepdims=True)
    i1 = jnp.min(jnp.where(sel == m1, row8, SUBLANES), axis=0, keepdims=True)
    sel2 = jnp.where(row8 == i1, neg_inf, sel)
    m2 = jnp.max(sel2, axis=0, keepdims=True)
    i2 = jnp.min(jnp.where(sel2 == m2, row8, SUBLANES), axis=0, keepdims=True)
    r = jnp.exp(m2 - m1)
    den = 1.0 + r
    g0 = pgrp / den
    g1 = pgrp * r / den
    e0 = gsel * EXPERTS_PER_GROUP + i1
    e1 = gsel * EXPERTS_PER_GROUP + i2

    row_e = lax.broadcasted_iota(I32, (n_experts, ts), 0)
    oh0 = row_e == e0
    oh1 = row_e == e1
    oh = jnp.where(oh0 | oh1, 1.0, 0.0).astype(F32)
    prefix = _dot(oh.astype(BF16), tri_ref[...])
    tot = prefix + base_ref[:, 0:1]
    rank0 = jnp.sum(jnp.where(oh0, tot, 0.0), axis=0, keepdims=True).astype(I32)
    rank1 = jnp.sum(jnp.where(oh1, tot, 0.0), axis=0, keepdims=True).astype(I32)
    base_ref[...] = base_ref[...] + jnp.sum(oh, axis=1, keepdims=True)

    ids_ref[0] = jnp.where(row8 == 0, e0, jnp.where(row8 == 1, e1, jnp.where(row8 == 2, rank0, rank1)))
    gate_ref[0] = jnp.where(row8 == 0, g0, g1)
    cnt_ref[...] = base_ref[...].astype(I32)


def _router_operands(w_rg, b_rg, w_re, b_re, ts):
    d, n_groups = w_rg.shape
    n_experts = w_re.shape[1]
    assert n_experts == n_groups * EXPERTS_PER_GROUP and n_groups <= SUBLANES
    assert SUBLANES + n_experts <= ROUTER_ROWS
    wt = jnp.zeros((ROUTER_ROWS, d), F32)
    wt = wt.at[0:n_groups].set(w_rg.T).at[SUBLANES:SUBLANES + n_experts].set(w_re.T)
    bt = jnp.zeros((ROUTER_ROWS, 1), F32)
    bt = bt.at[0:n_groups, 0].set(b_rg).at[SUBLANES:SUBLANES + n_experts, 0].set(b_re)
    w_hi = wt.astype(BF16)
    w_lo = (wt - w_hi.astype(F32)).astype(BF16)
    idx = jnp.arange(ts)
    tri = (idx[:, None] < idx[None, :]).astype(BF16)
    return w_hi, w_lo, bt, tri


def _const_spec(shape):
    zeros = (0,) * len(shape)
    return pl.BlockSpec(shape, lambda b, s: zeros)


def _mixer_in(prev, x2d, ts, n_s, d):
    n_tiles = x2d.shape[0] // ts
    tile = pl.BlockSpec((ts, d), lambda b, s: (b * n_s + s, 0))
    if prev is None:
        return [x2d], [tile]
    yg, gates = prev
    specs = [
        tile,
        pl.BlockSpec((ts, d // 2), lambda b, s: (b * n_s + s, 0)),
        pl.BlockSpec((ts, d // 2), lambda b, s: (n_tiles + b * n_s + s, 0)),
        pl.BlockSpec((ts, MOE_TOP_K), lambda b, s: (b * n_s + s, 0)),
    ]
    return [x2d, yg, yg, gates], specs


def _router_in(nffn, router, d, ts):
    w_hi, w_lo, bt, tri = router
    specs = [
        _const_spec((1, d)),
        _const_spec((ROUTER_ROWS, d)),
        _const_spec((ROUTER_ROWS, d)),
        _const_spec((ROUTER_ROWS, 1)),
        _const_spec((ts, ts)),
    ]
    return [nffn.reshape(1, d), w_hi, w_lo, bt, tri], specs


def _mixer_out(t, d, ts, n_s):
    n_tiles = t // ts
    n_experts = ROUTER_GROUPS[1]
    out_shape = [
        jax.ShapeDtypeStruct((t, d), F32),
        jax.ShapeDtypeStruct((t, d // 2), I32),
        jax.ShapeDtypeStruct((n_tiles, SUBLANES, ts), I32),
        jax.ShapeDtypeStruct((n_tiles, SUBLANES, ts), F32),
        jax.ShapeDtypeStruct((n_experts, LANES), I32),
    ]
    out_specs = [
        pl.BlockSpec((ts, d), lambda b, s: (b * n_s + s, 0)),
        pl.BlockSpec((ts, d // 2), lambda b, s: (b * n_s + s, 0)),
        pl.BlockSpec((1, SUBLANES, ts), lambda b, s: (b * n_s + s, 0, 0)),
        pl.BlockSpec((1, SUBLANES, ts), lambda b, s: (b * n_s + s, 0, 0)),
        pl.BlockSpec((n_experts, LANES), lambda b, s: (0, 0)),
    ]
    return out_shape, out_specs


def _mixer_call(kern, name, prev, x2d, bsz, seq, ts, mixer_ops, mixer_specs, nffn, router, scratch):
    t, d = x2d.shape
    n_s = seq // ts
    tok_ops, tok_specs = _mixer_in(prev, x2d, ts, n_s, d)
    r_ops, r_specs = _router_in(nffn, router, d, ts)
    out_shape, out_specs = _mixer_out(t, d, ts, n_s)
    return pl.pallas_call(
        functools.partial(kern, n_tok=len(tok_ops)),
        grid=(bsz, n_s),
        in_specs=tok_specs + mixer_specs + r_specs,
        out_specs=out_specs,
        out_shape=out_shape,
        scratch_shapes=scratch + [pltpu.VMEM((ROUTER_GROUPS[1], LANES), F32)],
        compiler_params=pltpu.CompilerParams(dimension_semantics=("arbitrary", "arbitrary"),
                                             vmem_limit_bytes=VMEM_LIMIT_BYTES),
        name=name,
    )(*tok_ops, *mixer_ops, *r_ops)


def _token_input(refs, n_tok):
    return refs[0][...] if n_tok == 1 else _combined(*refs[:n_tok])


def _pool_kernel(*refs, n_tok):
    x = _token_input(refs, n_tok)
    (nmix_ref, pw_ref, scale_ref,
     nffn_ref, wr_hi_ref, wr_lo_ref, br_ref, tri_ref,
     xm_ref, hp_ref, ids_ref, gate_ref, cnt_ref,
     carry_ref, base_ref) = refs[n_tok:]
    b = pl.program_id(0)
    s = pl.program_id(1)
    ts, d = x.shape
    ch = d // len(POOL_WINDOWS)

    @pl.when(s == 0)
    def _():
        carry_ref[...] = jnp.zeros_like(carry_ref)

    h = _rms(x, nmix_ref[...])
    ext = jnp.concatenate([carry_ref[...], h], axis=0)
    carry_ref[...] = h[ts - POOL_HALO:]

    pos = s * ts + lax.broadcasted_iota(I32, (ts, 1), 0) + 1
    ys = []
    for g, win in enumerate(POOL_WINDOWS):
        acc = ext[:, g * ch:(g + 1) * ch]
        k = 1
        while k < win:
            acc = acc + pltpu.roll(acc, k, axis=0)
            k *= 2
        inv_cnt = 1.0 / jnp.minimum(pos, win).astype(F32)
        pooled = acc[POOL_HALO:] * inv_cnt - h[:, g * ch:(g + 1) * ch]
        ys.append(_dot(pooled.astype(BF16), pw_ref[g]))
    y = jnp.concatenate(ys, axis=1) * scale_ref[...]
    xm = x + y
    xm_ref[...] = xm
    _route(xm, (b == 0) & (s == 0), nffn_ref, wr_hi_ref, wr_lo_ref, br_ref, tri_ref, base_ref,
           hp_ref, ids_ref, gate_ref, cnt_ref)


def _pool_layer(prev, x2d, bsz, seq, ts, nmix, pool_w, pool_scale, nffn, router):
    d = x2d.shape[1]
    n_win, ch, _ = pool_w.shape
    assert n_win == len(POOL_WINDOWS) and ch * n_win == d
    ops = [nmix.reshape(1, d), pool_w.astype(BF16), pool_scale.reshape(1, d)]
    specs = [_const_spec((1, d)), _const_spec((n_win, ch, ch)), _const_spec((1, d))]
    return _mixer_call(_pool_kernel, "pool_mixer", prev, x2d, bsz, seq, ts, ops, specs, nffn, router,
                       [pltpu.VMEM((POOL_HALO, d), F32)])


def _sconv_kernel(*refs, n_tok):
    x = _token_input(refs, n_tok)
    (nmix_ref, win_ref, taps_ref, wout_ref,
     nffn_ref, wr_hi_ref, wr_lo_ref, br_ref, tri_ref,
     xm_ref, hp_ref, ids_ref, gate_ref, cnt_ref,
     carry_ref, base_ref) = refs[n_tok:]
    b = pl.program_id(0)
    s = pl.program_id(1)
    ts, d = x.shape

    @pl.when(s == 0)
    def _():
        carry_ref[...] = jnp.zeros_like(carry_ref)

    h = _rms(x, nmix_ref[...]).astype(BF16)
    b_gate = _dot(h, win_ref[:, 0:d])
    c_gate = _dot(h, win_ref[:, d:2 * d])
    v = _dot(h, win_ref[:, 2 * d:3 * d])
    u_pre = c_gate * v
    ext = jnp.concatenate([carry_ref[...], u_pre], axis=0)
    carry_ref[...] = u_pre[ts - CONV_HALO:]
    width = taps_ref.shape[0]
    u = taps_ref[width - 1:width, :] * u_pre
    for k in range(1, width):
        u = u + taps_ref[width - 1 - k:width - k, :] * _shift_rows(ext, k, CONV_HALO)
    mix = _dot((b_gate * u).astype(BF16), wout_ref[...])
    xm = x + mix
    xm_ref[...] = xm
    _route(xm, (b == 0) & (s == 0), nffn_ref, wr_hi_ref, wr_lo_ref, br_ref, tri_ref, base_ref,
           hp_ref, ids_ref, gate_ref, cnt_ref)


def _sconv_layer(prev, x2d, bsz, seq, ts, nmix, w_in, taps, w_out, nffn, router):
    d = x2d.shape[1]
    width = taps.shape[0]
    assert width - 1 <= CONV_HALO
    ops = [nmix.reshape(1, d), w_in.astype(BF16), taps, w_out.astype(BF16)]
    specs = [_const_spec((1, d)), _const_spec((d, 3 * d)), _const_spec((width, d)), _const_spec((d, d))]
    return _mixer_call(_sconv_kernel, "sconv_mixer", prev, x2d, bsz, seq, ts, ops, specs, nffn, router,
                       [pltpu.VMEM((CONV_HALO, d), F32)])


def _ssm_kernel(*refs, n_tok):
    x = _token_input(refs, n_tok)
    (nmix_ref, wz_ref, wxbc_ref, wdt_hi_ref, wdt_lo_ref, convw_ref, convb_ref,
     dtb_ref, alog_ref, expand_ref, dskip_ref, normw_ref, wout_ref, triq_ref,
     nffn_ref, wr_hi_ref, wr_lo_ref, br_ref, tri_ref,
     xm_ref, hp_ref, ids_ref, gate_ref, cnt_ref,
     carry_ref, z_ref, xs_ref, bm_ref, cm_ref, y_ref, dtt_ref, at_ref, state_ref, base_ref) = refs[n_tok:]
    b = pl.program_id(0)
    s = pl.program_id(1)
    ts, d = x.shape
    di = z_ref.shape[1]
    gn = bm_ref.shape[1]
    cd = di + 2 * gn
    nh = alog_ref.shape[0]
    hp = di // nh
    grp = SSM_GROUPS
    st = gn // grp
    hpg = nh // grp
    gw = hpg * hp
    q = SSM_CHUNK
    n_chunks = ts // q

    @pl.when(s == 0)
    def _():
        carry_ref[...] = jnp.zeros_like(carry_ref)
        state_ref[...] = jnp.zeros_like(state_ref)

    h = _rms(x, nmix_ref[...])
    h_hi, h_lo = _hilo(h)

    z_ref[...] = _dot(h_hi, wz_ref[...])
    width = convw_ref.shape[0]
    col_blk = gn
    for c0 in range(0, cd, col_blk):
        cols = slice(c0, c0 + col_blk)
        pre = _dot(h_hi, wxbc_ref[:, cols])
        ext = jnp.concatenate([carry_ref[:, cols], pre], axis=0)
        carry_ref[:, cols] = pre[ts - CONV_HALO:]
        acc = convw_ref[width - 1:width, cols] * pre + convb_ref[:, cols]
        for k in range(1, width):
            acc = acc + convw_ref[width - 1 - k:width - k, cols] * _shift_rows(ext, k, CONV_HALO)
        act = _silu(acc)
        if c0 < di:
            xs_ref[:, cols] = act
        elif c0 < di + gn:
            bm_ref[...] = act
        else:
            cm_ref[...] = act

    dt_t = _dot_nt_hilo(wdt_hi_ref[...], wdt_lo_ref[...], h_hi, h_lo) + dtb_ref[...]
    dt_t = jnp.maximum(dt_t, 0.0) + jnp.log(1.0 + jnp.exp(-jnp.abs(dt_t)))
    a_t = dt_t * (-jnp.exp(alog_ref[...]))
    for c in range(n_chunks):
        dtt_ref[c] = dt_t[:, c * q:(c + 1) * q]
        at_ref[c] = a_t[:, c * q:(c + 1) * q]

    row_q = lax.broadcasted_iota(I32, (q, q), 0)
    col_q = lax.broadcasted_iota(I32, (q, q), 1)
    causal = row_q >= col_q
    lane_lo = lax.broadcasted_iota(I32, (q, 2 * hp), 1) < hp
    neg_inf = jnp.float32(-jnp.inf)
    expand = expand_ref[...]

    def chunk_body(c, carry):
        r0 = pl.multiple_of(c * q, q)
        rows = pl.ds(r0, q)
        acs_t = _dot3(at_ref[c], triq_ref[...])
        acs = acs_t.T
        dt_c = dtt_ref[c].T
        last = acs[q - 1:q, :]
        dt_e = _dot3(dt_c, expand)
        out_decay_e = _dot3(jnp.exp(acs), expand)
        st_decay_e = _dot3(jnp.exp(last - acs), expand)
        xc = xs_ref[rows, :]
        xdt = xc * dt_e
        bc = bm_ref[rows, :]
        cc = cm_ref[rows, :]
        for g in range(grp):
            gcols = slice(g * gw, (g + 1) * gw)
            bg = bc[:, g * st:(g + 1) * st].astype(BF16)
            cg = cc[:, g * st:(g + 1) * st].astype(BF16)
            xg = xdt[:, gcols]
            cb = _dot_nt(cg, bg)
            hprev = state_ref[g]
            y_off = _dot(cg, hprev.astype(BF16)) * out_decay_e[:, gcols]
            new_states = lax.dot_general(bg, (xg * st_decay_e[:, gcols]).astype(BF16), TN_DIMS,
                                         preferred_element_type=F32)
            state_ref[g] = hprev * out_decay_e[q - 1:q, gcols] + new_states
            for j in range(hpg // 2):
                h0 = g * hpg + 2 * j
                ms = []
                for hh in (h0, h0 + 1):
                    seg = acs[:, hh:hh + 1] - acs_t[hh:hh + 1, :]
                    ms.append((cb * jnp.exp(jnp.where(causal, seg, neg_inf))).astype(BF16))
                lhs = jnp.concatenate(ms, axis=1)
                xp = xg[:, 2 * j * hp:(2 * j + 2) * hp]
                rhs = jnp.concatenate([jnp.where(lane_lo, xp, 0.0), jnp.where(lane_lo, 0.0, xp)],
                                      axis=0).astype(BF16)
                pc = slice(g * gw + 2 * j * hp, g * gw + (2 * j + 2) * hp)
                y_ref[rows, pc] = _dot(lhs, rhs) + y_off[:, 2 * j * hp:(2 * j + 2) * hp]
        return carry

    lax.fori_loop(0, n_chunks, chunk_body, 0)

    out = None
    for g in range(grp):
        gcols = slice(g * gw, (g + 1) * gw)
        yg = (y_ref[:, gcols] + xs_ref[:, gcols] * dskip_ref[:, gcols]) * _silu(z_ref[:, gcols])
        yn = yg * lax.rsqrt(jnp.mean(yg * yg, axis=-1, keepdims=True) + NORM_EPS) * normw_ref[:, gcols]
        part = _dot(yn.astype(BF16), wout_ref[gcols, :])
        out = part if out is None else out + part
    xm = x + out
    xm_ref[...] = xm
    _route(xm, (b == 0) & (s == 0), nffn_ref, wr_hi_ref, wr_lo_ref, br_ref, tri_ref, base_ref,
           hp_ref, ids_ref, gate_ref, cnt_ref)


def _ssm_layer(prev, x2d, bsz, seq, ts, nmix, w_in, conv_w, conv_b, dt_bias, a_log, d_skip, norm_w, w_out,
               nffn, router):
    d = x2d.shape[1]
    assert ts % SSM_CHUNK == 0
    nh = a_log.shape[0]
    di = norm_w.shape[0]
    cd = conv_w.shape[1]
    gn = (cd - di) // 2
    hp = di // nh
    width = conv_w.shape[0]
    assert width - 1 <= CONV_HALO and (nh // SSM_GROUPS) % 2 == 0 and di % gn == 0
    st = gn // SSM_GROUPS
    gw = di // SSM_GROUPS
    q = SSM_CHUNK
    n_chunks = ts // q

    wz = w_in[:, :di].astype(BF16)
    wxbc = w_in[:, di:di + cd].astype(BF16)
    wdt_t = w_in[:, di + cd:].T
    wdt_hi = wdt_t.astype(BF16)
    wdt_lo = (wdt_t - wdt_hi.astype(F32)).astype(BF16)
    expand = (jnp.arange(di)[None, :] // hp == jnp.arange(nh)[:, None]).astype(BF16)
    dskip_e = jnp.repeat(d_skip, hp).reshape(1, di)
    idx = jnp.arange(q)
    triq = (idx[:, None] <= idx[None, :]).astype(BF16)

    ops = [nmix.reshape(1, d), wz, wxbc, wdt_hi, wdt_lo, conv_w, conv_b.reshape(1, cd),
           dt_bias.reshape(nh, 1), a_log.reshape(nh, 1), expand, dskip_e, norm_w.reshape(1, di),
           w_out.astype(BF16), triq]
    specs = [
        _const_spec((1, d)),
        _const_spec((d, di)),
        _const_spec((d, cd)),
        _const_spec((nh, d)),
        _const_spec((nh, d)),
        _const_spec((width, cd)),
        _const_spec((1, cd)),
        _const_spec((nh, 1)),
        _const_spec((nh, 1)),
        _const_spec((nh, di)),
        _const_spec((1, di)),
        _const_spec((1, di)),
        _const_spec((di, d)),
        _const_spec((q, q)),
    ]
    scratch = [
        pltpu.VMEM((CONV_HALO, cd), F32),
        pltpu.VMEM((ts, di), F32),
        pltpu.VMEM((ts, di), F32),
        pltpu.VMEM((ts, gn), F32),
        pltpu.VMEM((ts, gn), F32),
        pltpu.VMEM((ts, di), F32),
        pltpu.VMEM((n_chunks, nh, q), F32),
        pltpu.VMEM((n_chunks, nh, q), F32),
        pltpu.VMEM((SSM_GROUPS, st, gw), F32),
    ]
    return _mixer_call(_ssm_kernel, "ssm_mixer", prev, x2d, bsz, seq, ts, ops, specs, nffn, router, scratch)


def _sc_mesh():
    return plsc.VectorSubcoreMesh(core_axis_name="core", subcore_axis_name="subcore")


def _sc_worker_base(per_worker):
    return (lax.axis_index("subcore") * SC_CORES + lax.axis_index("core")) * per_worker


def _sc_scratch(d, dtype):
    win = SC_WINDOW
    return [pltpu.VMEM((win,), I32), pltpu.VMEM((win,), I32),
            pltpu.VMEM((win, d), dtype), pltpu.VMEM((win, d), dtype),
            pltpu.SemaphoreType.DMA, pltpu.SemaphoreType.DMA]


def _scatter_rows(src, idx, n_rows):
    n_src, d = src.shape
    m = idx.shape[0]
    win = SC_WINDOW
    per_worker = m // SC_WORKERS
    n_win = per_worker // win
    assert m == per_worker * SC_WORKERS and per_worker == n_win * win and n_win % 2 == 0
    assert n_src % per_worker == 0

    @functools.partial(pl.kernel, out_type=jax.ShapeDtypeStruct((n_rows, d), src.dtype), mesh=_sc_mesh(),
                       scratch_types=_sc_scratch(d, src.dtype), name="moe_scatter_rows")
    def scatter_kernel(x_hbm, i_hbm, o_hbm, idx_a, idx_b, rows_a, rows_b, sem_a, sem_b):
        base = _sc_worker_base(per_worker)
        src_base = lax.rem(base, n_src)

        def step(w, idx_v, rows_v, sem, first):
            if not first:
                pltpu.make_async_copy(rows_v, o_hbm.at[idx_v], sem).wait()
            pltpu.sync_copy(i_hbm.at[pl.ds(base + w * win, win)], idx_v)
            pltpu.sync_copy(x_hbm.at[pl.ds(src_base + w * win, win)], rows_v)
            pltpu.make_async_copy(rows_v, o_hbm.at[idx_v], sem).start()

        step(0, idx_a, rows_a, sem_a, True)
        step(1, idx_b, rows_b, sem_b, True)

        @pl.loop(2, n_win, step=2)
        def _(w):
            step(w, idx_a, rows_a, sem_a, False)
            step(w + 1, idx_b, rows_b, sem_b, False)

        pltpu.make_async_copy(rows_a, o_hbm.at[idx_a], sem_a).wait()
        pltpu.make_async_copy(rows_b, o_hbm.at[idx_b], sem_b).wait()

    return scatter_kernel(src, idx)


def _gather_rows(table, idx):
    d = table.shape[1]
    m = idx.shape[0]
    win = SC_WINDOW
    per_worker = m // SC_WORKERS
    n_win = per_worker // win
    assert m == per_worker * SC_WORKERS and per_worker == n_win * win and n_win % 2 == 0

    @functools.partial(pl.kernel, out_type=jax.ShapeDtypeStruct((m, d), table.dtype), mesh=_sc_mesh(),
                       scratch_types=_sc_scratch(d, table.dtype), name="moe_gather_rows")
    def gather_kernel(x_hbm, i_hbm, o_hbm, idx_a, idx_b, rows_a, rows_b, sem_a, sem_b):
        base = _sc_worker_base(per_worker)

        def out_copy(w, rows_v, sem):
            return pltpu.make_async_copy(rows_v, o_hbm.at[pl.ds(base + w * win, win)], sem)

        def step(w, idx_v, rows_v, sem, first):
            pltpu.sync_copy(i_hbm.at[pl.ds(base + w * win, win)], idx_v)
            if not first:
                out_copy(w, rows_v, sem).wait()
            pltpu.sync_copy(x_hbm.at[idx_v], rows_v)
            out_copy(w, rows_v, sem).start()

        step(0, idx_a, rows_a, sem_a, True)
        step(1, idx_b, rows_b, sem_b, True)

        @pl.loop(2, n_win, step=2)
        def _(w):
            step(w, idx_a, rows_a, sem_a, False)
            step(w + 1, idx_b, rows_b, sem_b, False)

        out_copy(0, rows_a, sem_a).wait()
        out_copy(0, rows_b, sem_b).wait()

    return gather_kernel(table, idx)


def _ffn_kernel(be_ref, nu_ref, nv_ref, xs_ref, wgu_ref, wdn_ref, ys_ref, wgu_bf_ref, wdn_bf_ref):
    i = pl.program_id(0)
    bm = xs_ref.shape[0]
    f = wdn_ref.shape[2]
    prev = be_ref[jnp.maximum(i - 1, 0)]

    @pl.when((i == 0) | (be_ref[i] != prev))
    def _():
        wgu_bf_ref[...] = wgu_ref[0, 0].astype(BF16)
        wdn_bf_ref[...] = wdn_ref[0, 0].astype(BF16)

    @pl.when(i < nu_ref[0])
    def _():
        valid = lax.broadcasted_iota(I32, (bm, 1), 0) < nv_ref[i]
        h = _unpack_bf16(jnp.where(valid, xs_ref[...], 0)).astype(BF16)
        gu = _dot(h, wgu_bf_ref[...])
        act = _silu(gu[:, :f]) * gu[:, f:]
        ys_ref[...] = _pack_bf16(_dot(act.astype(BF16), wdn_bf_ref[...]).astype(BF16))

    @pl.when(i >= nu_ref[0])
    def _():
        ys_ref[...] = jnp.zeros_like(ys_ref)


def _expert_ffn(xs, block_e, n_used, n_valid, w_gu_all, w_dn_all, layer):
    n_rows, dh = xs.shape
    bm = FFN_BLOCK
    n_blocks = n_rows // bm
    d, f = w_gu_all.shape[2], w_dn_all.shape[2]
    assert d == 2 * dh

    def xs_map(i, be, nu, nv):
        return (jnp.minimum(i, jnp.maximum(nu[0] - 1, 0)), 0)

    return pl.pallas_call(
        _ffn_kernel,
        grid_spec=pltpu.PrefetchScalarGridSpec(
            num_scalar_prefetch=3,
            grid=(n_blocks,),
            in_specs=[
                pl.BlockSpec((bm, dh), xs_map),
                pl.BlockSpec((1, 1, d, 2 * f), lambda i, be, nu, nv: (layer, be[i], 0, 0)),
                pl.BlockSpec((1, 1, f, d), lambda i, be, nu, nv: (layer, be[i], 0, 0)),
            ],
            out_specs=pl.BlockSpec((bm, dh), lambda i, be, nu, nv: (i, 0)),
            scratch_shapes=[pltpu.VMEM((d, 2 * f), BF16), pltpu.VMEM((f, d), BF16)],
        ),
        out_shape=jax.ShapeDtypeStruct((n_rows, dh), I32),
        compiler_params=pltpu.CompilerParams(dimension_semantics=("arbitrary",),
                                             vmem_limit_bytes=VMEM_LIMIT_BYTES),
        name="expert_ffn",
    )(block_e, n_used, n_valid, xs, w_gu_all, w_dn_all)


def _moe(hp, ids, gate, counts, w_gu_all, w_dn_all, layer):
    t = hp.shape[0]
    n_exp = w_gu_all.shape[1]
    bm = FFN_BLOCK
    n_assign = t * MOE_TOP_K
    n_rows = (n_assign + bm - 1) // bm * bm + n_exp * bm
    n_blocks = n_rows // bm

    expert = jnp.transpose(ids[:, 0:2, :], (1, 0, 2)).reshape(MOE_TOP_K, t)
    rank = jnp.transpose(ids[:, 2:4, :], (1, 0, 2)).reshape(MOE_TOP_K, t)
    gates = jnp.transpose(gate[:, 0:2, :], (0, 2, 1)).reshape(t, MOE_TOP_K)
    cnt = counts[:, 0]
    padded = (cnt + bm - 1) // bm * bm
    pad_end = jnp.cumsum(padded)
    pad_start = pad_end - padded
    e_ids = jnp.arange(n_exp, dtype=I32)
    start_of = jnp.sum(jnp.where(expert[..., None] == e_ids, pad_start, 0), axis=-1)
    dest = (start_of + rank).astype(I32).reshape(-1)
    blk_row = jnp.arange(n_blocks, dtype=I32) * bm
    block_e = jnp.minimum(jnp.sum(blk_row[:, None] >= pad_end[None, :], axis=1), n_exp - 1).astype(I32)
    n_used = (pad_end[-1:] // bm).astype(I32)
    n_valid = jnp.clip(pad_start[block_e] + cnt[block_e] - blk_row, 0, bm).astype(I32)

    xs = _scatter_rows(hp, dest, n_rows)
    ys = _expert_ffn(xs, block_e, n_used, n_valid, w_gu_all, w_dn_all, layer)
    return _gather_rows(ys, dest), gates


def _final_kernel(xm_ref, y0_ref, y1_ref, g_ref, nfin_ref, out_ref):
    out_ref[...] = _rms(_combined(xm_ref, y0_ref, y1_ref, g_ref), nfin_ref[...])


def _final(xm, yg, gates, norm_final):
    t, d = xm.shape
    ts = min(TS_FINAL, t)
    n_tiles = t // ts
    tile = pl.BlockSpec((ts, d), lambda i: (i, 0))
    return pl.pallas_call(
        _final_kernel,
        grid=(n_tiles,),
        in_specs=[tile, pl.BlockSpec((ts, d // 2), lambda i: (i, 0)),
                  pl.BlockSpec((ts, d // 2), lambda i: (n_tiles + i, 0)),
                  pl.BlockSpec((ts, MOE_TOP_K), lambda i: (i, 0)), pl.BlockSpec((1, d), lambda i: (0, 0))],
        out_specs=tile,
        out_shape=jax.ShapeDtypeStruct((t, d), F32),
        compiler_params=pltpu.CompilerParams(dimension_semantics=("arbitrary",),
                                             vmem_limit_bytes=VMEM_LIMIT_BYTES),
        name="final_combine_norm",
    )(xm, yg, yg, gates, norm_final.reshape(1, d))


def kernel(x, norm_mix, norm_ffn, norm_final, pool_w, pool_scale, sconv_in_w, sconv_taps, sconv_out_w, ssm_in_w, ssm_conv_w, ssm_conv_b, ssm_dt_bias, ssm_a_log, ssm_d, ssm_norm_w, ssm_out_w, router_group_w, router_group_b, router_expert_w, router_expert_b, expert_w_gu, expert_w_down):
    bsz, seq, d = x.shape
    depth = norm_mix.shape[0]
    assert (router_group_w.shape[2], router_expert_w.shape[2]) == ROUTER_GROUPS
    xm = x.reshape(bsz * seq, d)
    prev = None
    for i in range(depth):
        kind, j = i % N_MIXERS, i // N_MIXERS
        ts = min((TS_POOL, TS_SCONV, TS_SSM)[kind], seq)
        router = _router_operands(router_group_w[i], router_group_b[i], router_expert_w[i],
                                  router_expert_b[i], ts)
        if kind == 0:
            outs = _pool_layer(prev, xm, bsz, seq, ts, norm_mix[i], pool_w[j], pool_scale[j], norm_ffn[i], router)
        elif kind == 1:
            outs = _sconv_layer(prev, xm, bsz, seq, ts, norm_mix[i], sconv_in_w[j], sconv_taps[j],
                                sconv_out_w[j], norm_ffn[i], router)
        else:
            outs = _ssm_layer(prev, xm, bsz, seq, ts, norm_mix[i], ssm_in_w[j], ssm_conv_w[j], ssm_conv_b[j],
                              ssm_dt_bias[j], ssm_a_log[j], ssm_d[j], ssm_norm_w[j], ssm_out_w[j],
                              norm_ffn[i], router)
        xm, hp, ids, gate, counts = outs
        prev = _moe(hp, ids, gate, counts, expert_w_gu, expert_w_down, i)
    return _final(xm, prev[0], prev[1], norm_final).reshape(bsz, seq, d)
```

```python
import functools

import jax
import jax.numpy as jnp
from jax import lax
from jax.experimental import pallas as pl
from jax.experimental.pallas import tpu as pltpu
from jax.experimental.pallas import tpu_sc as plsc

F32 = jnp.float32
BF16 = jnp.bfloat16
I32 = jnp.int32

NORM_EPS = 1e-6
N_MIXERS = 3
POOL_WINDOWS = (2, 4, 8, 16)
POOL_HALO = 16
CONV_HALO = 8
SSM_GROUPS = 4
SSM_CHUNK = 128
MOE_TOP_K = 2
EXPERTS_PER_GROUP = 8
ROUTER_GROUPS = (4, 32)
ROUTER_ROWS = 48

SUBLANES = 8
LANES = 128
VMEM_LIMIT_BYTES = 56 * 1024 * 1024
SC_CORES = 2
SC_WORKERS = SC_CORES * 16
SC_WINDOW = 64

TS_POOL = 512
TS_SCONV = 512
TS_SSM = 256
TS_FINAL = 512
FFN_BLOCK = 512

NT_DIMS = (((1,), (1,)), ((), ()))
TN_DIMS = (((0,), (0,)), ((), ()))


def _dot(a, b):
    return jnp.dot(a, b, preferred_element_type=F32)


def _dot_nt(a, b):
    return lax.dot_general(a, b, NT_DIMS, preferred_element_type=F32)


def _split3(a):
    a1 = a.astype(BF16)
    r1 = a - a1.astype(F32)
    a2 = r1.astype(BF16)
    a3 = (r1 - a2.astype(F32)).astype(BF16)
    return a1, a2, a3


def _dot3(a, b):
    a1, a2, a3 = _split3(a)
    return _dot(a1, b) + _dot(a2, b) + _dot(a3, b)


def _hilo(a):
    hi = a.astype(BF16)
    lo = (a - hi.astype(F32)).astype(BF16)
    return hi, lo


def _dot_nt_hilo(w_hi, w_lo, a_hi, a_lo):
    return _dot_nt(w_hi, a_hi) + _dot_nt(w_hi, a_lo) + _dot_nt(w_lo, a_hi)


def _rms(x, w):
    return x * lax.rsqrt(jnp.mean(x * x, axis=-1, keepdims=True) + NORM_EPS) * w


def _silu(x):
    return x * (1.0 / (1.0 + jnp.exp(-x)))


def _shift_rows(ext, k, halo):
    return pltpu.roll(ext, k, axis=0)[halo:]


HI_HALF_MASK = -65536


def _pack_bf16(a_bf):
    half = a_bf.shape[1] // 2
    lo = lax.bitcast_convert_type(a_bf[:, :half].astype(F32), I32)
    hi = lax.bitcast_convert_type(a_bf[:, half:].astype(F32), I32)
    return lax.shift_right_logical(lo, 16) | (hi & HI_HALF_MASK)


def _unpack_bf16(w):
    lo = lax.bitcast_convert_type(lax.shift_left(w, 16), F32)
    hi = lax.bitcast_convert_type(w & HI_HALF_MASK, F32)
    return jnp.concatenate([lo, hi], axis=1)


def _combined(xm_ref, y0_ref, y1_ref, g_ref):
    g = g_ref[...]
    return xm_ref[...] + g[:, 0:1] * _unpack_bf16(y0_ref[...]) + g[:, 1:2] * _unpack_bf16(y1_ref[...])


def _route(xm, first, nffn_ref, wr_hi_ref, wr_lo_ref, br_ref, tri_ref, base_ref, hp_ref, ids_ref, gate_ref,
           cnt_ref):
    n_groups, n_experts = ROUTER_GROUPS
    ts = xm.shape[0]

    @pl.when(first)
    def _():
        base_ref[...] = jnp.zeros_like(base_ref)

    h2 = _rms(xm, nffn_ref[...])
    h_hi, h_lo = _hilo(h2)
    hp_ref[...] = _pack_bf16(h_hi)
    lt = _dot_nt_hilo(wr_hi_ref[...], wr_lo_ref[...], h_hi, h_lo) + br_ref[...]

    row8 = lax.broadcasted_iota(I32, (SUBLANES, ts), 0)
    neg_inf = jnp.float32(-jnp.inf)
    g = jnp.where(row8 < n_groups, lt[0:SUBLANES], neg_inf)
    gmax = jnp.max(g, axis=0, keepdims=True)
    gsel = jnp.min(jnp.where(g == gmax, row8, SUBLANES), axis=0, keepdims=True)
    pgrp = 1.0 / jnp.sum(jnp.exp(g - gmax), axis=0, keepdims=True)

    sel = lt[SUBLANES:2 * SUBLANES]
    for j in range(1, n_groups):
        sel = jnp.where(gsel == j, lt[(j + 1) * SUBLANES:(j + 2) * SUBLANES], sel)
    m1 = jnp.max(sel, axis=0, keepdims=True)
    i1 = jnp.min(jnp.where(sel == m1, row8, SUBLANES), axis=0, keepdims=True)
    sel2 = jnp.where(row8 == i1, neg_inf, sel)
    m2 = jnp.max(sel2, axis=0, keepdims=True)
    i2 = jnp.min(jnp.where(sel2 == m2, row8, SUBLANES), axis=0, keepdims=True)
    r = jnp.exp(m2 - m1)
    den = 1.0 + r
    g0 = pgrp / den
    g1 = pgrp * r / den
    e0 = gsel * EXPERTS_PER_GROUP + i1
    e1 = gsel * EXPERTS_PER_GROUP + i2

    row_e = lax.broadcasted_iota(I32, (n_experts, ts), 0)
    oh0 = row_e == e0
    oh1 = row_e == e1
    oh = jnp.where(oh0 | oh1, 1.0, 0.0).astype(F32)
    prefix = _dot(oh.astype(BF16), tri_ref[...])
    tot = prefix + base_ref[:, 0:1]
    rank0 = jnp.sum(jnp.where(oh0, tot, 0.0), axis=0, keepdims=True).astype(I32)
    rank1 = jnp.sum(jnp.where(oh1, tot, 0.0), axis=0, keepdims=True).astype(I32)
    base_ref[...] = base_ref[...] + jnp.sum(oh, axis=1, keepdims=True)

    ids_ref[0] = jnp.where(row8 == 0, e0, jnp.where(row8 == 1, e1, jnp.where(row8 == 2, rank0, rank1)))
    gate_ref[0] = jnp.where(row8 == 0, g0, g1)
    cnt_ref[...] = base_ref[...].astype(I32)


def _router_operands(w_rg, b_rg, w_re, b_re, ts):
    d, n_groups = w_rg.shape
    n_experts = w_re.shape[1]
    assert n_experts == n_groups * EXPERTS_PER_GROUP and n_groups <= SUBLANES
    assert SUBLANES + n_experts <= ROUTER_ROWS
    wt = jnp.zeros((ROUTER_ROWS, d), F32)
    wt = wt.at[0:n_groups].set(w_rg.T).at[SUBLANES:SUBLANES + n_experts].set(w_re.T)
    bt = jnp.zeros((ROUTER_ROWS, 1), F32)
    bt = bt.at[0:n_groups, 0].set(b_rg).at[SUBLANES:SUBLANES + n_experts, 0].set(b_re)
    w_hi = wt.astype(BF16)
    w_lo = (wt - w_hi.astype(F32)).astype(BF16)
    idx = jnp.arange(ts)
    tri = (idx[:, None] < idx[None, :]).astype(BF16)
    return w_hi, w_lo, bt, tri


def _const_spec(shape):
    zeros = (0,) * len(shape)
    return pl.BlockSpec(shape, lambda b, s: zeros)


def _mixer_in(prev, x2d, ts, n_s, d):
    n_tiles = x2d.shape[0] // ts
    tile = pl.BlockSpec((ts, d), lambda b, s: (b * n_s + s, 0))
    if prev is None:
        return [x2d], [tile]
    yg, gates = prev
    specs = [
        tile,
        pl.BlockSpec((ts, d // 2), lambda b, s: (b * n_s + s, 0)),
        pl.BlockSpec((ts, d // 2), lambda b, s: (n_tiles + b * n_s + s, 0)),
        pl.BlockSpec((ts, MOE_TOP_K), lambda b, s: (b * n_s + s, 0)),
    ]
    return [x2d, yg, yg, gates], specs


def _router_in(nffn, router, d, ts):
    w_hi, w_lo, bt, tri = router
    specs = [
        _const_spec((1, d)),
        _const_spec((ROUTER_ROWS, d)),
        _const_spec((ROUTER_ROWS, d)),
        _const_spec((ROUTER_ROWS, 1)),
        _const_spec((ts, ts)),
    ]
    return [nffn.reshape(1, d), w_hi, w_lo, bt, tri], specs


def _mixer_out(t, d, ts, n_s):
    n_tiles = t // ts
    n_experts = ROUTER_GROUPS[1]
    out_shape = [
        jax.ShapeDtypeStruct((t, d), F32),
        jax.ShapeDtypeStruct((t, d // 2), I32),
        jax.ShapeDtypeStruct((n_tiles, SUBLANES, ts), I32),
        jax.ShapeDtypeStruct((n_tiles, SUBLANES, ts), F32),
        jax.ShapeDtypeStruct((n_experts, LANES), I32),
    ]
    out_specs = [
        pl.BlockSpec((ts, d), lambda b, s: (b * n_s + s, 0)),
        pl.BlockSpec((ts, d // 2), lambda b, s: (b * n_s + s, 0)),
        pl.BlockSpec((1, SUBLANES, ts), lambda b, s: (b * n_s + s, 0, 0)),
        pl.BlockSpec((1, SUBLANES, ts), lambda b, s: (b * n_s + s, 0, 0)),
        pl.BlockSpec((n_experts, LANES), lambda b, s: (0, 0)),
    ]
    return out_shape, out_specs


def _mixer_call(kern, name, prev, x2d, bsz, seq, ts, mixer_ops, mixer_specs, nffn, router, scratch):
    t, d = x2d.shape
    n_s = seq // ts
    tok_ops, tok_specs = _mixer_in(prev, x2d, ts, n_s, d)
    r_ops, r_specs = _router_in(nffn, router, d, ts)
    out_shape, out_specs = _mixer_out(t, d, ts, n_s)
    return pl.pallas_call(
        functools.partial(kern, n_tok=len(tok_ops)),
        grid=(bsz, n_s),
        in_specs=tok_specs + mixer_specs + r_specs,
        out_specs=out_specs,
        out_shape=out_shape,
        scratch_shapes=scratch + [pltpu.VMEM((ROUTER_GROUPS[1], LANES), F32)],
        compiler_params=pltpu.CompilerParams(dimension_semantics=("arbitrary", "arbitrary"),
                                             vmem_limit_bytes=VMEM_LIMIT_BYTES),
        name=name,
    )(*tok_ops, *mixer_ops, *r_ops)


def _token_input(refs, n_tok):
    return refs[0][...] if n_tok == 1 else _combined(*refs[:n_tok])


def _pool_kernel(*refs, n_tok):
    x = _token_input(refs, n_tok)
    (nmix_ref, pw_ref, scale_ref,
     nffn_ref, wr_hi_ref, wr_lo_ref, br_ref, tri_ref,
     xm_ref, hp_ref, ids_ref, gate_ref, cnt_ref,
     carry_ref, base_ref) = refs[n_tok:]
    b = pl.program_id(0)
    s = pl.program_id(1)
    ts, d = x.shape
    ch = d // len(POOL_WINDOWS)

    @pl.when(s == 0)
    def _():
        carry_ref[...] = jnp.zeros_like(carry_ref)

    h = _rms(x, nmix_ref[...])
    ext = jnp.concatenate([carry_ref[...], h], axis=0)
    carry_ref[...] = h[ts - POOL_HALO:]

    pos = s * ts + lax.broadcasted_iota(I32, (ts, 1), 0) + 1
    ys = []
    for g, win in enumerate(POOL_WINDOWS):
        acc = ext[:, g * ch:(g + 1) * ch]
        k = 1
        while k < win:
            acc = acc + pltpu.roll(acc, k, axis=0)
            k *= 2
        inv_cnt = 1.0 / jnp.minimum(pos, win).astype(F32)
        pooled = acc[POOL_HALO:] * inv_cnt - h[:, g * ch:(g + 1) * ch]
        ys.append(_dot(pooled.astype(BF16), pw_ref[g]))
    y = jnp.concatenate(ys, axis=1) * scale_ref[...]
    xm = x + y
    xm_ref[...] = xm
    _route(xm, (b == 0) & (s == 0), nffn_ref, wr_hi_ref, wr_lo_ref, br_ref, tri_ref, base_ref,
           hp_ref, ids_ref, gate_ref, cnt_ref)


def _pool_layer(prev, x2d, bsz, seq, ts, nmix, pool_w, pool_scale, nffn, router):
    d = x2d.shape[1]
    n_win, ch, _ = pool_w.shape
    assert n_win == len(POOL_WINDOWS) and ch * n_win == d
    ops = [nmix.reshape(1, d), pool_w.astype(BF16), pool_scale.reshape(1, d)]
    specs = [_const_spec((1, d)), _const_spec((n_win, ch, ch)), _const_spec((1, d))]
    return _mixer_call(_pool_kernel, "pool_mixer", prev, x2d, bsz, seq, ts, ops, specs, nffn, router,
                       [pltpu.VMEM((POOL_HALO, d), F32)])


def _sconv_kernel(*refs, n_tok):
    x = _token_input(refs, n_tok)
    (nmix_ref, win_ref, taps_ref, wout_ref,
     nffn_ref, wr_hi_ref, wr_lo_ref, br_ref, tri_ref,
     xm_ref, hp_ref, ids_ref, gate_ref, cnt_ref,
     carry_ref, base_ref) = refs[n_tok:]
    b = pl.program_id(0)
    s = pl.program_id(1)
    ts, d = x.shape

    @pl.when(s == 0)
    def _():
        carry_ref[...] = jnp.zeros_like(carry_ref)

    h = _rms(x, nmix_ref[...]).astype(BF16)
    b_gate = _dot(h, win_ref[:, 0:d])
    c_gate = _dot(h, win_ref[:, d:2 * d])
    v = _dot(h, win_ref[:, 2 * d:3 * d])
    u_pre = c_gate * v
    ext = jnp.concatenate([carry_ref[...], u_pre], axis=0)
    carry_ref[...] = u_pre[ts - CONV_HALO:]
    width = taps_ref.shape[0]
    u = taps_ref[width - 1:width, :] * u_pre
    for k in range(1, width):
        u = u + taps_ref[width - 1 - k:width - k, :] * _shift_rows(ext, k, CONV_HALO)
    mix = _dot((b_gate * u).astype(BF16), wout_ref[...])
    xm = x + mix
    xm_ref[...] = xm
    _route(xm, (b == 0) & (s == 0), nffn_ref, wr_hi_ref, wr_lo_ref, br_ref, tri_ref, base_ref,
           hp_ref, ids_ref, gate_ref, cnt_ref)


def _sconv_layer(prev, x2d, bsz, seq, ts, nmix, w_in, taps, w_out, nffn, router):
    d = x2d.shape[1]
    width = taps.shape[0]
    assert width - 1 <= CONV_HALO
    ops = [nmix.reshape(1, d), w_in.astype(BF16), taps, w_out.astype(BF16)]
    specs = [_const_spec((1, d)), _const_spec((d, 3 * d)), _const_spec((width, d)), _const_spec((d, d))]
    return _mixer_call(_sconv_kernel, "sconv_mixer", prev, x2d, bsz, seq, ts, ops, specs, nffn, router,
                       [pltpu.VMEM((CONV_HALO, d), F32)])


def _ssm_kernel(*refs, n_tok):
    x = _token_input(refs, n_tok)
    (nmix_ref, wz_ref, wxbc_ref, wdt_hi_ref, wdt_lo_ref, convw_ref, convb_ref,
     dtb_ref, alog_ref, expand_ref, dskip_ref, normw_ref, wout_ref, triq_ref,
     nffn_ref, wr_hi_ref, wr_lo_ref, br_ref, tri_ref,
     xm_ref, hp_ref, ids_ref, gate_ref, cnt_ref,
     carry_ref, z_ref, xs_ref, bm_ref, cm_ref, y_ref, dtt_ref, at_ref, state_ref, base_ref) = refs[n_tok:]
    b = pl.program_id(0)
    s = pl.program_id(1)
    ts, d = x.shape
    di = z_ref.shape[1]
    gn = bm_ref.shape[1]
    cd = di + 2 * gn
    nh = alog_ref.shape[0]
    hp = di // nh
    grp = SSM_GROUPS
    st = gn // grp
    hpg = nh // grp
    gw = hpg * hp
    q = SSM_CHUNK
    n_chunks = ts // q

    @pl.when(s == 0)
    def _():
        carry_ref[...] = jnp.zeros_like(carry_ref)
        state_ref[...] = jnp.zeros_like(state_ref)

    h = _rms(x, nmix_ref[...])
    h_hi, h_lo = _hilo(h)

    z_ref[...] = _dot(h_hi, wz_ref[...])
    width = convw_ref.shape[0]
    col_blk = gn
    for c0 in range(0, cd, col_blk):
        cols = slice(c0, c0 + col_blk)
        pre = _dot(h_hi, wxbc_ref[:, cols])
        ext = jnp.concatenate([carry_ref[:, cols], pre], axis=0)
        carry_ref[:, cols] = pre[ts - CONV_HALO:]
        acc = convw_ref[width - 1:width, cols] * pre + convb_ref[:, cols]
        for k in range(1, width):
            acc = acc + convw_ref[width - 1 - k:width - k, cols] * _shift_rows(ext, k, CONV_HALO)
        act = _silu(acc)
        if c0 < di:
            xs_ref[:, cols] = act
        elif c0 < di + gn:
            bm_ref[...] = act
        else:
            cm_ref[...] = act

    dt_t = _dot_nt_hilo(wdt_hi_ref[...], wdt_lo_ref[...], h_hi, h_lo) + dtb_ref[...]
    dt_t = jnp.maximum(dt_t, 0.0) + jnp.log(1.0 + jnp.exp(-jnp.abs(dt_t)))
    a_t = dt_t * (-jnp.exp(alog_ref[...]))
    for c in range(n_chunks):
        dtt_ref[c] = dt_t[:, c * q:(c + 1) * q]
        at_ref[c] = a_t[:, c * q:(c + 1) * q]

    row_q = lax.broadcasted_iota(I32, (q, q), 0)
    col_q = lax.broadcasted_iota(I32, (q, q), 1)
    causal = row_q >= col_q
    lane_lo = lax.broadcasted_iota(I32, (q, 2 * hp), 1) < hp
    neg_inf = jnp.float32(-jnp.inf)
    expand = expand_ref[...]

    def chunk_body(c, carry):
        r0 = pl.multiple_of(c * q, q)
        rows = pl.ds(r0, q)
        acs_t = _dot3(at_ref[c], triq_ref[...])
        dt_c = dtt_ref[c]
        w_t = dt_c * jnp.exp(acs_t[:, q - 1:q] - acs_t)
        src_t = acs_t - jnp.log(dt_c)
        acs = acs_t.T
        od_hi, od_lo = _hilo(jnp.exp(acs))
        out_decay_e = _dot(od_hi, expand) + _dot(od_lo, expand)
        w_e = _dot(w_t.T.astype(BF16), expand)
        xc = xs_ref[rows, :]
        xw = xc * w_e
        bc = bm_ref[rows, :]
        cc = cm_ref[rows, :]
        for g in range(grp):
            gcols = slice(g * gw, (g + 1) * gw)
            bg = bc[:, g * st:(g + 1) * st].astype(BF16)
            cg = cc[:, g * st:(g + 1) * st].astype(BF16)
            xg = xc[:, gcols]
            cb = _dot_nt(cg, bg)
            hprev = state_ref[g]
            y_off = _dot(cg, hprev.astype(BF16)) * out_decay_e[:, gcols]
            new_states = lax.dot_general(bg, xw[:, gcols].astype(BF16), TN_DIMS, preferred_element_type=F32)
            state_ref[g] = hprev * out_decay_e[q - 1:q, gcols] + new_states
            for j in range(hpg // 2):
                h0 = g * hpg + 2 * j
                ms = []
                for hh in (h0, h0 + 1):
                    seg = acs[:, hh:hh + 1] - src_t[hh:hh + 1, :]
                    ms.append((cb * jnp.exp(jnp.where(causal, seg, neg_inf))).astype(BF16))
                lhs = jnp.concatenate(ms, axis=1)
                xp = xg[:, 2 * j * hp:(2 * j + 2) * hp]
                rhs = jnp.concatenate([jnp.where(lane_lo, xp, 0.0), jnp.where(lane_lo, 0.0, xp)],
                                      axis=0).astype(BF16)
                pc = slice(g * gw + 2 * j * hp, g * gw + (2 * j + 2) * hp)
                y_ref[rows, pc] = _dot(lhs, rhs) + y_off[:, 2 * j * hp:(2 * j + 2) * hp]
        return carry

    lax.fori_loop(0, n_chunks, chunk_body, 0)

    out = None
    for g in range(grp):
        gcols = slice(g * gw, (g + 1) * gw)
        yg = (y_ref[:, gcols] + xs_ref[:, gcols] * dskip_ref[:, gcols]) * _silu(z_ref[:, gcols])
        yn = yg * lax.rsqrt(jnp.mean(yg * yg, axis=-1, keepdims=True) + NORM_EPS) * normw_ref[:, gcols]
        part = _dot(yn.astype(BF16), wout_ref[gcols, :])
        out = part if out is None else out + part
    xm = x + out
    xm_ref[...] = xm
    _route(xm, (b == 0) & (s == 0), nffn_ref, wr_hi_ref, wr_lo_ref, br_ref, tri_ref, base_ref,
           hp_ref, ids_ref, gate_ref, cnt_ref)


def _ssm_layer(prev, x2d, bsz, seq, ts, nmix, w_in, conv_w, conv_b, dt_bias, a_log, d_skip, norm_w, w_out,
               nffn, router):
    d = x2d.shape[1]
    assert ts % SSM_CHUNK == 0
    nh = a_log.shape[0]
    di = norm_w.shape[0]
    cd = conv_w.shape[1]
    gn = (cd - di) // 2
    hp = di // nh
    width = conv_w.shape[0]
    assert width - 1 <= CONV_HALO and (nh // SSM_GROUPS) % 2 == 0 and di % gn == 0
    st = gn // SSM_GROUPS
    gw = di // SSM_GROUPS
    q = SSM_CHUNK
    n_chunks = ts // q

    wz = w_in[:, :di].astype(BF16)
    wxbc = w_in[:, di:di + cd].astype(BF16)
    wdt_t = w_in[:, di + cd:].T
    wdt_hi = wdt_t.astype(BF16)
    wdt_lo = (wdt_t - wdt_hi.astype(F32)).astype(BF16)
    expand = (jnp.arange(di)[None, :] // hp == jnp.arange(nh)[:, None]).astype(BF16)
    dskip_e = jnp.repeat(d_skip, hp).reshape(1, di)
    idx = jnp.arange(q)
    triq = (idx[:, None] <= idx[None, :]).astype(BF16)

    ops = [nmix.reshape(1, d), wz, wxbc, wdt_hi, wdt_lo, conv_w, conv_b.reshape(1, cd),
           dt_bias.reshape(nh, 1), a_log.reshape(nh, 1), expand, dskip_e, norm_w.reshape(1, di),
           w_out.astype(BF16), triq]
    specs = [
        _const_spec((1, d)),
        _const_spec((d, di)),
        _const_spec((d, cd)),
        _const_spec((nh, d)),
        _const_spec((nh, d)),
        _const_spec((width, cd)),
        _const_spec((1, cd)),
        _const_spec((nh, 1)),
        _const_spec((nh, 1)),
        _const_spec((nh, di)),
        _const_spec((1, di)),
        _const_spec((1, di)),
        _const_spec((di, d)),
        _const_spec((q, q)),
    ]
    scratch = [
        pltpu.VMEM((CONV_HALO, cd), F32),
        pltpu.VMEM((ts, di), F32),
        pltpu.VMEM((ts, di), F32),
        pltpu.VMEM((ts, gn), F32),
        pltpu.VMEM((ts, gn), F32),
        pltpu.VMEM((ts, di), F32),
        pltpu.VMEM((n_chunks, nh, q), F32),
        pltpu.VMEM((n_chunks, nh, q), F32),
        pltpu.VMEM((SSM_GROUPS, st, gw), F32),
    ]
    return _mixer_call(_ssm_kernel, "ssm_mixer", prev, x2d, bsz, seq, ts, ops, specs, nffn, router, scratch)


def _sc_mesh():
    return plsc.VectorSubcoreMesh(core_axis_name="core", subcore_axis_name="subcore")


def _sc_worker_base(per_worker):
    return (lax.axis_index("subcore") * SC_CORES + lax.axis_index("core")) * per_worker


def _sc_scratch(d, dtype):
    win = SC_WINDOW
    return [pltpu.VMEM((win,), I32), pltpu.VMEM((win,), I32),
            pltpu.VMEM((win, d), dtype), pltpu.VMEM((win, d), dtype),
            pltpu.SemaphoreType.DMA, pltpu.SemaphoreType.DMA]


def _scatter_rows(src, idx, n_rows):
    n_src, d = src.shape
    m = idx.shape[0]
    win = SC_WINDOW
    per_worker = m // SC_WORKERS
    n_win = per_worker // win
    assert m == per_worker * SC_WORKERS and per_worker == n_win * win and n_win % 2 == 0
    assert n_src % per_worker == 0

    @functools.partial(pl.kernel, out_type=jax.ShapeDtypeStruct((n_rows, d), src.dtype), mesh=_sc_mesh(),
                       scratch_types=_sc_scratch(d, src.dtype), name="moe_scatter_rows")
    def scatter_kernel(x_hbm, i_hbm, o_hbm, idx_a, idx_b, rows_a, rows_b, sem_a, sem_b):
        base = _sc_worker_base(per_worker)
        src_base = lax.rem(base, n_src)

        def step(w, idx_v, rows_v, sem, first):
            if not first:
                pltpu.make_async_copy(rows_v, o_hbm.at[idx_v], sem).wait()
            pltpu.sync_copy(i_hbm.at[pl.ds(base + w * win, win)], idx_v)
            pltpu.sync_copy(x_hbm.at[pl.ds(src_base + w * win, win)], rows_v)
            pltpu.make_async_copy(rows_v, o_hbm.at[idx_v], sem).start()

        step(0, idx_a, rows_a, sem_a, True)
        step(1, idx_b, rows_b, sem_b, True)

        @pl.loop(2, n_win, step=2)
        def _(w):
            step(w, idx_a, rows_a, sem_a, False)
            step(w + 1, idx_b, rows_b, sem_b, False)

        pltpu.make_async_copy(rows_a, o_hbm.at[idx_a], sem_a).wait()
        pltpu.make_async_copy(rows_b, o_hbm.at[idx_b], sem_b).wait()

    return scatter_kernel(src, idx)


def _gather_rows(table, idx):
    d = table.shape[1]
    m = idx.shape[0]
    win = SC_WINDOW
    per_worker = m // SC_WORKERS
    n_win = per_worker // win
    assert m == per_worker * SC_WORKERS and per_worker == n_win * win and n_win % 2 == 0

    @functools.partial(pl.kernel, out_type=jax.ShapeDtypeStruct((m, d), table.dtype), mesh=_sc_mesh(),
                       scratch_types=_sc_scratch(d, table.dtype), name="moe_gather_rows")
    def gather_kernel(x_hbm, i_hbm, o_hbm, idx_a, idx_b, rows_a, rows_b, sem_a, sem_b):
        base = _sc_worker_base(per_worker)

        def out_copy(w, rows_v, sem):
            return pltpu.make_async_copy(rows_v, o_hbm.at[pl.ds(base + w * win, win)], sem)

        def step(w, idx_v, rows_v, sem, first):
            pltpu.sync_copy(i_hbm.at[pl.ds(base + w * win, win)], idx_v)
            if not first:
                out_copy(w, rows_v, sem).wait()
            pltpu.sync_copy(x_hbm.at[idx_v], rows_v)
            out_copy(w, rows_v, sem).start()

        step(0, idx_a, rows_a, sem_a, True)
        step(1, idx_b, rows_b, sem_b, True)

        @pl.loop(2, n_win, step=2)
        def _(w):
            step(w, idx_a, rows_a, sem_a, False)
            step(w + 1, idx_b, rows_b, sem_b, False)

        out_copy(0, rows_a, sem_a).wait()
        out_copy(0, rows_b, sem_b).wait()

    return gather_kernel(table, idx)


def _ffn_kernel(be_ref, nu_ref, nv_ref, xs_ref, wgu_ref, wdn_ref, ys_ref, wgu_bf_ref, wdn_bf_ref):
    i = pl.program_id(0)
    bm = xs_ref.shape[0]
    f = wdn_ref.shape[2]
    prev = be_ref[jnp.maximum(i - 1, 0)]

    @pl.when((i == 0) | (be_ref[i] != prev))
    def _():
        wgu_bf_ref[...] = wgu_ref[0, 0].astype(BF16)
        wdn_bf_ref[...] = wdn_ref[0, 0].astype(BF16)

    @pl.when(i < nu_ref[0])
    def _():
        valid = lax.broadcasted_iota(I32, (bm, 1), 0) < nv_ref[i]
        h = _unpack_bf16(jnp.where(valid, xs_ref[...], 0)).astype(BF16)
        gu = _dot(h, wgu_bf_ref[...])
        act = _silu(gu[:, :f]) * gu[:, f:]
        ys_ref[...] = _pack_bf16(_dot(act.astype(BF16), wdn_bf_ref[...]).astype(BF16))

    @pl.when(i >= nu_ref[0])
    def _():
        ys_ref[...] = jnp.zeros_like(ys_ref)


def _expert_ffn(xs, block_e, n_used, n_valid, w_gu_all, w_dn_all, layer):
    n_rows, dh = xs.shape
    bm = FFN_BLOCK
    n_blocks = n_rows // bm
    d, f = w_gu_all.shape[2], w_dn_all.shape[2]
    assert d == 2 * dh

    def xs_map(i, be, nu, nv):
        return (jnp.minimum(i, jnp.maximum(nu[0] - 1, 0)), 0)

    return pl.pallas_call(
        _ffn_kernel,
        grid_spec=pltpu.PrefetchScalarGridSpec(
            num_scalar_prefetch=3,
            grid=(n_blocks,),
            in_specs=[
                pl.BlockSpec((bm, dh), xs_map),
                pl.BlockSpec((1, 1, d, 2 * f), lambda i, be, nu, nv: (layer, be[i], 0, 0)),
                pl.BlockSpec((1, 1, f, d), lambda i, be, nu, nv: (layer, be[i], 0, 0)),
            ],
            out_specs=pl.BlockSpec((bm, dh), lambda i, be, nu, nv: (i, 0)),
            scratch_shapes=[pltpu.VMEM((d, 2 * f), BF16), pltpu.VMEM((f, d), BF16)],
        ),
        out_shape=jax.ShapeDtypeStruct((n_rows, dh), I32),
        compiler_params=pltpu.CompilerParams(dimension_semantics=("arbitrary",),
                                             vmem_limit_bytes=VMEM_LIMIT_BYTES),
        name="expert_ffn",
    )(block_e, n_used, n_valid, xs, w_gu_all, w_dn_all)


def _moe(hp, ids, gate, counts, w_gu_all, w_dn_all, layer):
    t = hp.shape[0]
    n_exp = w_gu_all.shape[1]
    bm = FFN_BLOCK
    n_assign = t * MOE_TOP_K
    n_rows = (n_assign + bm - 1) // bm * bm + n_exp * bm
    n_blocks = n_rows // bm

    expert = jnp.transpose(ids[:, 0:2, :], (1, 0, 2)).reshape(MOE_TOP_K, t)
    rank = jnp.transpose(ids[:, 2:4, :], (1, 0, 2)).reshape(MOE_TOP_K, t)
    gates = jnp.transpose(gate[:, 0:2, :], (0, 2, 1)).reshape(t, MOE_TOP_K)
    cnt = counts[:, 0]
    padded = (cnt + bm - 1) // bm * bm
    pad_end = jnp.cumsum(padded)
    pad_start = pad_end - padded
    e_ids = jnp.arange(n_exp, dtype=I32)
    start_of = jnp.sum(jnp.where(expert[..., None] == e_ids, pad_start, 0), axis=-1)
    dest = (start_of + rank).astype(I32).reshape(-1)
    blk_row = jnp.arange(n_blocks, dtype=I32) * bm
    block_e = jnp.minimum(jnp.sum(blk_row[:, None] >= pad_end[None, :], axis=1), n_exp - 1).astype(I32)
    n_used = (pad_end[-1:] // bm).astype(I32)
    n_valid = jnp.clip(pad_start[block_e] + cnt[block_e] - blk_row, 0, bm).astype(I32)

    xs = _scatter_rows(hp, dest, n_rows)
    ys = _expert_ffn(xs, block_e, n_used, n_valid, w_gu_all, w_dn_all, layer)
    return _gather_rows(ys, dest), gates


def _final_kernel(xm_ref, y0_ref, y1_ref, g_ref, nfin_ref, out_ref):
    out_ref[...] = _rms(_combined(xm_ref, y0_ref, y1_ref, g_ref), nfin_ref[...])


def _final(xm, yg, gates, norm_final):
    t, d = xm.shape
    ts = min(TS_FINAL, t)
    n_tiles = t // ts
    tile = pl.BlockSpec((ts, d), lambda i: (i, 0))
    return pl.pallas_call(
        _final_kernel,
        grid=(n_tiles,),
        in_specs=[tile, pl.BlockSpec((ts, d // 2), lambda i: (i, 0)),
                  pl.BlockSpec((ts, d // 2), lambda i: (n_tiles + i, 0)),
                  pl.BlockSpec((ts, MOE_TOP_K), lambda i: (i, 0)), pl.BlockSpec((1, d), lambda i: (0, 0))],
        out_specs=tile,
        out_shape=jax.ShapeDtypeStruct((t, d), F32),
        compiler_params=pltpu.CompilerParams(dimension_semantics=("arbitrary",),
                                             vmem_limit_bytes=VMEM_LIMIT_BYTES),
        name="final_combine_norm",
    )(xm, yg, yg, gates, norm_final.reshape(1, d))


def kernel(x, norm_mix, norm_ffn, norm_final, pool_w, pool_scale, sconv_in_w, sconv_taps, sconv_out_w, ssm_in_w, ssm_conv_w, ssm_conv_b, ssm_dt_bias, ssm_a_log, ssm_d, ssm_norm_w, ssm_out_w, router_group_w, router_group_b, router_expert_w, router_expert_b, expert_w_gu, expert_w_down):
    bsz, seq, d = x.shape
    depth = norm_mix.shape[0]
    assert (router_group_w.shape[2], router_expert_w.shape[2]) == ROUTER_GROUPS
    xm = x.reshape(bsz * seq, d)
    prev = None
    for i in range(depth):
        kind, j = i % N_MIXERS, i // N_MIXERS
        ts = min((TS_POOL, TS_SCONV, TS_SSM)[kind], seq)
        router = _router_operands(router_group_w[i], router_group_b[i], router_expert_w[i],
                                  router_expert_b[i], ts)
        if kind == 0:
            outs = _pool_layer(prev, xm, bsz, seq, ts, norm_mix[i], pool_w[j], pool_scale[j], norm_ffn[i], router)
        elif kind == 1:
            outs = _sconv_layer(prev, xm, bsz, seq, ts, norm_mix[i], sconv_in_w[j], sconv_taps[j],
                                sconv_out_w[j], norm_ffn[i], router)
        else:
            outs = _ssm_layer(prev, xm, bsz, seq, ts, norm_mix[i], ssm_in_w[j], ssm_conv_w[j], ssm_conv_b[j],
                              ssm_dt_bias[j], ssm_a_log[j], ssm_d[j], ssm_norm_w[j], ssm_out_w[j],
                              norm_ffn[i], router)
        xm, hp, ids, gate, counts = outs
        prev = _moe(hp, ids, gate, counts, expert_w_gu, expert_w_down, i)
    return _final(xm, prev[0], prev[1], norm_final).reshape(bsz, seq, d)
```

```python
import functools

import jax
import jax.numpy as jnp
from jax import lax
from jax.experimental import pallas as pl
from jax.experimental.pallas import tpu as pltpu
from jax.experimental.pallas import tpu_sc as plsc

F32 = jnp.float32
BF16 = jnp.bfloat16
I32 = jnp.int32

NORM_EPS = 1e-6
N_MIXERS = 3
POOL_WINDOWS = (2, 4, 8, 16)
POOL_HALO = 16
CONV_HALO = 8
SSM_GROUPS = 4
SSM_CHUNK = 128
SSM_CONV_WIDTH = 4
MOE_TOP_K = 2
EXPERTS_PER_GROUP = 8
ROUTER_GROUPS = (4, 32)
ROUTER_ROWS = 48

SUBLANES = 8
LANES = 128
VMEM_LIMIT_BYTES = 56 * 1024 * 1024
SC_CORES = 2
SC_WORKERS = SC_CORES * 16
SC_WINDOW = 64

TS_POOL = 512
TS_SCONV = 512
TS_SSM = 256
TS_FINAL = 512
FFN_BLOCK = 512

NT_DIMS = (((1,), (1,)), ((), ()))
TN_DIMS = (((0,), (0,)), ((), ()))


def _dot(a, b):
    return jnp.dot(a, b, preferred_element_type=F32)


def _dot_nt(a, b):
    return lax.dot_general(a, b, NT_DIMS, preferred_element_type=F32)


def _split3(a):
    a1 = a.astype(BF16)
    r1 = a - a1.astype(F32)
    a2 = r1.astype(BF16)
    a3 = (r1 - a2.astype(F32)).astype(BF16)
    return a1, a2, a3


def _dot3(a, b):
    a1, a2, a3 = _split3(a)
    return _dot(a1, b) + _dot(a2, b) + _dot(a3, b)


def _hilo(a):
    hi = a.astype(BF16)
    lo = (a - hi.astype(F32)).astype(BF16)
    return hi, lo


def _dot_nt_hilo(w_hi, w_lo, a_hi, a_lo):
    return _dot_nt(w_hi, a_hi) + _dot_nt(w_hi, a_lo) + _dot_nt(w_lo, a_hi)


def _rms(x, w):
    return x * lax.rsqrt(jnp.mean(x * x, axis=-1, keepdims=True) + NORM_EPS) * w


def _silu(x):
    half = 0.5 * x
    return half + half * jnp.tanh(half)


def _shift_rows(ext, k, halo):
    return pltpu.roll(ext, k, axis=0)[halo:]


HI_HALF_MASK = -65536


def _pack_bf16(a_bf):
    half = a_bf.shape[1] // 2
    lo = lax.bitcast_convert_type(a_bf[:, :half].astype(F32), I32)
    hi = lax.bitcast_convert_type(a_bf[:, half:].astype(F32), I32)
    return lax.shift_right_logical(lo, 16) | (hi & HI_HALF_MASK)


def _unpack_bf16(w):
    lo = lax.bitcast_convert_type(lax.shift_left(w, 16), F32)
    hi = lax.bitcast_convert_type(w & HI_HALF_MASK, F32)
    return jnp.concatenate([lo, hi], axis=1)


def _combined(xm_ref, y0_ref, y1_ref, g_ref):
    g = g_ref[...]
    return xm_ref[...] + g[:, 0:1] * _unpack_bf16(y0_ref[...]) + g[:, 1:2] * _unpack_bf16(y1_ref[...])


def _route(xm, first, nffn_ref, wr_hi_ref, wr_lo_ref, br_ref, tri_ref, base_ref, hp_ref, ids_ref, gate_ref,
           cnt_ref):
    n_groups, n_experts = ROUTER_GROUPS
    ts = xm.shape[0]

    @pl.when(first)
    def _():
        base_ref[...] = jnp.zeros_like(base_ref)

    h2 = _rms(xm, nffn_ref[...])
    h_hi, h_lo = _hilo(h2)
    hp_ref[...] = _pack_bf16(h_hi)
    lt = _dot_nt_hilo(wr_hi_ref[...], wr_lo_ref[...], h_hi, h_lo) + br_ref[...]

    row8 = lax.broadcasted_iota(I32, (SUBLANES, ts), 0)
    neg_inf = jnp.float32(-jnp.inf)
    g = jnp.where(row8 < n_groups, lt[0:SUBLANES], neg_inf)
    gmax = jnp.max(g, axis=0, keepdims=True)
    gsel = jnp.min(jnp.where(g == gmax, row8, SUBLANES), axis=0, keepdims=True)
    pgrp = 1.0 / jnp.sum(jnp.exp(g - gmax), axis=0, keepdims=True)

    sel = lt[SUBLANES:2 * SUBLANES]
    for j in range(1, n_groups):
        sel = jnp.where(gsel == j, lt[(j + 1) * SUBLANES:(j + 2) * SUBLANES], sel)
    m1 = jnp.max(sel, axis=0, keepdims=True)
    i1 = jnp.min(jnp.where(sel == m1, row8, SUBLANES), axis=0, keepdims=True)
    sel2 = jnp.where(row8 == i1, neg_inf, sel)
    m2 = jnp.max(sel2, axis=0, keepdims=True)
    i2 = jnp.min(jnp.where(sel2 == m2, row8, SUBLANES), axis=0, keepdims=True)
    r = jnp.exp(m2 - m1)
    den = 1.0 + r
    g0 = pgrp / den
    g1 = pgrp * r / den
    e0 = gsel * EXPERTS_PER_GROUP + i1
    e1 = gsel * EXPERTS_PER_GROUP + i2

    row_e = lax.broadcasted_iota(I32, (n_experts, ts), 0)
    oh0 = row_e == e0
    oh1 = row_e == e1
    oh = jnp.where(oh0 | oh1, 1.0, 0.0).astype(F32)
    prefix = _dot(oh.astype(BF16), tri_ref[...])
    tot = prefix + base_ref[:, 0:1]
    rank0 = jnp.sum(jnp.where(oh0, tot, 0.0), axis=0, keepdims=True).astype(I32)
    rank1 = jnp.sum(jnp.where(oh1, tot, 0.0), axis=0, keepdims=True).astype(I32)
    base_ref[...] = base_ref[...] + jnp.sum(oh, axis=1, keepdims=True)

    ids_ref[0] = jnp.where(row8 == 0, e0, jnp.where(row8 == 1, e1, jnp.where(row8 == 2, rank0, rank1)))
    gate_ref[0] = jnp.where(row8 == 0, g0, g1)
    cnt_ref[...] = base_ref[...].astype(I32)


def _router_operands(w_rg, b_rg, w_re, b_re, ts):
    d, n_groups = w_rg.shape
    n_experts = w_re.shape[1]
    assert n_experts == n_groups * EXPERTS_PER_GROUP and n_groups <= SUBLANES
    assert SUBLANES + n_experts <= ROUTER_ROWS
    wt = jnp.zeros((ROUTER_ROWS, d), F32)
    wt = wt.at[0:n_groups].set(w_rg.T).at[SUBLANES:SUBLANES + n_experts].set(w_re.T)
    bt = jnp.zeros((ROUTER_ROWS, 1), F32)
    bt = bt.at[0:n_groups, 0].set(b_rg).at[SUBLANES:SUBLANES + n_experts, 0].set(b_re)
    w_hi = wt.astype(BF16)
    w_lo = (wt - w_hi.astype(F32)).astype(BF16)
    idx = jnp.arange(ts)
    tri = (idx[:, None] < idx[None, :]).astype(BF16)
    return w_hi, w_lo, bt, tri


def _const_spec(shape):
    zeros = (0,) * len(shape)
    return pl.BlockSpec(shape, lambda b, s: zeros)


def _mixer_in(prev, x2d, ts, n_s, d):
    n_tiles = x2d.shape[0] // ts
    tile = pl.BlockSpec((ts, d), lambda b, s: (b * n_s + s, 0))
    if prev is None:
        return [x2d], [tile]
    yg, gates = prev
    specs = [
        tile,
        pl.BlockSpec((ts, d // 2), lambda b, s: (b * n_s + s, 0)),
        pl.BlockSpec((ts, d // 2), lambda b, s: (n_tiles + b * n_s + s, 0)),
        pl.BlockSpec((ts, MOE_TOP_K), lambda b, s: (b * n_s + s, 0)),
    ]
    return [x2d, yg, yg, gates], specs


def _router_in(nffn, router, d, ts):
    w_hi, w_lo, bt, tri = router
    specs = [
        _const_spec((1, d)),
        _const_spec((ROUTER_ROWS, d)),
        _const_spec((ROUTER_ROWS, d)),
        _const_spec((ROUTER_ROWS, 1)),
        _const_spec((ts, ts)),
    ]
    return [nffn.reshape(1, d), w_hi, w_lo, bt, tri], specs


def _mixer_out(t, d, ts, n_s):
    n_tiles = t // ts
    n_experts = ROUTER_GROUPS[1]
    out_shape = [
        jax.ShapeDtypeStruct((t, d), F32),
        jax.ShapeDtypeStruct((t, d // 2), I32),
        jax.ShapeDtypeStruct((n_tiles, SUBLANES, ts), I32),
        jax.ShapeDtypeStruct((n_tiles, SUBLANES, ts), F32),
        jax.ShapeDtypeStruct((n_experts, LANES), I32),
    ]
    out_specs = [
        pl.BlockSpec((ts, d), lambda b, s: (b * n_s + s, 0)),
        pl.BlockSpec((ts, d // 2), lambda b, s: (b * n_s + s, 0)),
        pl.BlockSpec((1, SUBLANES, ts), lambda b, s: (b * n_s + s, 0, 0)),
        pl.BlockSpec((1, SUBLANES, ts), lambda b, s: (b * n_s + s, 0, 0)),
        pl.BlockSpec((n_experts, LANES), lambda b, s: (0, 0)),
    ]
    return out_shape, out_specs


def _mixer_call(kern, name, prev, x2d, bsz, seq, ts, mixer_ops, mixer_specs, nffn, router, scratch):
    t, d = x2d.shape
    n_s = seq // ts
    tok_ops, tok_specs = _mixer_in(prev, x2d, ts, n_s, d)
    r_ops, r_specs = _router_in(nffn, router, d, ts)
    out_shape, out_specs = _mixer_out(t, d, ts, n_s)
    return pl.pallas_call(
        functools.partial(kern, n_tok=len(tok_ops)),
        grid=(bsz, n_s),
        in_specs=tok_specs + mixer_specs + r_specs,
        out_specs=out_specs,
        out_shape=out_shape,
        scratch_shapes=scratch + [pltpu.VMEM((ROUTER_GROUPS[1], LANES), F32)],
        compiler_params=pltpu.CompilerParams(dimension_semantics=("arbitrary", "arbitrary"),
                                             vmem_limit_bytes=VMEM_LIMIT_BYTES),
        name=name,
    )(*tok_ops, *mixer_ops, *r_ops)


def _token_input(refs, n_tok):
    return refs[0][...] if n_tok == 1 else _combined(*refs[:n_tok])


def _pool_kernel(*refs, n_tok):
    x = _token_input(refs, n_tok)
    (nmix_ref, pw_ref, scale_ref,
     nffn_ref, wr_hi_ref, wr_lo_ref, br_ref, tri_ref,
     xm_ref, hp_ref, ids_ref, gate_ref, cnt_ref,
     carry_ref, base_ref) = refs[n_tok:]
    b = pl.program_id(0)
    s = pl.program_id(1)
    ts, d = x.shape
    ch = d // len(POOL_WINDOWS)

    @pl.when(s == 0)
    def _():
        carry_ref[...] = jnp.zeros_like(carry_ref)

    h = _rms(x, nmix_ref[...])
    ext = jnp.concatenate([carry_ref[...], h], axis=0)
    carry_ref[...] = h[ts - POOL_HALO:]

    pos = s * ts + lax.broadcasted_iota(I32, (ts, 1), 0) + 1
    ys = []
    for g, win in enumerate(POOL_WINDOWS):
        acc = ext[:, g * ch:(g + 1) * ch]
        k = 1
        while k < win:
            acc = acc + pltpu.roll(acc, k, axis=0)
            k *= 2
        inv_cnt = 1.0 / jnp.minimum(pos, win).astype(F32)
        pooled = acc[POOL_HALO:] * inv_cnt - h[:, g * ch:(g + 1) * ch]
        ys.append(_dot(pooled.astype(BF16), pw_ref[g]))
    y = jnp.concatenate(ys, axis=1) * scale_ref[...]
    xm = x + y
    xm_ref[...] = xm
    _route(xm, (b == 0) & (s == 0), nffn_ref, wr_hi_ref, wr_lo_ref, br_ref, tri_ref, base_ref,
           hp_ref, ids_ref, gate_ref, cnt_ref)


def _pool_layer(prev, x2d, bsz, seq, ts, nmix, pool_w, pool_scale, nffn, router):
    d = x2d.shape[1]
    n_win, ch, _ = pool_w.shape
    assert n_win == len(POOL_WINDOWS) and ch * n_win == d
    ops = [nmix.reshape(1, d), pool_w.astype(BF16), pool_scale.reshape(1, d)]
    specs = [_const_spec((1, d)), _const_spec((n_win, ch, ch)), _const_spec((1, d))]
    return _mixer_call(_pool_kernel, "pool_mixer", prev, x2d, bsz, seq, ts, ops, specs, nffn, router,
                       [pltpu.VMEM((POOL_HALO, d), F32)])


def _sconv_kernel(*refs, n_tok):
    x = _token_input(refs, n_tok)
    (nmix_ref, win_ref, taps_ref, wout_ref,
     nffn_ref, wr_hi_ref, wr_lo_ref, br_ref, tri_ref,
     xm_ref, hp_ref, ids_ref, gate_ref, cnt_ref,
     carry_ref, base_ref) = refs[n_tok:]
    b = pl.program_id(0)
    s = pl.program_id(1)
    ts, d = x.shape

    @pl.when(s == 0)
    def _():
        carry_ref[...] = jnp.zeros_like(carry_ref)

    h = _rms(x, nmix_ref[...]).astype(BF16)
    b_gate = _dot(h, win_ref[:, 0:d])
    c_gate = _dot(h, win_ref[:, d:2 * d])
    v = _dot(h, win_ref[:, 2 * d:3 * d])
    u_pre = c_gate * v
    ext = jnp.concatenate([carry_ref[...], u_pre], axis=0)
    carry_ref[...] = u_pre[ts - CONV_HALO:]
    width = taps_ref.shape[0]
    u = taps_ref[width - 1:width, :] * u_pre
    for k in range(1, width):
        u = u + taps_ref[width - 1 - k:width - k, :] * _shift_rows(ext, k, CONV_HALO)
    mix = _dot((b_gate * u).astype(BF16), wout_ref[...])
    xm = x + mix
    xm_ref[...] = xm
    _route(xm, (b == 0) & (s == 0), nffn_ref, wr_hi_ref, wr_lo_ref, br_ref, tri_ref, base_ref,
           hp_ref, ids_ref, gate_ref, cnt_ref)


def _sconv_layer(prev, x2d, bsz, seq, ts, nmix, w_in, taps, w_out, nffn, router):
    d = x2d.shape[1]
    width = taps.shape[0]
    assert width - 1 <= CONV_HALO
    ops = [nmix.reshape(1, d), w_in.astype(BF16), taps, w_out.astype(BF16)]
    specs = [_const_spec((1, d)), _const_spec((d, 3 * d)), _const_spec((width, d)), _const_spec((d, d))]
    return _mixer_call(_sconv_kernel, "sconv_mixer", prev, x2d, bsz, seq, ts, ops, specs, nffn, router,
                       [pltpu.VMEM((CONV_HALO, d), F32)])


def _ssm_kernel(*refs, n_tok):
    x = _token_input(refs, n_tok)
    (nmix_ref, wz_ref, wxbc_ref, wdt_hi_ref, wdt_lo_ref, convw_ref, convb_ref,
     dtb_ref, alog_ref, expand_ref, dskip_ref, normw_ref, wout_ref, triq_ref,
     nffn_ref, wr_hi_ref, wr_lo_ref, br_ref, tri_ref,
     xm_ref, hp_ref, ids_ref, gate_ref, cnt_ref,
     carry_ref, z_ref, xs_ref, bm_ref, cm_ref, y_ref, dtt_ref, at_ref, state_ref, base_ref) = refs[n_tok:]
    b = pl.program_id(0)
    s = pl.program_id(1)
    ts, d = x.shape
    di = z_ref.shape[1]
    gn = bm_ref.shape[1]
    cd = di + 2 * gn
    nh = alog_ref.shape[0]
    hp = di // nh
    grp = SSM_GROUPS
    st = gn // grp
    hpg = nh // grp
    gw = hpg * hp
    q = SSM_CHUNK
    n_chunks = ts // q

    @pl.when(s == 0)
    def _():
        carry_ref[...] = jnp.zeros_like(carry_ref)
        state_ref[...] = jnp.zeros_like(state_ref)

    h = _rms(x, nmix_ref[...])
    h_hi, h_lo = _hilo(h)

    z_ref[...] = _dot(h_hi, wz_ref[...])
    col_blk = gn
    for c0 in range(0, cd, col_blk):
        cols = slice(c0, c0 + col_blk)
        pre = _dot(h_hi, wxbc_ref[:, cols])
        ext = jnp.concatenate([carry_ref[:, cols], pre], axis=0)
        carry_ref[:, cols] = pre[ts - CONV_HALO:]
        ext1 = pltpu.roll(ext, 1, axis=0)
        pair = convw_ref[1:2, cols] * ext + convw_ref[0:1, cols] * ext1
        acc = (convw_ref[3:4, cols] * pre + convb_ref[:, cols] + convw_ref[2:3, cols] * ext1[CONV_HALO:]
               + _shift_rows(pair, 2, CONV_HALO))
        act = _silu(acc)
        if c0 < di:
            xs_ref[:, cols] = act
        elif c0 < di + gn:
            bm_ref[...] = act
        else:
            cm_ref[...] = act

    dt_t = _dot_nt_hilo(wdt_hi_ref[...], wdt_lo_ref[...], h_hi, h_lo) + dtb_ref[...]
    dt_t = jnp.maximum(dt_t, 0.0) + jnp.log(1.0 + jnp.exp(-jnp.abs(dt_t)))
    a_t = dt_t * (-jnp.exp(alog_ref[...]))
    for c in range(n_chunks):
        dtt_ref[c] = dt_t[:, c * q:(c + 1) * q]
        at_ref[c] = a_t[:, c * q:(c + 1) * q]

    row_q = lax.broadcasted_iota(I32, (q, q), 0)
    col_q = lax.broadcasted_iota(I32, (q, q), 1)
    causal = row_q >= col_q
    lane_lo = lax.broadcasted_iota(I32, (q, 2 * hp), 1) < hp
    neg_inf = jnp.float32(-jnp.inf)
    expand = expand_ref[...]

    def chunk_body(c, carry):
        r0 = pl.multiple_of(c * q, q)
        rows = pl.ds(r0, q)
        acs_t = _dot3(at_ref[c], triq_ref[...])
        dt_c = dtt_ref[c]
        w_t = dt_c * jnp.exp(acs_t[:, q - 1:q] - acs_t)
        src_t = acs_t - jnp.log(dt_c)
        acs = acs_t.T
        od_hi, od_lo = _hilo(jnp.exp(acs))
        out_decay_e = _dot(od_hi, expand) + _dot(od_lo, expand)
        w_e = _dot(w_t.T.astype(BF16), expand)
        xc = xs_ref[rows, :]
        xw = xc * w_e
        bc = bm_ref[rows, :]
        cc = cm_ref[rows, :]
        for g in range(grp):
            gcols = slice(g * gw, (g + 1) * gw)
            bg = bc[:, g * st:(g + 1) * st].astype(BF16)
            cg = cc[:, g * st:(g + 1) * st].astype(BF16)
            xg = xc[:, gcols]
            cb = _dot_nt(cg, bg)
            hprev = state_ref[g]
            y_off = _dot(cg, hprev.astype(BF16)) * out_decay_e[:, gcols]
            new_states = lax.dot_general(bg, xw[:, gcols].astype(BF16), TN_DIMS, preferred_element_type=F32)
            state_ref[g] = hprev * out_decay_e[q - 1:q, gcols] + new_states
            for j in range(hpg // 2):
                h0 = g * hpg + 2 * j
                ms = []
                for hh in (h0, h0 + 1):
                    seg = acs[:, hh:hh + 1] - src_t[hh:hh + 1, :]
                    ms.append((cb * jnp.exp(jnp.where(causal, seg, neg_inf))).astype(BF16))
                lhs = jnp.concatenate(ms, axis=1)
                xp = xg[:, 2 * j * hp:(2 * j + 2) * hp]
                rhs = jnp.concatenate([jnp.where(lane_lo, xp, 0.0), jnp.where(lane_lo, 0.0, xp)],
                                      axis=0).astype(BF16)
                pc = slice(g * gw + 2 * j * hp, g * gw + (2 * j + 2) * hp)
                y_ref[rows, pc] = _dot(lhs, rhs) + y_off[:, 2 * j * hp:(2 * j + 2) * hp]
        return carry

    lax.fori_loop(0, n_chunks, chunk_body, 0)

    out = None
    for g in range(grp):
        gcols = slice(g * gw, (g + 1) * gw)
        yg = (y_ref[:, gcols] + xs_ref[:, gcols] * dskip_ref[:, gcols]) * _silu(z_ref[:, gcols])
        yn = yg * lax.rsqrt(jnp.mean(yg * yg, axis=-1, keepdims=True) + NORM_EPS) * normw_ref[:, gcols]
        part = _dot(yn.astype(BF16), wout_ref[gcols, :])
        out = part if out is None else out + part
    xm = x + out
    xm_ref[...] = xm
    _route(xm, (b == 0) & (s == 0), nffn_ref, wr_hi_ref, wr_lo_ref, br_ref, tri_ref, base_ref,
           hp_ref, ids_ref, gate_ref, cnt_ref)


def _ssm_layer(prev, x2d, bsz, seq, ts, nmix, w_in, conv_w, conv_b, dt_bias, a_log, d_skip, norm_w, w_out,
               nffn, router):
    d = x2d.shape[1]
    assert ts % SSM_CHUNK == 0
    nh = a_log.shape[0]
    di = norm_w.shape[0]
    cd = conv_w.shape[1]
    gn = (cd - di) // 2
    hp = di // nh
    width = conv_w.shape[0]
    assert width == SSM_CONV_WIDTH and (nh // SSM_GROUPS) % 2 == 0 and di % gn == 0
    st = gn // SSM_GROUPS
    gw = di // SSM_GROUPS
    q = SSM_CHUNK
    n_chunks = ts // q

    wz = w_in[:, :di].astype(BF16)
    wxbc = w_in[:, di:di + cd].astype(BF16)
    wdt_t = w_in[:, di + cd:].T
    wdt_hi = wdt_t.astype(BF16)
    wdt_lo = (wdt_t - wdt_hi.astype(F32)).astype(BF16)
    expand = (jnp.arange(di)[None, :] // hp == jnp.arange(nh)[:, None]).astype(BF16)
    dskip_e = jnp.repeat(d_skip, hp).reshape(1, di)
    idx = jnp.arange(q)
    triq = (idx[:, None] <= idx[None, :]).astype(BF16)

    ops = [nmix.reshape(1, d), wz, wxbc, wdt_hi, wdt_lo, conv_w, conv_b.reshape(1, cd),
           dt_bias.reshape(nh, 1), a_log.reshape(nh, 1), expand, dskip_e, norm_w.reshape(1, di),
           w_out.astype(BF16), triq]
    specs = [
        _const_spec((1, d)),
        _const_spec((d, di)),
        _const_spec((d, cd)),
        _const_spec((nh, d)),
        _const_spec((nh, d)),
        _const_spec((width, cd)),
        _const_spec((1, cd)),
        _const_spec((nh, 1)),
        _const_spec((nh, 1)),
        _const_spec((nh, di)),
        _const_spec((1, di)),
        _const_spec((1, di)),
        _const_spec((di, d)),
        _const_spec((q, q)),
    ]
    scratch = [
        pltpu.VMEM((CONV_HALO, cd), F32),
        pltpu.VMEM((ts, di), F32),
        pltpu.VMEM((ts, di), F32),
        pltpu.VMEM((ts, gn), F32),
        pltpu.VMEM((ts, gn), F32),
        pltpu.VMEM((ts, di), F32),
        pltpu.VMEM((n_chunks, nh, q), F32),
        pltpu.VMEM((n_chunks, nh, q), F32),
        pltpu.VMEM((SSM_GROUPS, st, gw), F32),
    ]
    return _mixer_call(_ssm_kernel, "ssm_mixer", prev, x2d, bsz, seq, ts, ops, specs, nffn, router, scratch)


def _sc_mesh():
    return plsc.VectorSubcoreMesh(core_axis_name="core", subcore_axis_name="subcore")


def _sc_worker_base(per_worker):
    return (lax.axis_index("subcore") * SC_CORES + lax.axis_index("core")) * per_worker


def _sc_scratch(d, dtype):
    win = SC_WINDOW
    return [pltpu.VMEM((win,), I32), pltpu.VMEM((win,), I32),
            pltpu.VMEM((win, d), dtype), pltpu.VMEM((win, d), dtype),
            pltpu.SemaphoreType.DMA, pltpu.SemaphoreType.DMA]


def _scatter_rows(src, idx, n_rows):
    n_src, d = src.shape
    m = idx.shape[0]
    win = SC_WINDOW
    per_worker = m // SC_WORKERS
    n_win = per_worker // win
    assert m == per_worker * SC_WORKERS and per_worker == n_win * win and n_win % 2 == 0
    assert n_src % per_worker == 0

    @functools.partial(pl.kernel, out_type=jax.ShapeDtypeStruct((n_rows, d), src.dtype), mesh=_sc_mesh(),
                       scratch_types=_sc_scratch(d, src.dtype), name="moe_scatter_rows")
    def scatter_kernel(x_hbm, i_hbm, o_hbm, idx_a, idx_b, rows_a, rows_b, sem_a, sem_b):
        base = _sc_worker_base(per_worker)
        src_base = lax.rem(base, n_src)

        def step(w, idx_v, rows_v, sem, first):
            if not first:
                pltpu.make_async_copy(rows_v, o_hbm.at[idx_v], sem).wait()
            pltpu.sync_copy(i_hbm.at[pl.ds(base + w * win, win)], idx_v)
            pltpu.sync_copy(x_hbm.at[pl.ds(src_base + w * win, win)], rows_v)
            pltpu.make_async_copy(rows_v, o_hbm.at[idx_v], sem).start()

        step(0, idx_a, rows_a, sem_a, True)
        step(1, idx_b, rows_b, sem_b, True)

        @pl.loop(2, n_win, step=2)
        def _(w):
            step(w, idx_a, rows_a, sem_a, False)
            step(w + 1, idx_b, rows_b, sem_b, False)

        pltpu.make_async_copy(rows_a, o_hbm.at[idx_a], sem_a).wait()
        pltpu.make_async_copy(rows_b, o_hbm.at[idx_b], sem_b).wait()

    return scatter_kernel(src, idx)


def _gather_rows(table, idx):
    d = table.shape[1]
    m = idx.shape[0]
    win = SC_WINDOW
    per_worker = m // SC_WORKERS
    n_win = per_worker // win
    assert m == per_worker * SC_WORKERS and per_worker == n_win * win and n_win % 2 == 0

    @functools.partial(pl.kernel, out_type=jax.ShapeDtypeStruct((m, d), table.dtype), mesh=_sc_mesh(),
                       scratch_types=_sc_scratch(d, table.dtype), name="moe_gather_rows")
    def gather_kernel(x_hbm, i_hbm, o_hbm, idx_a, idx_b, rows_a, rows_b, sem_a, sem_b):
        base = _sc_worker_base(per_worker)

        def out_copy(w, rows_v, sem):
            return pltpu.make_async_copy(rows_v, o_hbm.at[pl.ds(base + w * win, win)], sem)

        def step(w, idx_v, rows_v, sem, first):
            pltpu.sync_copy(i_hbm.at[pl.ds(base + w * win, win)], idx_v)
            if not first:
                out_copy(w, rows_v, sem).wait()
            pltpu.sync_copy(x_hbm.at[idx_v], rows_v)
            out_copy(w, rows_v, sem).start()

        step(0, idx_a, rows_a, sem_a, True)
        step(1, idx_b, rows_b, sem_b, True)

        @pl.loop(2, n_win, step=2)
        def _(w):
            step(w, idx_a, rows_a, sem_a, False)
            step(w + 1, idx_b, rows_b, sem_b, False)

        out_copy(0, rows_a, sem_a).wait()
        out_copy(0, rows_b, sem_b).wait()

    return gather_kernel(table, idx)


def _ffn_kernel(be_ref, nu_ref, nv_ref, slot_ref, nxt_ref, xs_ref, wgu_hbm, wdn_hbm, ys_ref,
                wgu_f32_ref, wdn_f32_ref, wgu_bf_ref, wdn_bf_ref, sem_ref, *, layer):
    i = pl.program_id(0)
    bm = xs_ref.shape[0]
    f = wdn_bf_ref.shape[0]

    def weight_copies(expert, slot):
        return (pltpu.make_async_copy(wgu_hbm.at[layer, expert], wgu_f32_ref.at[slot], sem_ref.at[slot, 0]),
                pltpu.make_async_copy(wdn_hbm.at[layer, expert], wdn_f32_ref.at[slot], sem_ref.at[slot, 1]))

    @pl.when(i == 0)
    def _():
        for cp in weight_copies(be_ref[0], slot_ref[0]):
            cp.start()

    first_of_expert = (i == 0) | (be_ref[i] != be_ref[jnp.maximum(i - 1, 0)])

    @pl.when(first_of_expert & (i < nu_ref[0]))
    def _():
        slot = slot_ref[i]
        for cp in weight_copies(be_ref[i], slot):
            cp.wait()

        @pl.when(nxt_ref[i] >= 0)
        def _():
            for cp in weight_copies(nxt_ref[i], 1 - slot):
                cp.start()

        wgu_bf_ref[...] = wgu_f32_ref[slot].astype(BF16)
        wdn_bf_ref[...] = wdn_f32_ref[slot].astype(BF16)

    @pl.when(i < nu_ref[0])
    def _():
        valid = lax.broadcasted_iota(I32, (bm, 1), 0) < nv_ref[i]
        h = _unpack_bf16(jnp.where(valid, xs_ref[...], 0)).astype(BF16)
        gu = _dot(h, wgu_bf_ref[...])
        act = _silu(gu[:, :f]) * gu[:, f:]
        ys_ref[...] = _pack_bf16(_dot(act.astype(BF16), wdn_bf_ref[...]).astype(BF16))

    @pl.when(i >= nu_ref[0])
    def _():
        ys_ref[...] = jnp.zeros_like(ys_ref)


def _weight_schedule(block_e, n_used):
    n_blocks = block_e.shape[0]
    idx = jnp.arange(n_blocks, dtype=I32)
    starts_segment = jnp.concatenate([jnp.ones((1,), bool), block_e[1:] != block_e[:-1]])
    slot = ((jnp.cumsum(starts_segment.astype(I32)) - 1) % 2).astype(I32)
    later_start = lax.cummin(jnp.where(starts_segment, idx, n_blocks)[::-1])[::-1]
    next_start = jnp.concatenate([later_start[1:], jnp.full((1,), n_blocks, I32)])
    next_e = jnp.where(next_start < n_used[0], block_e[jnp.minimum(next_start, n_blocks - 1)], -1)
    return slot, next_e.astype(I32)


def _expert_ffn(xs, block_e, n_used, n_valid, w_gu_all, w_dn_all, layer):
    n_rows, dh = xs.shape
    bm = FFN_BLOCK
    n_blocks = n_rows // bm
    d, f = w_gu_all.shape[2], w_dn_all.shape[2]
    assert d == 2 * dh

    slot, next_e = _weight_schedule(block_e, n_used)

    def xs_map(i, be, nu, nv, sl, nx):
        return (jnp.minimum(i, jnp.maximum(nu[0] - 1, 0)), 0)

    return pl.pallas_call(
        functools.partial(_ffn_kernel, layer=layer),
        grid_spec=pltpu.PrefetchScalarGridSpec(
            num_scalar_prefetch=5,
            grid=(n_blocks,),
            in_specs=[
                pl.BlockSpec((bm, dh), xs_map),
                pl.BlockSpec(memory_space=pl.ANY),
                pl.BlockSpec(memory_space=pl.ANY),
            ],
            out_specs=pl.BlockSpec((bm, dh), lambda i, be, nu, nv, sl, nx: (i, 0)),
            scratch_shapes=[
                pltpu.VMEM((2, d, 2 * f), F32),
                pltpu.VMEM((2, f, d), F32),
                pltpu.VMEM((d, 2 * f), BF16),
                pltpu.VMEM((f, d), BF16),
                pltpu.SemaphoreType.DMA((2, 2)),
            ],
        ),
        out_shape=jax.ShapeDtypeStruct((n_rows, dh), I32),
        compiler_params=pltpu.CompilerParams(dimension_semantics=("arbitrary",),
                                             vmem_limit_bytes=VMEM_LIMIT_BYTES),
        name="expert_ffn",
    )(block_e, n_used, n_valid, slot, next_e, xs, w_gu_all, w_dn_all)


def _moe(hp, ids, gate, counts, w_gu_all, w_dn_all, layer):
    t = hp.shape[0]
    n_exp = w_gu_all.shape[1]
    bm = FFN_BLOCK
    n_assign = t * MOE_TOP_K
    n_rows = (n_assign + bm - 1) // bm * bm + n_exp * bm
    n_blocks = n_rows // bm

    expert = jnp.transpose(ids[:, 0:2, :], (1, 0, 2)).reshape(MOE_TOP_K, t)
    rank = jnp.transpose(ids[:, 2:4, :], (1, 0, 2)).reshape(MOE_TOP_K, t)
    gates = jnp.transpose(gate[:, 0:2, :], (0, 2, 1)).reshape(t, MOE_TOP_K)
    cnt = counts[:, 0]
    padded = (cnt + bm - 1) // bm * bm
    pad_end = jnp.cumsum(padded)
    pad_start = pad_end - padded
    dest = rank
    for e in range(n_exp):
        dest = dest + jnp.where(expert == e, pad_start[e], 0)
    dest = dest.astype(I32).reshape(-1)
    blk_row = jnp.arange(n_blocks, dtype=I32) * bm
    block_e = jnp.minimum(jnp.sum(blk_row[:, None] >= pad_end[None, :], axis=1), n_exp - 1).astype(I32)
    n_used = (pad_end[-1:] // bm).astype(I32)
    n_valid = jnp.clip(pad_start[block_e] + cnt[block_e] - blk_row, 0, bm).astype(I32)

    xs = _scatter_rows(hp, dest, n_rows)
    ys = _expert_ffn(xs, block_e, n_used, n_valid, w_gu_all, w_dn_all, layer)
    return _gather_rows(ys, dest), gates


def _final_kernel(xm_ref, y0_ref, y1_ref, g_ref, nfin_ref, out_ref):
    out_ref[...] = _rms(_combined(xm_ref, y0_ref, y1_ref, g_ref), nfin_ref[...])


def _final(xm, yg, gates, norm_final):
    t, d = xm.shape
    ts = min(TS_FINAL, t)
    n_tiles = t // ts
    tile = pl.BlockSpec((ts, d), lambda i: (i, 0))
    return pl.pallas_call(
        _final_kernel,
        grid=(n_tiles,),
        in_specs=[tile, pl.BlockSpec((ts, d // 2), lambda i: (i, 0)),
                  pl.BlockSpec((ts, d // 2), lambda i: (n_tiles + i, 0)),
                  pl.BlockSpec((ts, MOE_TOP_K), lambda i: (i, 0)), pl.BlockSpec((1, d), lambda i: (0, 0))],
        out_specs=tile,
        out_shape=jax.ShapeDtypeStruct((t, d), F32),
        compiler_params=pltpu.CompilerParams(dimension_semantics=("arbitrary",),
                                             vmem_limit_bytes=VMEM_LIMIT_BYTES),
        name="final_combine_norm",
    )(xm, yg, yg, gates, norm_final.reshape(1, d))


def kernel(x, norm_mix, norm_ffn, norm_final, pool_w, pool_scale, sconv_in_w, sconv_taps, sconv_out_w, ssm_in_w, ssm_conv_w, ssm_conv_b, ssm_dt_bias, ssm_a_log, ssm_d, ssm_norm_w, ssm_out_w, router_group_w, router_group_b, router_expert_w, router_expert_b, expert_w_gu, expert_w_down):
    bsz, seq, d = x.shape
    depth = norm_mix.shape[0]
    assert (router_group_w.shape[2], router_expert_w.shape[2]) == ROUTER_GROUPS
    xm = x.reshape(bsz * seq, d)
    prev = None
    for i in range(depth):
        kind, j = i % N_MIXERS, i // N_MIXERS
        ts = min((TS_POOL, TS_SCONV, TS_SSM)[kind], seq)
        router = _router_operands(router_group_w[i], router_group_b[i], router_expert_w[i],
                                  router_expert_b[i], ts)
        if kind == 0:
            outs = _pool_layer(prev, xm, bsz, seq, ts, norm_mix[i], pool_w[j], pool_scale[j], norm_ffn[i], router)
        elif kind == 1:
            outs = _sconv_layer(prev, xm, bsz, seq, ts, norm_mix[i], sconv_in_w[j], sconv_taps[j],
                                sconv_out_w[j], norm_ffn[i], router)
        else:
            outs = _ssm_layer(prev, xm, bsz, seq, ts, norm_mix[i], ssm_in_w[j], ssm_conv_w[j], ssm_conv_b[j],
                              ssm_dt_bias[j], ssm_a_log[j], ssm_d[j], ssm_norm_w[j], ssm_out_w[j],
                              norm_ffn[i], router)
        xm, hp, ids, gate, counts = outs
        prev = _moe(hp, ids, gate, counts, expert_w_gu, expert_w_down, i)
    return _final(xm, prev[0], prev[1], norm_final).reshape(bsz, seq, d)
```

```python
import functools

import jax
import jax.numpy as jnp
from jax import lax
from jax.experimental import pallas as pl
from jax.experimental.pallas import tpu as pltpu
from jax.experimental.pallas import tpu_sc as plsc

F32 = jnp.float32
BF16 = jnp.bfloat16
I32 = jnp.int32

NORM_EPS = 1e-6
N_MIXERS = 3
POOL_WINDOWS = (2, 4, 8, 16)
POOL_HALO = 16
CONV_HALO = 8
SSM_GROUPS = 4
SSM_CHUNK = 128
SSM_CONV_WIDTH = 4
MOE_TOP_K = 2
EXPERTS_PER_GROUP = 8
ROUTER_GROUPS = (4, 32)
ROUTER_ROWS = 48

SUBLANES = 8
LANES = 128
VMEM_LIMIT_BYTES = 56 * 1024 * 1024
SC_CORES = 2
SC_WORKERS = SC_CORES * 16
SC_WINDOW = 64

TS_POOL = 512
TS_SCONV = 512
TS_SSM = 256
TS_FINAL = 512
FFN_BLOCK = 512

NT_DIMS = (((1,), (1,)), ((), ()))
TN_DIMS = (((0,), (0,)), ((), ()))


def _dot(a, b):
    return jnp.dot(a, b, preferred_element_type=F32)


def _dot_nt(a, b):
    return lax.dot_general(a, b, NT_DIMS, preferred_element_type=F32)


def _split3(a):
    a1 = a.astype(BF16)
    r1 = a - a1.astype(F32)
    a2 = r1.astype(BF16)
    a3 = (r1 - a2.astype(F32)).astype(BF16)
    return a1, a2, a3


def _dot3(a, b):
    a1, a2, a3 = _split3(a)
    return _dot(a1, b) + _dot(a2, b) + _dot(a3, b)


def _hilo(a):
    hi = a.astype(BF16)
    lo = (a - hi.astype(F32)).astype(BF16)
    return hi, lo


def _dot_nt_hilo(w_hi, w_lo, a_hi, a_lo):
    return _dot_nt(w_hi, a_hi) + _dot_nt(w_hi, a_lo) + _dot_nt(w_lo, a_hi)


def _rms(x, w):
    return x * lax.rsqrt(jnp.mean(x * x, axis=-1, keepdims=True) + NORM_EPS) * w


def _silu(x):
    half = 0.5 * x
    return half + half * jnp.tanh(half)


def _shift_rows(ext, k, halo):
    return pltpu.roll(ext, k, axis=0)[halo:]


HI_HALF_MASK = -65536


def _pack_bf16(a_bf):
    half = a_bf.shape[1] // 2
    lo = lax.bitcast_convert_type(a_bf[:, :half].astype(F32), I32)
    hi = lax.bitcast_convert_type(a_bf[:, half:].astype(F32), I32)
    return lax.shift_right_logical(lo, 16) | (hi & HI_HALF_MASK)


def _unpack_bf16(w):
    lo = lax.bitcast_convert_type(lax.shift_left(w, 16), F32)
    hi = lax.bitcast_convert_type(w & HI_HALF_MASK, F32)
    return jnp.concatenate([lo, hi], axis=1)


def _combined(xm_ref, y0_ref, y1_ref, g_ref):
    g = g_ref[...]
    return xm_ref[...] + g[:, 0:1] * _unpack_bf16(y0_ref[...]) + g[:, 1:2] * _unpack_bf16(y1_ref[...])


def _route(xm, first, nffn_ref, wr_hi_ref, wr_lo_ref, br_ref, tri_ref, base_ref, hp_ref, ids_ref, gate_ref,
           cnt_ref):
    n_groups, n_experts = ROUTER_GROUPS
    ts = xm.shape[0]

    @pl.when(first)
    def _():
        base_ref[...] = jnp.zeros_like(base_ref)

    h2 = _rms(xm, nffn_ref[...])
    h_hi, h_lo = _hilo(h2)
    hp_ref[...] = _pack_bf16(h_hi)
    lt = _dot_nt_hilo(wr_hi_ref[...], wr_lo_ref[...], h_hi, h_lo) + br_ref[...]

    row8 = lax.broadcasted_iota(I32, (SUBLANES, ts), 0)
    neg_inf = jnp.float32(-jnp.inf)
    g = jnp.where(row8 < n_groups, lt[0:SUBLANES], neg_inf)
    gmax = jnp.max(g, axis=0, keepdims=True)
    gsel = jnp.min(jnp.where(g == gmax, row8, SUBLANES), axis=0, keepdims=True)
    pgrp = 1.0 / jnp.sum(jnp.exp(g - gmax), axis=0, keepdims=True)

    sel = lt[SUBLANES:2 * SUBLANES]
    for j in range(1, n_groups):
        sel = jnp.where(gsel == j, lt[(j + 1) * SUBLANES:(j + 2) * SUBLANES], sel)
    m1 = jnp.max(sel, axis=0, keepdims=True)
    i1 = jnp.min(jnp.where(sel == m1, row8, SUBLANES), axis=0, keepdims=True)
    sel2 = jnp.where(row8 == i1, neg_inf, sel)
    m2 = jnp.max(sel2, axis=0, keepdims=True)
    i2 = jnp.min(jnp.where(sel2 == m2, row8, SUBLANES), axis=0, keepdims=True)
    r = jnp.exp(m2 - m1)
    den = 1.0 + r
    g0 = pgrp / den
    g1 = pgrp * r / den
    e0 = gsel * EXPERTS_PER_GROUP + i1
    e1 = gsel * EXPERTS_PER_GROUP + i2

    row_e = lax.broadcasted_iota(I32, (n_experts, ts), 0)
    oh0 = row_e == e0
    oh1 = row_e == e1
    oh = jnp.where(oh0 | oh1, 1.0, 0.0).astype(F32)
    prefix = _dot(oh.astype(BF16), tri_ref[...])
    tot = prefix + base_ref[:, 0:1]
    rank0 = jnp.sum(jnp.where(oh0, tot, 0.0), axis=0, keepdims=True).astype(I32)
    rank1 = jnp.sum(jnp.where(oh1, tot, 0.0), axis=0, keepdims=True).astype(I32)
    base_ref[...] = base_ref[...] + jnp.sum(oh, axis=1, keepdims=True)

    ids_ref[...] = jnp.where(row8 == 0, e0, jnp.where(row8 == 1, e1, jnp.where(row8 == 2, rank0, rank1)))
    eye = (lax.broadcasted_iota(I32, (SUBLANES, SUBLANES), 0)
           == lax.broadcasted_iota(I32, (SUBLANES, SUBLANES), 1)).astype(BF16)
    gate_ref[...] = sum(lax.dot_general(part, eye, TN_DIMS, preferred_element_type=F32)
                        for part in _split3(jnp.where(row8 == 0, g0, g1)))
    cnt_ref[...] = base_ref[...].astype(I32)


def _router_operands(w_rg, b_rg, w_re, b_re, ts):
    d, n_groups = w_rg.shape
    n_experts = w_re.shape[1]
    assert n_experts == n_groups * EXPERTS_PER_GROUP and n_groups <= SUBLANES
    assert SUBLANES + n_experts <= ROUTER_ROWS
    wt = jnp.zeros((ROUTER_ROWS, d), F32)
    wt = wt.at[0:n_groups].set(w_rg.T).at[SUBLANES:SUBLANES + n_experts].set(w_re.T)
    bt = jnp.zeros((ROUTER_ROWS, 1), F32)
    bt = bt.at[0:n_groups, 0].set(b_rg).at[SUBLANES:SUBLANES + n_experts, 0].set(b_re)
    w_hi = wt.astype(BF16)
    w_lo = (wt - w_hi.astype(F32)).astype(BF16)
    idx = jnp.arange(ts)
    tri = (idx[:, None] < idx[None, :]).astype(BF16)
    return w_hi, w_lo, bt, tri


def _const_spec(shape):
    zeros = (0,) * len(shape)
    return pl.BlockSpec(shape, lambda b, s: zeros)


def _mixer_in(prev, x2d, ts, n_s, d):
    n_tiles = x2d.shape[0] // ts
    tile = pl.BlockSpec((ts, d), lambda b, s: (b * n_s + s, 0))
    if prev is None:
        return [x2d], [tile]
    yg, gates = prev
    specs = [
        tile,
        pl.BlockSpec((ts, d // 2), lambda b, s: (b * n_s + s, 0)),
        pl.BlockSpec((ts, d // 2), lambda b, s: (n_tiles + b * n_s + s, 0)),
        pl.BlockSpec((ts, SUBLANES), lambda b, s: (b * n_s + s, 0)),
    ]
    return [x2d, yg, yg, gates], specs


def _router_in(nffn, router, d, ts):
    w_hi, w_lo, bt, tri = router
    specs = [
        _const_spec((1, d)),
        _const_spec((ROUTER_ROWS, d)),
        _const_spec((ROUTER_ROWS, d)),
        _const_spec((ROUTER_ROWS, 1)),
        _const_spec((ts, ts)),
    ]
    return [nffn.reshape(1, d), w_hi, w_lo, bt, tri], specs


def _mixer_out(t, d, ts, n_s):
    n_tiles = t // ts
    n_experts = ROUTER_GROUPS[1]
    out_shape = [
        jax.ShapeDtypeStruct((t, d), F32),
        jax.ShapeDtypeStruct((t, d // 2), I32),
        jax.ShapeDtypeStruct((SUBLANES, t), I32),
        jax.ShapeDtypeStruct((t, SUBLANES), F32),
        jax.ShapeDtypeStruct((n_experts, LANES), I32),
    ]
    out_specs = [
        pl.BlockSpec((ts, d), lambda b, s: (b * n_s + s, 0)),
        pl.BlockSpec((ts, d // 2), lambda b, s: (b * n_s + s, 0)),
        pl.BlockSpec((SUBLANES, ts), lambda b, s: (0, b * n_s + s)),
        pl.BlockSpec((ts, SUBLANES), lambda b, s: (b * n_s + s, 0)),
        pl.BlockSpec((n_experts, LANES), lambda b, s: (0, 0)),
    ]
    return out_shape, out_specs


def _mixer_call(kern, name, prev, x2d, bsz, seq, ts, mixer_ops, mixer_specs, nffn, router, scratch):
    t, d = x2d.shape
    n_s = seq // ts
    tok_ops, tok_specs = _mixer_in(prev, x2d, ts, n_s, d)
    r_ops, r_specs = _router_in(nffn, router, d, ts)
    out_shape, out_specs = _mixer_out(t, d, ts, n_s)
    return pl.pallas_call(
        functools.partial(kern, n_tok=len(tok_ops)),
        grid=(bsz, n_s),
        in_specs=tok_specs + mixer_specs + r_specs,
        out_specs=out_specs,
        out_shape=out_shape,
        scratch_shapes=scratch + [pltpu.VMEM((ROUTER_GROUPS[1], LANES), F32)],
        compiler_params=pltpu.CompilerParams(dimension_semantics=("arbitrary", "arbitrary"),
                                             vmem_limit_bytes=VMEM_LIMIT_BYTES),
        name=name,
    )(*tok_ops, *mixer_ops, *r_ops)


def _token_input(refs, n_tok):
    return refs[0][...] if n_tok == 1 else _combined(*refs[:n_tok])


def _pool_kernel(*refs, n_tok):
    x = _token_input(refs, n_tok)
    (nmix_ref, pw_ref, scale_ref,
     nffn_ref, wr_hi_ref, wr_lo_ref, br_ref, tri_ref,
     xm_ref, hp_ref, ids_ref, gate_ref, cnt_ref,
     carry_ref, base_ref) = refs[n_tok:]
    b = pl.program_id(0)
    s = pl.program_id(1)
    ts, d = x.shape
    ch = d // len(POOL_WINDOWS)

    @pl.when(s == 0)
    def _():
        carry_ref[...] = jnp.zeros_like(carry_ref)

    h = _rms(x, nmix_ref[...])
    ext = jnp.concatenate([carry_ref[...], h], axis=0)
    carry_ref[...] = h[ts - POOL_HALO:]

    pos = s * ts + lax.broadcasted_iota(I32, (ts, 1), 0) + 1
    ys = []
    for g, win in enumerate(POOL_WINDOWS):
        acc = ext[:, g * ch:(g + 1) * ch]
        k = 1
        while k < win:
            acc = acc + pltpu.roll(acc, k, axis=0)
            k *= 2
        inv_cnt = 1.0 / jnp.minimum(pos, win).astype(F32)
        pooled = acc[POOL_HALO:] * inv_cnt - h[:, g * ch:(g + 1) * ch]
        ys.append(_dot(pooled.astype(BF16), pw_ref[g]))
    y = jnp.concatenate(ys, axis=1) * scale_ref[...]
    xm = x + y
    xm_ref[...] = xm
    _route(xm, (b == 0) & (s == 0), nffn_ref, wr_hi_ref, wr_lo_ref, br_ref, tri_ref, base_ref,
           hp_ref, ids_ref, gate_ref, cnt_ref)


def _pool_layer(prev, x2d, bsz, seq, ts, nmix, pool_w, pool_scale, nffn, router):
    d = x2d.shape[1]
    n_win, ch, _ = pool_w.shape
    assert n_win == len(POOL_WINDOWS) and ch * n_win == d
    ops = [nmix.reshape(1, d), pool_w.astype(BF16), pool_scale.reshape(1, d)]
    specs = [_const_spec((1, d)), _const_spec((n_win, ch, ch)), _const_spec((1, d))]
    return _mixer_call(_pool_kernel, "pool_mixer", prev, x2d, bsz, seq, ts, ops, specs, nffn, router,
                       [pltpu.VMEM((POOL_HALO, d), F32)])


def _sconv_kernel(*refs, n_tok):
    x = _token_input(refs, n_tok)
    (nmix_ref, win_ref, taps_ref, wout_ref,
     nffn_ref, wr_hi_ref, wr_lo_ref, br_ref, tri_ref,
     xm_ref, hp_ref, ids_ref, gate_ref, cnt_ref,
     carry_ref, base_ref) = refs[n_tok:]
    b = pl.program_id(0)
    s = pl.program_id(1)
    ts, d = x.shape

    @pl.when(s == 0)
    def _():
        carry_ref[...] = jnp.zeros_like(carry_ref)

    h = _rms(x, nmix_ref[...]).astype(BF16)
    b_gate = _dot(h, win_ref[:, 0:d])
    c_gate = _dot(h, win_ref[:, d:2 * d])
    v = _dot(h, win_ref[:, 2 * d:3 * d])
    u_pre = c_gate * v
    ext = jnp.concatenate([carry_ref[...], u_pre], axis=0)
    carry_ref[...] = u_pre[ts - CONV_HALO:]
    width = taps_ref.shape[0]
    u = taps_ref[width - 1:width, :] * u_pre
    for k in range(1, width):
        u = u + taps_ref[width - 1 - k:width - k, :] * _shift_rows(ext, k, CONV_HALO)
    mix = _dot((b_gate * u).astype(BF16), wout_ref[...])
    xm = x + mix
    xm_ref[...] = xm
    _route(xm, (b == 0) & (s == 0), nffn_ref, wr_hi_ref, wr_lo_ref, br_ref, tri_ref, base_ref,
           hp_ref, ids_ref, gate_ref, cnt_ref)


def _sconv_layer(prev, x2d, bsz, seq, ts, nmix, w_in, taps, w_out, nffn, router):
    d = x2d.shape[1]
    width = taps.shape[0]
    assert width - 1 <= CONV_HALO
    ops = [nmix.reshape(1, d), w_in.astype(BF16), taps, w_out.astype(BF16)]
    specs = [_const_spec((1, d)), _const_spec((d, 3 * d)), _const_spec((width, d)), _const_spec((d, d))]
    return _mixer_call(_sconv_kernel, "sconv_mixer", prev, x2d, bsz, seq, ts, ops, specs, nffn, router,
                       [pltpu.VMEM((CONV_HALO, d), F32)])


def _ssm_kernel(*refs, n_tok):
    x = _token_input(refs, n_tok)
    (nmix_ref, wz_ref, wxbc_ref, wdt_hi_ref, wdt_lo_ref, convw_ref, convb_ref,
     dtb_ref, alog_ref, expand_ref, dskip_ref, normw_ref, wout_ref, triq_ref,
     nffn_ref, wr_hi_ref, wr_lo_ref, br_ref, tri_ref,
     xm_ref, hp_ref, ids_ref, gate_ref, cnt_ref,
     carry_ref, z_ref, xs_ref, bm_ref, cm_ref, y_ref, dtt_ref, at_ref, state_ref, base_ref) = refs[n_tok:]
    b = pl.program_id(0)
    s = pl.program_id(1)
    ts, d = x.shape
    di = z_ref.shape[1]
    gn = bm_ref.shape[1]
    cd = di + 2 * gn
    nh = alog_ref.shape[0]
    hp = di // nh
    grp = SSM_GROUPS
    st = gn // grp
    hpg = nh // grp
    gw = hpg * hp
    q = SSM_CHUNK
    n_chunks = ts // q

    @pl.when(s == 0)
    def _():
        carry_ref[...] = jnp.zeros_like(carry_ref)
        state_ref[...] = jnp.zeros_like(state_ref)

    h = _rms(x, nmix_ref[...])
    h_hi, h_lo = _hilo(h)

    z_ref[...] = _dot(h_hi, wz_ref[...])
    col_blk = gn
    for c0 in range(0, cd, col_blk):
        cols = slice(c0, c0 + col_blk)
        pre = _dot(h_hi, wxbc_ref[:, cols])
        ext = jnp.concatenate([carry_ref[:, cols], pre], axis=0)
        carry_ref[:, cols] = pre[ts - CONV_HALO:]
        ext1 = pltpu.roll(ext, 1, axis=0)
        pair = convw_ref[1:2, cols] * ext + convw_ref[0:1, cols] * ext1
        acc = (convw_ref[3:4, cols] * pre + convb_ref[:, cols] + convw_ref[2:3, cols] * ext1[CONV_HALO:]
               + _shift_rows(pair, 2, CONV_HALO))
        act = _silu(acc)
        if c0 < di:
            xs_ref[:, cols] = act
        elif c0 < di + gn:
            bm_ref[...] = act
        else:
            cm_ref[...] = act

    dt_t = _dot_nt_hilo(wdt_hi_ref[...], wdt_lo_ref[...], h_hi, h_lo) + dtb_ref[...]
    dt_t = jnp.maximum(dt_t, 0.0) + jnp.log(1.0 + jnp.exp(-jnp.abs(dt_t)))
    a_t = dt_t * (-jnp.exp(alog_ref[...]))
    for c in range(n_chunks):
        dtt_ref[c] = dt_t[:, c * q:(c + 1) * q]
        at_ref[c] = a_t[:, c * q:(c + 1) * q]

    row_q = lax.broadcasted_iota(I32, (q, q), 0)
    col_q = lax.broadcasted_iota(I32, (q, q), 1)
    causal = row_q >= col_q
    lane_lo = lax.broadcasted_iota(I32, (q, 2 * hp), 1) < hp
    neg_inf = jnp.float32(-jnp.inf)
    expand = expand_ref[...]

    def chunk_body(c, carry):
        r0 = pl.multiple_of(c * q, q)
        rows = pl.ds(r0, q)
        acs_t = _dot3(at_ref[c], triq_ref[...])
        dt_c = dtt_ref[c]
        w_t = dt_c * jnp.exp(acs_t[:, q - 1:q] - acs_t)
        src_t = acs_t - jnp.log(dt_c)
        acs = acs_t.T
        od_hi, od_lo = _hilo(jnp.exp(acs))
        out_decay_e = _dot(od_hi, expand) + _dot(od_lo, expand)
        w_e = _dot(w_t.T.astype(BF16), expand)
        xc = xs_ref[rows, :]
        xw = xc * w_e
        bc = bm_ref[rows, :]
        cc = cm_ref[rows, :]
        for g in range(grp):
            gcols = slice(g * gw, (g + 1) * gw)
            bg = bc[:, g * st:(g + 1) * st].astype(BF16)
            cg = cc[:, g * st:(g + 1) * st].astype(BF16)
            xg = xc[:, gcols]
            cb = _dot_nt(cg, bg)
            hprev = state_ref[g]
            y_off = _dot(cg, hprev.astype(BF16)) * out_decay_e[:, gcols]
            new_states = lax.dot_general(bg, xw[:, gcols].astype(BF16), TN_DIMS, preferred_element_type=F32)
            state_ref[g] = hprev * out_decay_e[q - 1:q, gcols] + new_states
            for j in range(hpg // 2):
                h0 = g * hpg + 2 * j
                ms = []
                for hh in (h0, h0 + 1):
                    seg = acs[:, hh:hh + 1] - src_t[hh:hh + 1, :]
                    ms.append((cb * jnp.exp(jnp.where(causal, seg, neg_inf))).astype(BF16))
                lhs = jnp.concatenate(ms, axis=1)
                xp = xg[:, 2 * j * hp:(2 * j + 2) * hp]
                rhs = jnp.concatenate([jnp.where(lane_lo, xp, 0.0), jnp.where(lane_lo, 0.0, xp)],
                                      axis=0).astype(BF16)
                pc = slice(g * gw + 2 * j * hp, g * gw + (2 * j + 2) * hp)
                y_ref[rows, pc] = _dot(lhs, rhs) + y_off[:, 2 * j * hp:(2 * j + 2) * hp]
        return carry

    lax.fori_loop(0, n_chunks, chunk_body, 0)

    out = None
    for g in range(grp):
        gcols = slice(g * gw, (g + 1) * gw)
        yg = (y_ref[:, gcols] + xs_ref[:, gcols] * dskip_ref[:, gcols]) * _silu(z_ref[:, gcols])
        yn = yg * lax.rsqrt(jnp.mean(yg * yg, axis=-1, keepdims=True) + NORM_EPS) * normw_ref[:, gcols]
        part = _dot(yn.astype(BF16), wout_ref[gcols, :])
        out = part if out is None else out + part
    xm = x + out
    xm_ref[...] = xm
    _route(xm, (b == 0) & (s == 0), nffn_ref, wr_hi_ref, wr_lo_ref, br_ref, tri_ref, base_ref,
           hp_ref, ids_ref, gate_ref, cnt_ref)


def _ssm_layer(prev, x2d, bsz, seq, ts, nmix, w_in, conv_w, conv_b, dt_bias, a_log, d_skip, norm_w, w_out,
               nffn, router):
    d = x2d.shape[1]
    assert ts % SSM_CHUNK == 0
    nh = a_log.shape[0]
    di = norm_w.shape[0]
    cd = conv_w.shape[1]
    gn = (cd - di) // 2
    hp = di // nh
    width = conv_w.shape[0]
    assert width == SSM_CONV_WIDTH and (nh // SSM_GROUPS) % 2 == 0 and di % gn == 0
    st = gn // SSM_GROUPS
    gw = di // SSM_GROUPS
    q = SSM_CHUNK
    n_chunks = ts // q

    wz = w_in[:, :di].astype(BF16)
    wxbc = w_in[:, di:di + cd].astype(BF16)
    wdt_t = w_in[:, di + cd:].T
    wdt_hi = wdt_t.astype(BF16)
    wdt_lo = (wdt_t - wdt_hi.astype(F32)).astype(BF16)
    expand = (jnp.arange(di)[None, :] // hp == jnp.arange(nh)[:, None]).astype(BF16)
    dskip_e = jnp.repeat(d_skip, hp).reshape(1, di)
    idx = jnp.arange(q)
    triq = (idx[:, None] <= idx[None, :]).astype(BF16)

    ops = [nmix.reshape(1, d), wz, wxbc, wdt_hi, wdt_lo, conv_w, conv_b.reshape(1, cd),
           dt_bias.reshape(nh, 1), a_log.reshape(nh, 1), expand, dskip_e, norm_w.reshape(1, di),
           w_out.astype(BF16), triq]
    specs = [
        _const_spec((1, d)),
        _const_spec((d, di)),
        _const_spec((d, cd)),
        _const_spec((nh, d)),
        _const_spec((nh, d)),
        _const_spec((width, cd)),
        _const_spec((1, cd)),
        _const_spec((nh, 1)),
        _const_spec((nh, 1)),
        _const_spec((nh, di)),
        _const_spec((1, di)),
        _const_spec((1, di)),
        _const_spec((di, d)),
        _const_spec((q, q)),
    ]
    scratch = [
        pltpu.VMEM((CONV_HALO, cd), F32),
        pltpu.VMEM((ts, di), F32),
        pltpu.VMEM((ts, di), F32),
        pltpu.VMEM((ts, gn), F32),
        pltpu.VMEM((ts, gn), F32),
        pltpu.VMEM((ts, di), F32),
        pltpu.VMEM((n_chunks, nh, q), F32),
        pltpu.VMEM((n_chunks, nh, q), F32),
        pltpu.VMEM((SSM_GROUPS, st, gw), F32),
    ]
    return _mixer_call(_ssm_kernel, "ssm_mixer", prev, x2d, bsz, seq, ts, ops, specs, nffn, router, scratch)


def _sc_mesh():
    return plsc.VectorSubcoreMesh(core_axis_name="core", subcore_axis_name="subcore")


def _sc_worker_base(per_worker):
    return (lax.axis_index("subcore") * SC_CORES + lax.axis_index("core")) * per_worker


def _sc_scratch(d, dtype):
    win = SC_WINDOW
    return [pltpu.VMEM((win,), I32), pltpu.VMEM((win,), I32),
            pltpu.VMEM((win, d), dtype), pltpu.VMEM((win, d), dtype),
            pltpu.SemaphoreType.DMA, pltpu.SemaphoreType.DMA]


def _scatter_rows(src, idx, n_rows):
    n_src, d = src.shape
    m = idx.shape[0]
    win = SC_WINDOW
    per_worker = m // SC_WORKERS
    n_win = per_worker // win
    assert m == per_worker * SC_WORKERS and per_worker == n_win * win and n_win % 2 == 0
    assert n_src % per_worker == 0

    @functools.partial(pl.kernel, out_type=jax.ShapeDtypeStruct((n_rows, d), src.dtype), mesh=_sc_mesh(),
                       scratch_types=_sc_scratch(d, src.dtype), name="moe_scatter_rows")
    def scatter_kernel(x_hbm, i_hbm, o_hbm, idx_a, idx_b, rows_a, rows_b, sem_a, sem_b):
        base = _sc_worker_base(per_worker)
        src_base = lax.rem(base, n_src)

        def step(w, idx_v, rows_v, sem, first):
            if not first:
                pltpu.make_async_copy(rows_v, o_hbm.at[idx_v], sem).wait()
            pltpu.sync_copy(i_hbm.at[pl.ds(base + w * win, win)], idx_v)
            pltpu.sync_copy(x_hbm.at[pl.ds(src_base + w * win, win)], rows_v)
            pltpu.make_async_copy(rows_v, o_hbm.at[idx_v], sem).start()

        step(0, idx_a, rows_a, sem_a, True)
        step(1, idx_b, rows_b, sem_b, True)

        @pl.loop(2, n_win, step=2)
        def _(w):
            step(w, idx_a, rows_a, sem_a, False)
            step(w + 1, idx_b, rows_b, sem_b, False)

        pltpu.make_async_copy(rows_a, o_hbm.at[idx_a], sem_a).wait()
        pltpu.make_async_copy(rows_b, o_hbm.at[idx_b], sem_b).wait()

    return scatter_kernel(src, idx)


def _gather_rows(table, idx):
    d = table.shape[1]
    m = idx.shape[0]
    win = SC_WINDOW
    per_worker = m // SC_WORKERS
    n_win = per_worker // win
    assert m == per_worker * SC_WORKERS and per_worker == n_win * win and n_win % 2 == 0

    @functools.partial(pl.kernel, out_type=jax.ShapeDtypeStruct((m, d), table.dtype), mesh=_sc_mesh(),
                       scratch_types=_sc_scratch(d, table.dtype), name="moe_gather_rows")
    def gather_kernel(x_hbm, i_hbm, o_hbm, idx_a, idx_b, rows_a, rows_b, sem_a, sem_b):
        base = _sc_worker_base(per_worker)

        def out_copy(w, rows_v, sem):
            return pltpu.make_async_copy(rows_v, o_hbm.at[pl.ds(base + w * win, win)], sem)

        def step(w, idx_v, rows_v, sem, first):
            pltpu.sync_copy(i_hbm.at[pl.ds(base + w * win, win)], idx_v)
            if not first:
                out_copy(w, rows_v, sem).wait()
            pltpu.sync_copy(x_hbm.at[idx_v], rows_v)
            out_copy(w, rows_v, sem).start()

        step(0, idx_a, rows_a, sem_a, True)
        step(1, idx_b, rows_b, sem_b, True)

        @pl.loop(2, n_win, step=2)
        def _(w):
            step(w, idx_a, rows_a, sem_a, False)
            step(w + 1, idx_b, rows_b, sem_b, False)

        out_copy(0, rows_a, sem_a).wait()
        out_copy(0, rows_b, sem_b).wait()

    return gather_kernel(table, idx)


def _ffn_kernel(be_ref, nu_ref, nv_ref, slot_ref, nxt_ref, xs_ref, wgu_hbm, wdn_hbm, ys_ref,
                wgu_f32_ref, wdn_f32_ref, wgu_bf_ref, wdn_bf_ref, sem_ref, *, layer):
    i = pl.program_id(0)
    bm = xs_ref.shape[0]
    f = wdn_bf_ref.shape[0]

    def weight_copies(expert, slot):
        return (pltpu.make_async_copy(wgu_hbm.at[layer, expert], wgu_f32_ref.at[slot], sem_ref.at[slot, 0]),
                pltpu.make_async_copy(wdn_hbm.at[layer, expert], wdn_f32_ref.at[slot], sem_ref.at[slot, 1]))

    @pl.when(i == 0)
    def _():
        for cp in weight_copies(be_ref[0], slot_ref[0]):
            cp.start()

    first_of_expert = (i == 0) | (be_ref[i] != be_ref[jnp.maximum(i - 1, 0)])

    @pl.when(first_of_expert & (i < nu_ref[0]))
    def _():
        slot = slot_ref[i]
        for cp in weight_copies(be_ref[i], slot):
            cp.wait()

        @pl.when(nxt_ref[i] >= 0)
        def _():
            for cp in weight_copies(nxt_ref[i], 1 - slot):
                cp.start()

        wgu_bf_ref[...] = wgu_f32_ref[slot].astype(BF16)
        wdn_bf_ref[...] = wdn_f32_ref[slot].astype(BF16)

    @pl.when(i < nu_ref[0])
    def _():
        valid = lax.broadcasted_iota(I32, (bm, 1), 0) < nv_ref[i]
        h = _unpack_bf16(jnp.where(valid, xs_ref[...], 0)).astype(BF16)
        gu = _dot(h, wgu_bf_ref[...])
        act = _silu(gu[:, :f]) * gu[:, f:]
        ys_ref[...] = _pack_bf16(_dot(act.astype(BF16), wdn_bf_ref[...]).astype(BF16))

    @pl.when(i >= nu_ref[0])
    def _():
        ys_ref[...] = jnp.zeros_like(ys_ref)


def _weight_schedule(block_e, n_used):
    n_blocks = block_e.shape[0]
    idx = jnp.arange(n_blocks, dtype=I32)
    starts_segment = jnp.concatenate([jnp.ones((1,), bool), block_e[1:] != block_e[:-1]])
    slot = ((jnp.cumsum(starts_segment.astype(I32)) - 1) % 2).astype(I32)
    later_start = lax.cummin(jnp.where(starts_segment, idx, n_blocks)[::-1])[::-1]
    next_start = jnp.concatenate([later_start[1:], jnp.full((1,), n_blocks, I32)])
    next_e = jnp.where(next_start < n_used[0], block_e[jnp.minimum(next_start, n_blocks - 1)], -1)
    return slot, next_e.astype(I32)


def _expert_ffn(xs, block_e, n_used, n_valid, w_gu_all, w_dn_all, layer):
    n_rows, dh = xs.shape
    bm = FFN_BLOCK
    n_blocks = n_rows // bm
    d, f = w_gu_all.shape[2], w_dn_all.shape[2]
    assert d == 2 * dh

    slot, next_e = _weight_schedule(block_e, n_used)

    def xs_map(i, be, nu, nv, sl, nx):
        return (jnp.minimum(i, jnp.maximum(nu[0] - 1, 0)), 0)

    return pl.pallas_call(
        functools.partial(_ffn_kernel, layer=layer),
        grid_spec=pltpu.PrefetchScalarGridSpec(
            num_scalar_prefetch=5,
            grid=(n_blocks,),
            in_specs=[
                pl.BlockSpec((bm, dh), xs_map),
                pl.BlockSpec(memory_space=pl.ANY),
                pl.BlockSpec(memory_space=pl.ANY),
            ],
            out_specs=pl.BlockSpec((bm, dh), lambda i, be, nu, nv, sl, nx: (i, 0)),
            scratch_shapes=[
                pltpu.VMEM((2, d, 2 * f), F32),
                pltpu.VMEM((2, f, d), F32),
                pltpu.VMEM((d, 2 * f), BF16),
                pltpu.VMEM((f, d), BF16),
                pltpu.SemaphoreType.DMA((2, 2)),
            ],
        ),
        out_shape=jax.ShapeDtypeStruct((n_rows, dh), I32),
        compiler_params=pltpu.CompilerParams(dimension_semantics=("arbitrary",),
                                             vmem_limit_bytes=VMEM_LIMIT_BYTES),
        name="expert_ffn",
    )(block_e, n_used, n_valid, slot, next_e, xs, w_gu_all, w_dn_all)


def _moe(hp, ids, gate, counts, w_gu_all, w_dn_all, layer):
    t = hp.shape[0]
    n_exp = w_gu_all.shape[1]
    bm = FFN_BLOCK
    n_assign = t * MOE_TOP_K
    n_rows = (n_assign + bm - 1) // bm * bm + n_exp * bm
    n_blocks = n_rows // bm

    expert = ids[0:MOE_TOP_K]
    rank = ids[MOE_TOP_K:2 * MOE_TOP_K]
    cnt = counts[:, 0]
    padded = (cnt + bm - 1) // bm * bm
    pad_end = jnp.cumsum(padded)
    pad_start = pad_end - padded
    dest = rank
    for e in range(n_exp):
        dest = dest + jnp.where(expert == e, pad_start[e], 0)
    dest = dest.astype(I32).reshape(-1)
    blk_row = jnp.arange(n_blocks, dtype=I32) * bm
    block_e = jnp.minimum(jnp.sum(blk_row[:, None] >= pad_end[None, :], axis=1), n_exp - 1).astype(I32)
    n_used = (pad_end[-1:] // bm).astype(I32)
    n_valid = jnp.clip(pad_start[block_e] + cnt[block_e] - blk_row, 0, bm).astype(I32)

    xs = _scatter_rows(hp, dest, n_rows)
    ys = _expert_ffn(xs, block_e, n_used, n_valid, w_gu_all, w_dn_all, layer)
    return _gather_rows(ys, dest), gate


def _final_kernel(xm_ref, y0_ref, y1_ref, g_ref, nfin_ref, out_ref):
    out_ref[...] = _rms(_combined(xm_ref, y0_ref, y1_ref, g_ref), nfin_ref[...])


def _final(xm, yg, gates, norm_final):
    t, d = xm.shape
    ts = min(TS_FINAL, t)
    n_tiles = t // ts
    tile = pl.BlockSpec((ts, d), lambda i: (i, 0))
    return pl.pallas_call(
        _final_kernel,
        grid=(n_tiles,),
        in_specs=[tile, pl.BlockSpec((ts, d // 2), lambda i: (i, 0)),
                  pl.BlockSpec((ts, d // 2), lambda i: (n_tiles + i, 0)),
                  pl.BlockSpec((ts, SUBLANES), lambda i: (i, 0)), pl.BlockSpec((1, d), lambda i: (0, 0))],
        out_specs=tile,
        out_shape=jax.ShapeDtypeStruct((t, d), F32),
        compiler_params=pltpu.CompilerParams(dimension_semantics=("arbitrary",),
                                             vmem_limit_bytes=VMEM_LIMIT_BYTES),
        name="final_combine_norm",
    )(xm, yg, yg, gates, norm_final.reshape(1, d))


def kernel(x, norm_mix, norm_ffn, norm_final, pool_w, pool_scale, sconv_in_w, sconv_taps, sconv_out_w, ssm_in_w, ssm_conv_w, ssm_conv_b, ssm_dt_bias, ssm_a_log, ssm_d, ssm_norm_w, ssm_out_w, router_group_w, router_group_b, router_expert_w, router_expert_b, expert_w_gu, expert_w_down):
    bsz, seq, d = x.shape
    depth = norm_mix.shape[0]
    assert (router_group_w.shape[2], router_expert_w.shape[2]) == ROUTER_GROUPS
    xm = x.reshape(bsz * seq, d)
    prev = None
    for i in range(depth):
        kind, j = i % N_MIXERS, i // N_MIXERS
        ts = min((TS_POOL, TS_SCONV, TS_SSM)[kind], seq)
        router = _router_operands(router_group_w[i], router_group_b[i], router_expert_w[i],
                                  router_expert_b[i], ts)
        if kind == 0:
            outs = _pool_layer(prev, xm, bsz, seq, ts, norm_mix[i], pool_w[j], pool_scale[j], norm_ffn[i], router)
        elif kind == 1:
            outs = _sconv_layer(prev, xm, bsz, seq, ts, norm_mix[i], sconv_in_w[j], sconv_taps[j],
                                sconv_out_w[j], norm_ffn[i], router)
        else:
            outs = _ssm_layer(prev, xm, bsz, seq, ts, norm_mix[i], ssm_in_w[j], ssm_conv_w[j], ssm_conv_b[j],
                              ssm_dt_bias[j], ssm_a_log[j], ssm_d[j], ssm_norm_w[j], ssm_out_w[j],
                              norm_ffn[i], router)
        xm, hp, ids, gate, counts = outs
        prev = _moe(hp, ids, gate, counts, expert_w_gu, expert_w_down, i)
    return _final(xm, prev[0], prev[1], norm_final).reshape(bsz, seq, d)
```

```python
import functools

import jax
import jax.numpy as jnp
from jax import lax
from jax.experimental import pallas as pl
from jax.experimental.pallas import tpu as pltpu
from jax.experimental.pallas import tpu_sc as plsc

F32 = jnp.float32
BF16 = jnp.bfloat16
I32 = jnp.int32

NORM_EPS = 1e-6
N_MIXERS = 3
POOL_WINDOWS = (2, 4, 8, 16)
POOL_HALO = 16
CONV_HALO = 8
SSM_GROUPS = 4
SSM_CHUNK = 128
SSM_CONV_WIDTH = 4
MOE_TOP_K = 2
EXPERTS_PER_GROUP = 8
ROUTER_GROUPS = (4, 32)
ROUTER_ROWS = 48

SUBLANES = 8
LANES = 128
VMEM_LIMIT_BYTES = 56 * 1024 * 1024
SC_CORES = 2
SC_WORKERS = SC_CORES * 16
SC_WINDOW = 64

TS_POOL = 512
TS_SCONV = 512
TS_SSM = 256
TS_FINAL = 512
TS_DEST = 4096
FFN_BLOCK = 512

NT_DIMS = (((1,), (1,)), ((), ()))
TN_DIMS = (((0,), (0,)), ((), ()))


def _dot(a, b):
    return jnp.dot(a, b, preferred_element_type=F32)


def _dot_nt(a, b):
    return lax.dot_general(a, b, NT_DIMS, preferred_element_type=F32)


def _split3(a):
    a1 = a.astype(BF16)
    r1 = a - a1.astype(F32)
    a2 = r1.astype(BF16)
    a3 = (r1 - a2.astype(F32)).astype(BF16)
    return a1, a2, a3


def _dot3(a, b):
    a1, a2, a3 = _split3(a)
    return _dot(a1, b) + _dot(a2, b) + _dot(a3, b)


def _hilo(a):
    hi = a.astype(BF16)
    lo = (a - hi.astype(F32)).astype(BF16)
    return hi, lo


def _dot_nt_hilo(w_hi, w_lo, a_hi, a_lo):
    return _dot_nt(w_hi, a_hi) + _dot_nt(w_hi, a_lo) + _dot_nt(w_lo, a_hi)


def _rms(x, w):
    return x * lax.rsqrt(jnp.mean(x * x, axis=-1, keepdims=True) + NORM_EPS) * w


def _silu(x):
    half = 0.5 * x
    return half + half * jnp.tanh(half)


def _shift_rows(ext, k, halo):
    return pltpu.roll(ext, k, axis=0)[halo:]


HI_HALF_MASK = -65536


def _pack_bf16(a_bf):
    half = a_bf.shape[1] // 2
    lo = lax.bitcast_convert_type(a_bf[:, :half].astype(F32), I32)
    hi = lax.bitcast_convert_type(a_bf[:, half:].astype(F32), I32)
    return lax.shift_right_logical(lo, 16) | (hi & HI_HALF_MASK)


def _unpack_bf16(w):
    lo = lax.bitcast_convert_type(lax.shift_left(w, 16), F32)
    hi = lax.bitcast_convert_type(w & HI_HALF_MASK, F32)
    return jnp.concatenate([lo, hi], axis=1)


def _combined(xm_ref, y0_ref, y1_ref, g_ref):
    g = g_ref[...]
    return xm_ref[...] + g[:, 0:1] * _unpack_bf16(y0_ref[...]) + g[:, 1:2] * _unpack_bf16(y1_ref[...])


def _route(xm, first, nffn_ref, wr_hi_ref, wr_lo_ref, br_ref, tri_ref, base_ref, hp_ref, ids_ref, gate_ref,
           cnt_ref):
    n_groups, n_experts = ROUTER_GROUPS
    ts = xm.shape[0]

    @pl.when(first)
    def _():
        base_ref[...] = jnp.zeros_like(base_ref)

    h2 = _rms(xm, nffn_ref[...])
    h_hi, h_lo = _hilo(h2)
    hp_ref[...] = _pack_bf16(h_hi)
    lt = _dot_nt_hilo(wr_hi_ref[...], wr_lo_ref[...], h_hi, h_lo) + br_ref[...]

    row8 = lax.broadcasted_iota(I32, (SUBLANES, ts), 0)
    neg_inf = jnp.float32(-jnp.inf)
    g = jnp.where(row8 < n_groups, lt[0:SUBLANES], neg_inf)
    gmax = jnp.max(g, axis=0, keepdims=True)
    gsel = jnp.min(jnp.where(g == gmax, row8, SUBLANES), axis=0, keepdims=True)
    pgrp = 1.0 / jnp.sum(jnp.exp(g - gmax), axis=0, keepdims=True)

    sel = lt[SUBLANES:2 * SUBLANES]
    for j in range(1, n_groups):
        sel = jnp.where(gsel == j, lt[(j + 1) * SUBLANES:(j + 2) * SUBLANES], sel)
    m1 = jnp.max(sel, axis=0, keepdims=True)
    i1 = jnp.min(jnp.where(sel == m1, row8, SUBLANES), axis=0, keepdims=True)
    sel2 = jnp.where(row8 == i1, neg_inf, sel)
    m2 = jnp.max(sel2, axis=0, keepdims=True)
    i2 = jnp.min(jnp.where(sel2 == m2, row8, SUBLANES), axis=0, keepdims=True)
    r = jnp.exp(m2 - m1)
    den = 1.0 + r
    g0 = pgrp / den
    g1 = pgrp * r / den
    e0 = gsel * EXPERTS_PER_GROUP + i1
    e1 = gsel * EXPERTS_PER_GROUP + i2

    row_e = lax.broadcasted_iota(I32, (n_experts, ts), 0)
    oh0 = row_e == e0
    oh1 = row_e == e1
    oh = jnp.where(oh0 | oh1, 1.0, 0.0).astype(F32)
    prefix = _dot(oh.astype(BF16), tri_ref[...])
    tot = prefix + base_ref[:, 0:1]
    rank0 = jnp.sum(jnp.where(oh0, tot, 0.0), axis=0, keepdims=True).astype(I32)
    rank1 = jnp.sum(jnp.where(oh1, tot, 0.0), axis=0, keepdims=True).astype(I32)
    base_ref[...] = base_ref[...] + jnp.sum(oh, axis=1, keepdims=True)

    ids_ref[...] = jnp.where(row8 == 0, e0, jnp.where(row8 == 1, e1, jnp.where(row8 == 2, rank0, rank1)))
    eye = (lax.broadcasted_iota(I32, (SUBLANES, SUBLANES), 0)
           == lax.broadcasted_iota(I32, (SUBLANES, SUBLANES), 1)).astype(BF16)
    gate_ref[...] = sum(lax.dot_general(part, eye, TN_DIMS, preferred_element_type=F32)
                        for part in _split3(jnp.where(row8 == 0, g0, g1)))
    cnt_ref[...] = base_ref[...].astype(I32)


def _router_operands(w_rg, b_rg, w_re, b_re, ts):
    d, n_groups = w_rg.shape
    n_experts = w_re.shape[1]
    assert n_experts == n_groups * EXPERTS_PER_GROUP and n_groups <= SUBLANES
    assert SUBLANES + n_experts <= ROUTER_ROWS
    wt = jnp.zeros((ROUTER_ROWS, d), F32)
    wt = wt.at[0:n_groups].set(w_rg.T).at[SUBLANES:SUBLANES + n_experts].set(w_re.T)
    bt = jnp.zeros((ROUTER_ROWS, 1), F32)
    bt = bt.at[0:n_groups, 0].set(b_rg).at[SUBLANES:SUBLANES + n_experts, 0].set(b_re)
    w_hi = wt.astype(BF16)
    w_lo = (wt - w_hi.astype(F32)).astype(BF16)
    idx = jnp.arange(ts)
    tri = (idx[:, None] < idx[None, :]).astype(BF16)
    return w_hi, w_lo, bt, tri


def _const_spec(shape):
    zeros = (0,) * len(shape)
    return pl.BlockSpec(shape, lambda b, s: zeros)


def _mixer_in(prev, x2d, ts, n_s, d):
    n_tiles = x2d.shape[0] // ts
    tile = pl.BlockSpec((ts, d), lambda b, s: (b * n_s + s, 0))
    if prev is None:
        return [x2d], [tile]
    yg, gates = prev
    specs = [
        tile,
        pl.BlockSpec((ts, d // 2), lambda b, s: (b * n_s + s, 0)),
        pl.BlockSpec((ts, d // 2), lambda b, s: (n_tiles + b * n_s + s, 0)),
        pl.BlockSpec((ts, SUBLANES), lambda b, s: (b * n_s + s, 0)),
    ]
    return [x2d, yg, yg, gates], specs


def _router_in(nffn, router, d, ts):
    w_hi, w_lo, bt, tri = router
    specs = [
        _const_spec((1, d)),
        _const_spec((ROUTER_ROWS, d)),
        _const_spec((ROUTER_ROWS, d)),
        _const_spec((ROUTER_ROWS, 1)),
        _const_spec((ts, ts)),
    ]
    return [nffn.reshape(1, d), w_hi, w_lo, bt, tri], specs


def _mixer_out(t, d, ts, n_s):
    n_tiles = t // ts
    n_experts = ROUTER_GROUPS[1]
    out_shape = [
        jax.ShapeDtypeStruct((t, d), F32),
        jax.ShapeDtypeStruct((t, d // 2), I32),
        jax.ShapeDtypeStruct((SUBLANES, t), I32),
        jax.ShapeDtypeStruct((t, SUBLANES), F32),
        jax.ShapeDtypeStruct((n_experts, LANES), I32),
    ]
    out_specs = [
        pl.BlockSpec((ts, d), lambda b, s: (b * n_s + s, 0)),
        pl.BlockSpec((ts, d // 2), lambda b, s: (b * n_s + s, 0)),
        pl.BlockSpec((SUBLANES, ts), lambda b, s: (0, b * n_s + s)),
        pl.BlockSpec((ts, SUBLANES), lambda b, s: (b * n_s + s, 0)),
        pl.BlockSpec((n_experts, LANES), lambda b, s: (0, 0)),
    ]
    return out_shape, out_specs


def _mixer_call(kern, name, prev, x2d, bsz, seq, ts, mixer_ops, mixer_specs, nffn, router, scratch):
    t, d = x2d.shape
    n_s = seq // ts
    tok_ops, tok_specs = _mixer_in(prev, x2d, ts, n_s, d)
    r_ops, r_specs = _router_in(nffn, router, d, ts)
    out_shape, out_specs = _mixer_out(t, d, ts, n_s)
    return pl.pallas_call(
        functools.partial(kern, n_tok=len(tok_ops)),
        grid=(bsz, n_s),
        in_specs=tok_specs + mixer_specs + r_specs,
        out_specs=out_specs,
        out_shape=out_shape,
        scratch_shapes=scratch + [pltpu.VMEM((ROUTER_GROUPS[1], LANES), F32)],
        compiler_params=pltpu.CompilerParams(dimension_semantics=("arbitrary", "arbitrary"),
                                             vmem_limit_bytes=VMEM_LIMIT_BYTES),
        name=name,
    )(*tok_ops, *mixer_ops, *r_ops)


def _token_input(refs, n_tok):
    return refs[0][...] if n_tok == 1 else _combined(*refs[:n_tok])


def _pool_kernel(*refs, n_tok):
    x = _token_input(refs, n_tok)
    (nmix_ref, pw_ref, scale_ref,
     nffn_ref, wr_hi_ref, wr_lo_ref, br_ref, tri_ref,
     xm_ref, hp_ref, ids_ref, gate_ref, cnt_ref,
     carry_ref, base_ref) = refs[n_tok:]
    b = pl.program_id(0)
    s = pl.program_id(1)
    ts, d = x.shape
    ch = d // len(POOL_WINDOWS)

    @pl.when(s == 0)
    def _():
        carry_ref[...] = jnp.zeros_like(carry_ref)

    h = _rms(x, nmix_ref[...])
    ext = jnp.concatenate([carry_ref[...], h], axis=0)
    carry_ref[...] = h[ts - POOL_HALO:]

    pos = s * ts + lax.broadcasted_iota(I32, (ts, 1), 0) + 1
    ys = []
    for g, win in enumerate(POOL_WINDOWS):
        acc = ext[:, g * ch:(g + 1) * ch]
        k = 1
        while k < win:
            acc = acc + pltpu.roll(acc, k, axis=0)
            k *= 2
        inv_cnt = 1.0 / jnp.minimum(pos, win).astype(F32)
        pooled = acc[POOL_HALO:] * inv_cnt - h[:, g * ch:(g + 1) * ch]
        ys.append(_dot(pooled.astype(BF16), pw_ref[g]))
    y = jnp.concatenate(ys, axis=1) * scale_ref[...]
    xm = x + y
    xm_ref[...] = xm
    _route(xm, (b == 0) & (s == 0), nffn_ref, wr_hi_ref, wr_lo_ref, br_ref, tri_ref, base_ref,
           hp_ref, ids_ref, gate_ref, cnt_ref)


def _pool_layer(prev, x2d, bsz, seq, ts, nmix, pool_w, pool_scale, nffn, router):
    d = x2d.shape[1]
    n_win, ch, _ = pool_w.shape
    assert n_win == len(POOL_WINDOWS) and ch * n_win == d
    ops = [nmix.reshape(1, d), pool_w.astype(BF16), pool_scale.reshape(1, d)]
    specs = [_const_spec((1, d)), _const_spec((n_win, ch, ch)), _const_spec((1, d))]
    return _mixer_call(_pool_kernel, "pool_mixer", prev, x2d, bsz, seq, ts, ops, specs, nffn, router,
                       [pltpu.VMEM((POOL_HALO, d), F32)])


def _sconv_kernel(*refs, n_tok):
    x = _token_input(refs, n_tok)
    (nmix_ref, win_ref, taps_ref, wout_ref,
     nffn_ref, wr_hi_ref, wr_lo_ref, br_ref, tri_ref,
     xm_ref, hp_ref, ids_ref, gate_ref, cnt_ref,
     carry_ref, base_ref) = refs[n_tok:]
    b = pl.program_id(0)
    s = pl.program_id(1)
    ts, d = x.shape

    @pl.when(s == 0)
    def _():
        carry_ref[...] = jnp.zeros_like(carry_ref)

    h = _rms(x, nmix_ref[...]).astype(BF16)
    b_gate = _dot(h, win_ref[:, 0:d])
    c_gate = _dot(h, win_ref[:, d:2 * d])
    v = _dot(h, win_ref[:, 2 * d:3 * d])
    u_pre = c_gate * v
    ext = jnp.concatenate([carry_ref[...], u_pre], axis=0)
    carry_ref[...] = u_pre[ts - CONV_HALO:]
    width = taps_ref.shape[0]
    u = taps_ref[width - 1:width, :] * u_pre
    for k in range(1, width):
        u = u + taps_ref[width - 1 - k:width - k, :] * _shift_rows(ext, k, CONV_HALO)
    mix = _dot((b_gate * u).astype(BF16), wout_ref[...])
    xm = x + mix
    xm_ref[...] = xm
    _route(xm, (b == 0) & (s == 0), nffn_ref, wr_hi_ref, wr_lo_ref, br_ref, tri_ref, base_ref,
           hp_ref, ids_ref, gate_ref, cnt_ref)


def _sconv_layer(prev, x2d, bsz, seq, ts, nmix, w_in, taps, w_out, nffn, router):
    d = x2d.shape[1]
    width = taps.shape[0]
    assert width - 1 <= CONV_HALO
    ops = [nmix.reshape(1, d), w_in.astype(BF16), taps, w_out.astype(BF16)]
    specs = [_const_spec((1, d)), _const_spec((d, 3 * d)), _const_spec((width, d)), _const_spec((d, d))]
    return _mixer_call(_sconv_kernel, "sconv_mixer", prev, x2d, bsz, seq, ts, ops, specs, nffn, router,
                       [pltpu.VMEM((CONV_HALO, d), F32)])


def _ssm_kernel(*refs, n_tok):
    x = _token_input(refs, n_tok)
    (nmix_ref, wz_ref, wxbc_ref, wdt_hi_ref, wdt_lo_ref, convw_ref, convb_ref,
     dtb_ref, alog_ref, expand_ref, dskip_ref, normw_ref, wout_ref, triq_ref,
     nffn_ref, wr_hi_ref, wr_lo_ref, br_ref, tri_ref,
     xm_ref, hp_ref, ids_ref, gate_ref, cnt_ref,
     carry_ref, z_ref, xs_ref, bm_ref, cm_ref, y_ref, dtt_ref, at_ref, state_ref, base_ref) = refs[n_tok:]
    b = pl.program_id(0)
    s = pl.program_id(1)
    ts, d = x.shape
    di = z_ref.shape[1]
    gn = bm_ref.shape[1]
    cd = di + 2 * gn
    nh = alog_ref.shape[0]
    hp = di // nh
    grp = SSM_GROUPS
    st = gn // grp
    hpg = nh // grp
    gw = hpg * hp
    q = SSM_CHUNK
    n_chunks = ts // q

    @pl.when(s == 0)
    def _():
        carry_ref[...] = jnp.zeros_like(carry_ref)
        state_ref[...] = jnp.zeros_like(state_ref)

    h = _rms(x, nmix_ref[...])
    h_hi, h_lo = _hilo(h)

    z_ref[...] = _dot(h_hi, wz_ref[...])
    col_blk = gn
    for c0 in range(0, cd, col_blk):
        cols = slice(c0, c0 + col_blk)
        pre = _dot(h_hi, wxbc_ref[:, cols])
        ext = jnp.concatenate([carry_ref[:, cols], pre], axis=0)
        carry_ref[:, cols] = pre[ts - CONV_HALO:]
        ext1 = pltpu.roll(ext, 1, axis=0)
        pair = convw_ref[1:2, cols] * ext + convw_ref[0:1, cols] * ext1
        acc = (convw_ref[3:4, cols] * pre + convb_ref[:, cols] + convw_ref[2:3, cols] * ext1[CONV_HALO:]
               + _shift_rows(pair, 2, CONV_HALO))
        act = _silu(acc)
        if c0 < di:
            xs_ref[:, cols] = act
        elif c0 < di + gn:
            bm_ref[...] = act
        else:
            cm_ref[...] = act

    dt_t = _dot_nt_hilo(wdt_hi_ref[...], wdt_lo_ref[...], h_hi, h_lo) + dtb_ref[...]
    dt_t = jnp.maximum(dt_t, 0.0) + jnp.log(1.0 + jnp.exp(-jnp.abs(dt_t)))
    a_t = dt_t * (-jnp.exp(alog_ref[...]))
    for c in range(n_chunks):
        dtt_ref[c] = dt_t[:, c * q:(c + 1) * q]
        at_ref[c] = a_t[:, c * q:(c + 1) * q]

    row_q = lax.broadcasted_iota(I32, (q, q), 0)
    col_q = lax.broadcasted_iota(I32, (q, q), 1)
    causal = row_q >= col_q
    lane_lo = lax.broadcasted_iota(I32, (q, 2 * hp), 1) < hp
    neg_inf = jnp.float32(-jnp.inf)
    expand = expand_ref[...]

    def chunk_body(c, carry):
        r0 = pl.multiple_of(c * q, q)
        rows = pl.ds(r0, q)
        acs_t = _dot3(at_ref[c], triq_ref[...])
        dt_c = dtt_ref[c]
        w_t = dt_c * jnp.exp(acs_t[:, q - 1:q] - acs_t)
        src_t = acs_t - jnp.log(dt_c)
        acs = acs_t.T
        od_hi, od_lo = _hilo(jnp.exp(acs))
        out_decay_e = _dot(od_hi, expand) + _dot(od_lo, expand)
        w_e = _dot(w_t.T.astype(BF16), expand)
        xc = xs_ref[rows, :]
        xw = xc * w_e
        bc = bm_ref[rows, :]
        cc = cm_ref[rows, :]
        for g in range(grp):
            gcols = slice(g * gw, (g + 1) * gw)
            bg = bc[:, g * st:(g + 1) * st].astype(BF16)
            cg = cc[:, g * st:(g + 1) * st].astype(BF16)
            xg = xc[:, gcols]
            cb = _dot_nt(cg, bg)
            hprev = state_ref[g]
            y_off = _dot(cg, hprev.astype(BF16)) * out_decay_e[:, gcols]
            new_states = lax.dot_general(bg, xw[:, gcols].astype(BF16), TN_DIMS, preferred_element_type=F32)
            state_ref[g] = hprev * out_decay_e[q - 1:q, gcols] + new_states
            for j in range(hpg // 2):
                h0 = g * hpg + 2 * j
                ms = []
                for hh in (h0, h0 + 1):
                    seg = acs[:, hh:hh + 1] - src_t[hh:hh + 1, :]
                    ms.append((cb * jnp.exp(jnp.where(causal, seg, neg_inf))).astype(BF16))
                lhs = jnp.concatenate(ms, axis=1)
                xp = xg[:, 2 * j * hp:(2 * j + 2) * hp]
                rhs = jnp.concatenate([jnp.where(lane_lo, xp, 0.0), jnp.where(lane_lo, 0.0, xp)],
                                      axis=0).astype(BF16)
                pc = slice(g * gw + 2 * j * hp, g * gw + (2 * j + 2) * hp)
                y_ref[rows, pc] = _dot(lhs, rhs) + y_off[:, 2 * j * hp:(2 * j + 2) * hp]
        return carry

    lax.fori_loop(0, n_chunks, chunk_body, 0)

    out = None
    for g in range(grp):
        gcols = slice(g * gw, (g + 1) * gw)
        yg = (y_ref[:, gcols] + xs_ref[:, gcols] * dskip_ref[:, gcols]) * _silu(z_ref[:, gcols])
        yn = yg * lax.rsqrt(jnp.mean(yg * yg, axis=-1, keepdims=True) + NORM_EPS) * normw_ref[:, gcols]
        part = _dot(yn.astype(BF16), wout_ref[gcols, :])
        out = part if out is None else out + part
    xm = x + out
    xm_ref[...] = xm
    _route(xm, (b == 0) & (s == 0), nffn_ref, wr_hi_ref, wr_lo_ref, br_ref, tri_ref, base_ref,
           hp_ref, ids_ref, gate_ref, cnt_ref)


def _ssm_layer(prev, x2d, bsz, seq, ts, nmix, w_in, conv_w, conv_b, dt_bias, a_log, d_skip, norm_w, w_out,
               nffn, router):
    d = x2d.shape[1]
    assert ts % SSM_CHUNK == 0
    nh = a_log.shape[0]
    di = norm_w.shape[0]
    cd = conv_w.shape[1]
    gn = (cd - di) // 2
    hp = di // nh
    width = conv_w.shape[0]
    assert width == SSM_CONV_WIDTH and (nh // SSM_GROUPS) % 2 == 0 and di % gn == 0
    st = gn // SSM_GROUPS
    gw = di // SSM_GROUPS
    q = SSM_CHUNK
    n_chunks = ts // q

    wz = w_in[:, :di].astype(BF16)
    wxbc = w_in[:, di:di + cd].astype(BF16)
    wdt_t = w_in[:, di + cd:].T
    wdt_hi = wdt_t.astype(BF16)
    wdt_lo = (wdt_t - wdt_hi.astype(F32)).astype(BF16)
    expand = (jnp.arange(di)[None, :] // hp == jnp.arange(nh)[:, None]).astype(BF16)
    dskip_e = jnp.repeat(d_skip, hp).reshape(1, di)
    idx = jnp.arange(q)
    triq = (idx[:, None] <= idx[None, :]).astype(BF16)

    ops = [nmix.reshape(1, d), wz, wxbc, wdt_hi, wdt_lo, conv_w, conv_b.reshape(1, cd),
           dt_bias.reshape(nh, 1), a_log.reshape(nh, 1), expand, dskip_e, norm_w.reshape(1, di),
           w_out.astype(BF16), triq]
    specs = [
        _const_spec((1, d)),
        _const_spec((d, di)),
        _const_spec((d, cd)),
        _const_spec((nh, d)),
        _const_spec((nh, d)),
        _const_spec((width, cd)),
        _const_spec((1, cd)),
        _const_spec((nh, 1)),
        _const_spec((nh, 1)),
        _const_spec((nh, di)),
        _const_spec((1, di)),
        _const_spec((1, di)),
        _const_spec((di, d)),
        _const_spec((q, q)),
    ]
    scratch = [
        pltpu.VMEM((CONV_HALO, cd), F32),
        pltpu.VMEM((ts, di), F32),
        pltpu.VMEM((ts, di), F32),
        pltpu.VMEM((ts, gn), F32),
        pltpu.VMEM((ts, gn), F32),
        pltpu.VMEM((ts, di), F32),
        pltpu.VMEM((n_chunks, nh, q), F32),
        pltpu.VMEM((n_chunks, nh, q), F32),
        pltpu.VMEM((SSM_GROUPS, st, gw), F32),
    ]
    return _mixer_call(_ssm_kernel, "ssm_mixer", prev, x2d, bsz, seq, ts, ops, specs, nffn, router, scratch)


def _sc_mesh():
    return plsc.VectorSubcoreMesh(core_axis_name="core", subcore_axis_name="subcore")


def _sc_worker_base(per_worker):
    return (lax.axis_index("subcore") * SC_CORES + lax.axis_index("core")) * per_worker


def _sc_scratch(d, dtype):
    win = SC_WINDOW
    return [pltpu.VMEM((win,), I32), pltpu.VMEM((win,), I32),
            pltpu.VMEM((win, d), dtype), pltpu.VMEM((win, d), dtype),
            pltpu.SemaphoreType.DMA, pltpu.SemaphoreType.DMA]


def _scatter_rows(src, idx, n_rows):
    n_src, d = src.shape
    m = idx.shape[0]
    win = SC_WINDOW
    per_worker = m // SC_WORKERS
    n_win = per_worker // win
    assert m == per_worker * SC_WORKERS and per_worker == n_win * win and n_win % 2 == 0
    assert n_src % per_worker == 0

    @functools.partial(pl.kernel, out_type=jax.ShapeDtypeStruct((n_rows, d), src.dtype), mesh=_sc_mesh(),
                       scratch_types=_sc_scratch(d, src.dtype), name="moe_scatter_rows")
    def scatter_kernel(x_hbm, i_hbm, o_hbm, idx_a, idx_b, rows_a, rows_b, sem_a, sem_b):
        base = _sc_worker_base(per_worker)
        src_base = lax.rem(base, n_src)

        def step(w, idx_v, rows_v, sem, first):
            if not first:
                pltpu.make_async_copy(rows_v, o_hbm.at[idx_v], sem).wait()
            pltpu.sync_copy(i_hbm.at[pl.ds(base + w * win, win)], idx_v)
            pltpu.sync_copy(x_hbm.at[pl.ds(src_base + w * win, win)], rows_v)
            pltpu.make_async_copy(rows_v, o_hbm.at[idx_v], sem).start()

        step(0, idx_a, rows_a, sem_a, True)
        step(1, idx_b, rows_b, sem_b, True)

        @pl.loop(2, n_win, step=2)
        def _(w):
            step(w, idx_a, rows_a, sem_a, False)
            step(w + 1, idx_b, rows_b, sem_b, False)

        pltpu.make_async_copy(rows_a, o_hbm.at[idx_a], sem_a).wait()
        pltpu.make_async_copy(rows_b, o_hbm.at[idx_b], sem_b).wait()

    return scatter_kernel(src, idx)


def _gather_rows(table, idx):
    d = table.shape[1]
    m = idx.shape[0]
    win = SC_WINDOW
    per_worker = m // SC_WORKERS
    n_win = per_worker // win
    assert m == per_worker * SC_WORKERS and per_worker == n_win * win and n_win % 2 == 0

    @functools.partial(pl.kernel, out_type=jax.ShapeDtypeStruct((m, d), table.dtype), mesh=_sc_mesh(),
                       scratch_types=_sc_scratch(d, table.dtype), name="moe_gather_rows")
    def gather_kernel(x_hbm, i_hbm, o_hbm, idx_a, idx_b, rows_a, rows_b, sem_a, sem_b):
        base = _sc_worker_base(per_worker)

        def out_copy(w, rows_v, sem):
            return pltpu.make_async_copy(rows_v, o_hbm.at[pl.ds(base + w * win, win)], sem)

        def step(w, idx_v, rows_v, sem, first):
            pltpu.sync_copy(i_hbm.at[pl.ds(base + w * win, win)], idx_v)
            if not first:
                out_copy(w, rows_v, sem).wait()
            pltpu.sync_copy(x_hbm.at[idx_v], rows_v)
            out_copy(w, rows_v, sem).start()

        step(0, idx_a, rows_a, sem_a, True)
        step(1, idx_b, rows_b, sem_b, True)

        @pl.loop(2, n_win, step=2)
        def _(w):
            step(w, idx_a, rows_a, sem_a, False)
            step(w + 1, idx_b, rows_b, sem_b, False)

        out_copy(0, rows_a, sem_a).wait()
        out_copy(0, rows_b, sem_b).wait()

    return gather_kernel(table, idx)


def _ffn_kernel(be_ref, nu_ref, nv_ref, slot_ref, nxt_ref, xs_ref, wgu_hbm, wdn_hbm, ys_ref,
                wgu_f32_ref, wdn_f32_ref, wgu_bf_ref, wdn_bf_ref, sem_ref, *, layer):
    i = pl.program_id(0)
    bm = xs_ref.shape[0]
    f = wdn_bf_ref.shape[0]

    def weight_copies(expert, slot):
        return (pltpu.make_async_copy(wgu_hbm.at[layer, expert], wgu_f32_ref.at[slot], sem_ref.at[slot, 0]),
                pltpu.make_async_copy(wdn_hbm.at[layer, expert], wdn_f32_ref.at[slot], sem_ref.at[slot, 1]))

    @pl.when(i == 0)
    def _():
        for cp in weight_copies(be_ref[0], slot_ref[0]):
            cp.start()

    first_of_expert = (i == 0) | (be_ref[i] != be_ref[jnp.maximum(i - 1, 0)])

    @pl.when(first_of_expert & (i < nu_ref[0]))
    def _():
        slot = slot_ref[i]
        for cp in weight_copies(be_ref[i], slot):
            cp.wait()

        @pl.when(nxt_ref[i] >= 0)
        def _():
            for cp in weight_copies(nxt_ref[i], 1 - slot):
                cp.start()

        wgu_bf_ref[...] = wgu_f32_ref[slot].astype(BF16)
        wdn_bf_ref[...] = wdn_f32_ref[slot].astype(BF16)

    @pl.when(i < nu_ref[0])
    def _():
        valid = lax.broadcasted_iota(I32, (bm, 1), 0) < nv_ref[i]
        h = _unpack_bf16(jnp.where(valid, xs_ref[...], 0)).astype(BF16)
        gu = _dot(h, wgu_bf_ref[...])
        act = _silu(gu[:, :f]) * gu[:, f:]
        ys_ref[...] = _pack_bf16(_dot(act.astype(BF16), wdn_bf_ref[...]).astype(BF16))

    @pl.when(i >= nu_ref[0])
    def _():
        ys_ref[...] = jnp.zeros_like(ys_ref)


def _block_schedule(cnt, n_blocks, bm):
    n_exp = cnt.shape[0]
    padded = (cnt + bm - 1) // bm * bm
    e_ids = jnp.arange(n_exp, dtype=I32)
    before = e_ids[None, :] < e_ids[:, None]
    pad_start = jnp.sum(jnp.where(before, padded[None, :], 0), axis=1).astype(I32)
    pad_end = pad_start + padded
    nonempty = cnt > 0
    seg_of = jnp.sum(before & nonempty[None, :], axis=1)
    after = (e_ids[None, :] > e_ids[:, None]) & nonempty[None, :]
    next_of = jnp.min(jnp.where(after, e_ids[None, :], n_exp), axis=1)
    next_of = jnp.where(next_of == n_exp, -1, next_of)
    blk_row = jnp.arange(n_blocks, dtype=I32) * bm
    block_e = jnp.minimum(jnp.sum(blk_row[:, None] >= pad_end[None, :], axis=1), n_exp - 1).astype(I32)
    of_block = block_e[:, None] == e_ids[None, :]

    def per_block(v):
        return jnp.sum(jnp.where(of_block, v[None, :], 0), axis=1).astype(I32)

    n_used = (jnp.sum(padded, keepdims=True) // bm).astype(I32)
    n_valid = jnp.clip(per_block(pad_start + cnt) - blk_row, 0, bm).astype(I32)
    return pad_start, (block_e, n_used, n_valid, per_block(seg_of % 2), per_block(next_of))


def _expert_ffn(xs, schedule, w_gu_all, w_dn_all, layer):
    n_rows, dh = xs.shape
    bm = FFN_BLOCK
    n_blocks = n_rows // bm
    d, f = w_gu_all.shape[2], w_dn_all.shape[2]
    assert d == 2 * dh

    def xs_map(i, be, nu, nv, sl, nx):
        return (jnp.minimum(i, jnp.maximum(nu[0] - 1, 0)), 0)

    return pl.pallas_call(
        functools.partial(_ffn_kernel, layer=layer),
        grid_spec=pltpu.PrefetchScalarGridSpec(
            num_scalar_prefetch=5,
            grid=(n_blocks,),
            in_specs=[
                pl.BlockSpec((bm, dh), xs_map),
                pl.BlockSpec(memory_space=pl.ANY),
                pl.BlockSpec(memory_space=pl.ANY),
            ],
            out_specs=pl.BlockSpec((bm, dh), lambda i, be, nu, nv, sl, nx: (i, 0)),
            scratch_shapes=[
                pltpu.VMEM((2, d, 2 * f), F32),
                pltpu.VMEM((2, f, d), F32),
                pltpu.VMEM((d, 2 * f), BF16),
                pltpu.VMEM((f, d), BF16),
                pltpu.SemaphoreType.DMA((2, 2)),
            ],
        ),
        out_shape=jax.ShapeDtypeStruct((n_rows, dh), I32),
        compiler_params=pltpu.CompilerParams(dimension_semantics=("arbitrary",),
                                             vmem_limit_bytes=VMEM_LIMIT_BYTES),
        name="expert_ffn",
    )(*schedule, xs, w_gu_all, w_dn_all)


def _dest_kernel(pad_start_ref, ids_ref, dest_ref):
    expert = ids_ref[0:MOE_TOP_K, :]
    dest = ids_ref[MOE_TOP_K:2 * MOE_TOP_K, :]
    for e in range(pad_start_ref.shape[0]):
        dest = dest + jnp.where(expert == e, pad_start_ref[e], 0)
    dest_ref[...] = dest


def _assignment_rows(ids, pad_start):
    t = ids.shape[1]
    tn = min(TS_DEST, t)
    assert t % tn == 0
    return pl.pallas_call(
        _dest_kernel,
        grid_spec=pltpu.PrefetchScalarGridSpec(
            num_scalar_prefetch=1,
            grid=(t // tn,),
            in_specs=[pl.BlockSpec((SUBLANES, tn), lambda i, ps: (0, i))],
            out_specs=pl.BlockSpec((MOE_TOP_K, tn), lambda i, ps: (0, i)),
        ),
        out_shape=jax.ShapeDtypeStruct((MOE_TOP_K, t), I32),
        compiler_params=pltpu.CompilerParams(dimension_semantics=("arbitrary",)),
        name="moe_dest",
    )(pad_start, ids)


def _moe(hp, ids, gate, counts, w_gu_all, w_dn_all, layer):
    t = hp.shape[0]
    n_exp = w_gu_all.shape[1]
    bm = FFN_BLOCK
    n_assign = t * MOE_TOP_K
    n_rows = (n_assign + bm - 1) // bm * bm + n_exp * bm
    n_blocks = n_rows // bm

    pad_start, schedule = _block_schedule(counts[:, 0], n_blocks, bm)
    dest = _assignment_rows(ids, pad_start).reshape(-1)
    xs = _scatter_rows(hp, dest, n_rows)
    ys = _expert_ffn(xs, schedule, w_gu_all, w_dn_all, layer)
    return _gather_rows(ys, dest), gate


def _final_kernel(xm_ref, y0_ref, y1_ref, g_ref, nfin_ref, out_ref):
    out_ref[...] = _rms(_combined(xm_ref, y0_ref, y1_ref, g_ref), nfin_ref[...])


def _final(xm, yg, gates, norm_final):
    t, d = xm.shape
    ts = min(TS_FINAL, t)
    n_tiles = t // ts
    tile = pl.BlockSpec((ts, d), lambda i: (i, 0))
    return pl.pallas_call(
        _final_kernel,
        grid=(n_tiles,),
        in_specs=[tile, pl.BlockSpec((ts, d // 2), lambda i: (i, 0)),
                  pl.BlockSpec((ts, d // 2), lambda i: (n_tiles + i, 0)),
                  pl.BlockSpec((ts, SUBLANES), lambda i: (i, 0)), pl.BlockSpec((1, d), lambda i: (0, 0))],
        out_specs=tile,
        out_shape=jax.ShapeDtypeStruct((t, d), F32),
        compiler_params=pltpu.CompilerParams(dimension_semantics=("arbitrary",),
                                             vmem_limit_bytes=VMEM_LIMIT_BYTES),
        name="final_combine_norm",
    )(xm, yg, yg, gates, norm_final.reshape(1, d))


def kernel(x, norm_mix, norm_ffn, norm_final, pool_w, pool_scale, sconv_in_w, sconv_taps, sconv_out_w, ssm_in_w, ssm_conv_w, ssm_conv_b, ssm_dt_bias, ssm_a_log, ssm_d, ssm_norm_w, ssm_out_w, router_group_w, router_group_b, router_expert_w, router_expert_b, expert_w_gu, expert_w_down):
    bsz, seq, d = x.shape
    depth = norm_mix.shape[0]
    assert (router_group_w.shape[2], router_expert_w.shape[2]) == ROUTER_GROUPS
    xm = x.reshape(bsz * seq, d)
    prev = None
    for i in range(depth):
        kind, j = i % N_MIXERS, i // N_MIXERS
        ts = min((TS_POOL, TS_SCONV, TS_SSM)[kind], seq)
        router = _router_operands(router_group_w[i], router_group_b[i], router_expert_w[i],
                                  router_expert_b[i], ts)
        if kind == 0:
            outs = _pool_layer(prev, xm, bsz, seq, ts, norm_mix[i], pool_w[j], pool_scale[j], norm_ffn[i], router)
        elif kind == 1:
            outs = _sconv_layer(prev, xm, bsz, seq, ts, norm_mix[i], sconv_in_w[j], sconv_taps[j],
                                sconv_out_w[j], norm_ffn[i], router)
        else:
            outs = _ssm_layer(prev, xm, bsz, seq, ts, norm_mix[i], ssm_in_w[j], ssm_conv_w[j], ssm_conv_b[j],
                              ssm_dt_bias[j], ssm_a_log[j], ssm_d[j], ssm_norm_w[j], ssm_out_w[j],
                              norm_ffn[i], router)
        xm, hp, ids, gate, counts = outs
        prev = _moe(hp, ids, gate, counts, expert_w_gu, expert_w_down, i)
    return _final(xm, prev[0], prev[1], norm_final).reshape(bsz, seq, d)
```

```python
import functools
from typing import NamedTuple

import jax
import jax.numpy as jnp
from jax import lax
from jax.experimental import pallas as pl
from jax.experimental.pallas import tpu as pltpu
from jax.experimental.pallas import tpu_sc as plsc

F32 = jnp.float32
BF16 = jnp.bfloat16
I32 = jnp.int32

NORM_EPS = 1e-6
N_MIXERS = 3
POOL_WINDOWS = (2, 4, 8, 16)
POOL_HALO = 16
CONV_HALO = 8
SSM_GROUPS = 4
SSM_CHUNK = 128
SSM_CONV_WIDTH = 4
MOE_TOP_K = 2
TOKEN_GROUPS = 2
EXPERTS_PER_GROUP = 8
ROUTER_GROUPS = (4, 32)
ROUTER_ROWS = 48

SUBLANES = 8
LANES = 128
VMEM_LIMIT_BYTES = 56 * 1024 * 1024
SC_CORES = 2
SC_WORKERS = SC_CORES * 16
SC_WINDOW = 64

TS_POOL = 512
TS_SCONV = 512
TS_SSM = 256
TS_FINAL = 512
TS_DEST = 4096
FFN_BLOCK = 512

NT_DIMS = (((1,), (1,)), ((), ()))
TN_DIMS = (((0,), (0,)), ((), ()))


def _dot(a, b):
    return jnp.dot(a, b, preferred_element_type=F32)


def _dot_nt(a, b):
    return lax.dot_general(a, b, NT_DIMS, preferred_element_type=F32)


def _split3(a):
    a1 = a.astype(BF16)
    r1 = a - a1.astype(F32)
    a2 = r1.astype(BF16)
    a3 = (r1 - a2.astype(F32)).astype(BF16)
    return a1, a2, a3


def _dot3(a, b):
    a1, a2, a3 = _split3(a)
    return _dot(a1, b) + _dot(a2, b) + _dot(a3, b)


def _hilo(a):
    hi = a.astype(BF16)
    lo = (a - hi.astype(F32)).astype(BF16)
    return hi, lo


def _dot_nt_hilo(w_hi, w_lo, a_hi, a_lo):
    return _dot_nt(w_hi, a_hi) + _dot_nt(w_hi, a_lo) + _dot_nt(w_lo, a_hi)


def _rms(x, w):
    return x * lax.rsqrt(jnp.mean(x * x, axis=-1, keepdims=True) + NORM_EPS) * w


def _silu(x):
    half = 0.5 * x
    return half + half * jnp.tanh(half)


def _shift_rows(ext, k, halo):
    return pltpu.roll(ext, k, axis=0)[halo:]


HI_HALF_MASK = -65536


def _pack_bf16(a_bf):
    half = a_bf.shape[1] // 2
    lo = lax.bitcast_convert_type(a_bf[:, :half].astype(F32), I32)
    hi = lax.bitcast_convert_type(a_bf[:, half:].astype(F32), I32)
    return lax.shift_right_logical(lo, 16) | (hi & HI_HALF_MASK)


def _unpack_bf16(w):
    lo = lax.bitcast_convert_type(lax.shift_left(w, 16), F32)
    hi = lax.bitcast_convert_type(w & HI_HALF_MASK, F32)
    return jnp.concatenate([lo, hi], axis=1)


def _combined(xm_ref, y0_ref, y1_ref, g_ref):
    g = g_ref[...]
    return xm_ref[...] + g[:, 0:1] * _unpack_bf16(y0_ref[...]) + g[:, 1:2] * _unpack_bf16(y1_ref[...])


def _route(xm, first, nffn_ref, wr_hi_ref, wr_lo_ref, br_ref, tri_ref, base_ref, hp_ref, ids_ref, gate_ref,
           cnt_ref):
    n_groups, n_experts = ROUTER_GROUPS
    ts = xm.shape[0]

    @pl.when(first)
    def _():
        base_ref[...] = jnp.zeros_like(base_ref)

    h2 = _rms(xm, nffn_ref[...])
    h_hi, h_lo = _hilo(h2)
    hp_ref[...] = _pack_bf16(h_hi)
    lt = _dot_nt_hilo(wr_hi_ref[...], wr_lo_ref[...], h_hi, h_lo) + br_ref[...]

    row8 = lax.broadcasted_iota(I32, (SUBLANES, ts), 0)
    neg_inf = jnp.float32(-jnp.inf)
    g = jnp.where(row8 < n_groups, lt[0:SUBLANES], neg_inf)
    gmax = jnp.max(g, axis=0, keepdims=True)
    gsel = jnp.min(jnp.where(g == gmax, row8, SUBLANES), axis=0, keepdims=True)
    pgrp = 1.0 / jnp.sum(jnp.exp(g - gmax), axis=0, keepdims=True)

    sel = lt[SUBLANES:2 * SUBLANES]
    for j in range(1, n_groups):
        sel = jnp.where(gsel == j, lt[(j + 1) * SUBLANES:(j + 2) * SUBLANES], sel)
    m1 = jnp.max(sel, axis=0, keepdims=True)
    i1 = jnp.min(jnp.where(sel == m1, row8, SUBLANES), axis=0, keepdims=True)
    sel2 = jnp.where(row8 == i1, neg_inf, sel)
    m2 = jnp.max(sel2, axis=0, keepdims=True)
    i2 = jnp.min(jnp.where(sel2 == m2, row8, SUBLANES), axis=0, keepdims=True)
    r = jnp.exp(m2 - m1)
    den = 1.0 + r
    g0 = pgrp / den
    g1 = pgrp * r / den
    e0 = gsel * EXPERTS_PER_GROUP + i1
    e1 = gsel * EXPERTS_PER_GROUP + i2

    row_e = lax.broadcasted_iota(I32, (n_experts, ts), 0)
    oh0 = row_e == e0
    oh1 = row_e == e1
    oh = jnp.where(oh0 | oh1, 1.0, 0.0).astype(F32)
    prefix = _dot(oh.astype(BF16), tri_ref[...])
    tot = prefix + base_ref[:, 0:1]
    rank0 = jnp.sum(jnp.where(oh0, tot, 0.0), axis=0, keepdims=True).astype(I32)
    rank1 = jnp.sum(jnp.where(oh1, tot, 0.0), axis=0, keepdims=True).astype(I32)
    base_ref[...] = base_ref[...] + jnp.sum(oh, axis=1, keepdims=True)

    ids_ref[...] = jnp.where(row8 == 0, e0, jnp.where(row8 == 1, e1, jnp.where(row8 == 2, rank0, rank1)))
    eye = (lax.broadcasted_iota(I32, (SUBLANES, SUBLANES), 0)
           == lax.broadcasted_iota(I32, (SUBLANES, SUBLANES), 1)).astype(BF16)
    gate_ref[...] = sum(lax.dot_general(part, eye, TN_DIMS, preferred_element_type=F32)
                        for part in _split3(jnp.where(row8 == 0, g0, g1)))
    cnt_ref[...] = base_ref[...].astype(I32)


def _router_operands(w_rg, b_rg, w_re, b_re, ts):
    d, n_groups = w_rg.shape
    n_experts = w_re.shape[1]
    assert n_experts == n_groups * EXPERTS_PER_GROUP and n_groups <= SUBLANES
    assert SUBLANES + n_experts <= ROUTER_ROWS
    wt = jnp.zeros((ROUTER_ROWS, d), F32)
    wt = wt.at[0:n_groups].set(w_rg.T).at[SUBLANES:SUBLANES + n_experts].set(w_re.T)
    bt = jnp.zeros((ROUTER_ROWS, 1), F32)
    bt = bt.at[0:n_groups, 0].set(b_rg).at[SUBLANES:SUBLANES + n_experts, 0].set(b_re)
    w_hi = wt.astype(BF16)
    w_lo = (wt - w_hi.astype(F32)).astype(BF16)
    idx = jnp.arange(ts)
    tri = (idx[:, None] < idx[None, :]).astype(BF16)
    return w_hi, w_lo, bt, tri


def _const_spec(shape):
    zeros = (0,) * len(shape)
    return pl.BlockSpec(shape, lambda b, s: zeros)


class Stream(NamedTuple):
    rows: jax.Array
    tile0: int


def _mixer_in(prev, src, t, ts, n_s, d):
    n_tiles = t // ts
    tile0 = src.tile0
    tile = pl.BlockSpec((ts, d), lambda b, s: (tile0 + b * n_s + s, 0))
    if prev is None:
        return [src.rows], [tile]
    yg, gates = prev
    specs = [
        tile,
        pl.BlockSpec((ts, d // 2), lambda b, s: (b * n_s + s, 0)),
        pl.BlockSpec((ts, d // 2), lambda b, s: (n_tiles + b * n_s + s, 0)),
        pl.BlockSpec((ts, SUBLANES), lambda b, s: (b * n_s + s, 0)),
    ]
    return [src.rows, yg, yg, gates], specs


def _router_in(nffn, router, d, ts):
    w_hi, w_lo, bt, tri = router
    specs = [
        _const_spec((1, d)),
        _const_spec((ROUTER_ROWS, d)),
        _const_spec((ROUTER_ROWS, d)),
        _const_spec((ROUTER_ROWS, 1)),
        _const_spec((ts, ts)),
    ]
    return [nffn.reshape(1, d), w_hi, w_lo, bt, tri], specs


def _mixer_out(t, d, ts, n_s):
    n_tiles = t // ts
    n_experts = ROUTER_GROUPS[1]
    out_shape = [
        jax.ShapeDtypeStruct((t, d), F32),
        jax.ShapeDtypeStruct((t, d // 2), I32),
        jax.ShapeDtypeStruct((SUBLANES, t), I32),
        jax.ShapeDtypeStruct((t, SUBLANES), F32),
        jax.ShapeDtypeStruct((n_experts, LANES), I32),
    ]
    out_specs = [
        pl.BlockSpec((ts, d), lambda b, s: (b * n_s + s, 0)),
        pl.BlockSpec((ts, d // 2), lambda b, s: (b * n_s + s, 0)),
        pl.BlockSpec((SUBLANES, ts), lambda b, s: (0, b * n_s + s)),
        pl.BlockSpec((ts, SUBLANES), lambda b, s: (b * n_s + s, 0)),
        pl.BlockSpec((n_experts, LANES), lambda b, s: (0, 0)),
    ]
    return out_shape, out_specs


def _mixer_call(kern, name, prev, src, bsz, seq, ts, mixer_ops, mixer_specs, nffn, router, scratch):
    t, d = bsz * seq, src.rows.shape[1]
    n_s = seq // ts
    tok_ops, tok_specs = _mixer_in(prev, src, t, ts, n_s, d)
    r_ops, r_specs = _router_in(nffn, router, d, ts)
    out_shape, out_specs = _mixer_out(t, d, ts, n_s)
    return pl.pallas_call(
        functools.partial(kern, n_tok=len(tok_ops)),
        grid=(bsz, n_s),
        in_specs=tok_specs + mixer_specs + r_specs,
        out_specs=out_specs,
        out_shape=out_shape,
        scratch_shapes=scratch + [pltpu.VMEM((ROUTER_GROUPS[1], LANES), F32)],
        compiler_params=pltpu.CompilerParams(dimension_semantics=("arbitrary", "arbitrary"),
                                             vmem_limit_bytes=VMEM_LIMIT_BYTES),
        name=name,
    )(*tok_ops, *mixer_ops, *r_ops)


def _token_input(refs, n_tok):
    return refs[0][...] if n_tok == 1 else _combined(*refs[:n_tok])


def _pool_kernel(*refs, n_tok):
    x = _token_input(refs, n_tok)
    (nmix_ref, pw_ref, scale_ref,
     nffn_ref, wr_hi_ref, wr_lo_ref, br_ref, tri_ref,
     xm_ref, hp_ref, ids_ref, gate_ref, cnt_ref,
     carry_ref, base_ref) = refs[n_tok:]
    b = pl.program_id(0)
    s = pl.program_id(1)
    ts, d = x.shape
    ch = d // len(POOL_WINDOWS)

    @pl.when(s == 0)
    def _():
        carry_ref[...] = jnp.zeros_like(carry_ref)

    h = _rms(x, nmix_ref[...])
    ext = jnp.concatenate([carry_ref[...], h], axis=0)
    carry_ref[...] = h[ts - POOL_HALO:]

    pos = s * ts + lax.broadcasted_iota(I32, (ts, 1), 0) + 1
    ys = []
    for g, win in enumerate(POOL_WINDOWS):
        acc = ext[:, g * ch:(g + 1) * ch]
        k = 1
        while k < win:
            acc = acc + pltpu.roll(acc, k, axis=0)
            k *= 2
        inv_cnt = 1.0 / jnp.minimum(pos, win).astype(F32)
        pooled = acc[POOL_HALO:] * inv_cnt - h[:, g * ch:(g + 1) * ch]
        ys.append(_dot(pooled.astype(BF16), pw_ref[g]))
    y = jnp.concatenate(ys, axis=1) * scale_ref[...]
    xm = x + y
    xm_ref[...] = xm
    _route(xm, (b == 0) & (s == 0), nffn_ref, wr_hi_ref, wr_lo_ref, br_ref, tri_ref, base_ref,
           hp_ref, ids_ref, gate_ref, cnt_ref)


def _pool_layer(prev, src, bsz, seq, ts, nmix, pool_w, pool_scale, nffn, router):
    d = src.rows.shape[1]
    n_win, ch, _ = pool_w.shape
    assert n_win == len(POOL_WINDOWS) and ch * n_win == d
    ops = [nmix.reshape(1, d), pool_w.astype(BF16), pool_scale.reshape(1, d)]
    specs = [_const_spec((1, d)), _const_spec((n_win, ch, ch)), _const_spec((1, d))]
    return _mixer_call(_pool_kernel, "pool_mixer", prev, src, bsz, seq, ts, ops, specs, nffn, router,
                       [pltpu.VMEM((POOL_HALO, d), F32)])


def _sconv_kernel(*refs, n_tok):
    x = _token_input(refs, n_tok)
    (nmix_ref, win_ref, taps_ref, wout_ref,
     nffn_ref, wr_hi_ref, wr_lo_ref, br_ref, tri_ref,
     xm_ref, hp_ref, ids_ref, gate_ref, cnt_ref,
     carry_ref, base_ref) = refs[n_tok:]
    b = pl.program_id(0)
    s = pl.program_id(1)
    ts, d = x.shape

    @pl.when(s == 0)
    def _():
        carry_ref[...] = jnp.zeros_like(carry_ref)

    h = _rms(x, nmix_ref[...]).astype(BF16)
    b_gate = _dot(h, win_ref[:, 0:d])
    c_gate = _dot(h, win_ref[:, d:2 * d])
    v = _dot(h, win_ref[:, 2 * d:3 * d])
    u_pre = c_gate * v
    ext = jnp.concatenate([carry_ref[...], u_pre], axis=0)
    carry_ref[...] = u_pre[ts - CONV_HALO:]
    width = taps_ref.shape[0]
    u = taps_ref[width - 1:width, :] * u_pre
    for k in range(1, width):
        u = u + taps_ref[width - 1 - k:width - k, :] * _shift_rows(ext, k, CONV_HALO)
    mix = _dot((b_gate * u).astype(BF16), wout_ref[...])
    xm = x + mix
    xm_ref[...] = xm
    _route(xm, (b == 0) & (s == 0), nffn_ref, wr_hi_ref, wr_lo_ref, br_ref, tri_ref, base_ref,
           hp_ref, ids_ref, gate_ref, cnt_ref)


def _sconv_layer(prev, src, bsz, seq, ts, nmix, w_in, taps, w_out, nffn, router):
    d = src.rows.shape[1]
    width = taps.shape[0]
    assert width - 1 <= CONV_HALO
    ops = [nmix.reshape(1, d), w_in.astype(BF16), taps, w_out.astype(BF16)]
    specs = [_const_spec((1, d)), _const_spec((d, 3 * d)), _const_spec((width, d)), _const_spec((d, d))]
    return _mixer_call(_sconv_kernel, "sconv_mixer", prev, src, bsz, seq, ts, ops, specs, nffn, router,
                       [pltpu.VMEM((CONV_HALO, d), F32)])


def _ssm_kernel(*refs, n_tok):
    x = _token_input(refs, n_tok)
    (nmix_ref, wz_ref, wxbc_ref, wdt_hi_ref, wdt_lo_ref, convw_ref, convb_ref,
     dtb_ref, alog_ref, expand_ref, dskip_ref, normw_ref, wout_ref, triq_ref,
     nffn_ref, wr_hi_ref, wr_lo_ref, br_ref, tri_ref,
     xm_ref, hp_ref, ids_ref, gate_ref, cnt_ref,
     carry_ref, z_ref, xs_ref, bm_ref, cm_ref, y_ref, dtt_ref, at_ref, state_ref, base_ref) = refs[n_tok:]
    b = pl.program_id(0)
    s = pl.program_id(1)
    ts, d = x.shape
    di = z_ref.shape[1]
    gn = bm_ref.shape[1]
    cd = di + 2 * gn
    nh = alog_ref.shape[0]
    hp = di // nh
    grp = SSM_GROUPS
    st = gn // grp
    hpg = nh // grp
    gw = hpg * hp
    q = SSM_CHUNK
    n_chunks = ts // q

    @pl.when(s == 0)
    def _():
        carry_ref[...] = jnp.zeros_like(carry_ref)
        state_ref[...] = jnp.zeros_like(state_ref)

    h = _rms(x, nmix_ref[...])
    h_hi, h_lo = _hilo(h)

    z_ref[...] = _dot(h_hi, wz_ref[...])
    col_blk = gn
    for c0 in range(0, cd, col_blk):
        cols = slice(c0, c0 + col_blk)
        pre = _dot(h_hi, wxbc_ref[:, cols])
        ext = jnp.concatenate([carry_ref[:, cols], pre], axis=0)
        carry_ref[:, cols] = pre[ts - CONV_HALO:]
        ext1 = pltpu.roll(ext, 1, axis=0)
        pair = convw_ref[1:2, cols] * ext + convw_ref[0:1, cols] * ext1
        acc = (convw_ref[3:4, cols] * pre + convb_ref[:, cols] + convw_ref[2:3, cols] * ext1[CONV_HALO:]
               + _shift_rows(pair, 2, CONV_HALO))
        act = _silu(acc)
        if c0 < di:
            xs_ref[:, cols] = act
        elif c0 < di + gn:
            bm_ref[...] = act
        else:
            cm_ref[...] = act

    dt_t = _dot_nt_hilo(wdt_hi_ref[...], wdt_lo_ref[...], h_hi, h_lo) + dtb_ref[...]
    dt_t = jnp.maximum(dt_t, 0.0) + jnp.log(1.0 + jnp.exp(-jnp.abs(dt_t)))
    a_t = dt_t * (-jnp.exp(alog_ref[...]))
    for c in range(n_chunks):
        dtt_ref[c] = dt_t[:, c * q:(c + 1) * q]
        at_ref[c] = a_t[:, c * q:(c + 1) * q]

    row_q = lax.broadcasted_iota(I32, (q, q), 0)
    col_q = lax.broadcasted_iota(I32, (q, q), 1)
    causal = row_q >= col_q
    lane_lo = lax.broadcasted_iota(I32, (q, 2 * hp), 1) < hp
    neg_inf = jnp.float32(-jnp.inf)
    expand = expand_ref[...]

    def chunk_body(c, carry):
        r0 = pl.multiple_of(c * q, q)
        rows = pl.ds(r0, q)
        acs_t = _dot3(at_ref[c], triq_ref[...])
        dt_c = dtt_ref[c]
        w_t = dt_c * jnp.exp(acs_t[:, q - 1:q] - acs_t)
        src_t = acs_t - jnp.log(dt_c)
        acs = acs_t.T
        od_hi, od_lo = _hilo(jnp.exp(acs))
        out_decay_e = _dot(od_hi, expand) + _dot(od_lo, expand)
        w_e = _dot(w_t.T.astype(BF16), expand)
        xc = xs_ref[rows, :]
        xw = xc * w_e
        bc = bm_ref[rows, :]
        cc = cm_ref[rows, :]
        for g in range(grp):
            gcols = slice(g * gw, (g + 1) * gw)
            bg = bc[:, g * st:(g + 1) * st].astype(BF16)
            cg = cc[:, g * st:(g + 1) * st].astype(BF16)
            xg = xc[:, gcols]
            cb = _dot_nt(cg, bg)
            hprev = state_ref[g]
            y_off = _dot(cg, hprev.astype(BF16)) * out_decay_e[:, gcols]
            new_states = lax.dot_general(bg, xw[:, gcols].astype(BF16), TN_DIMS, preferred_element_type=F32)
            state_ref[g] = hprev * out_decay_e[q - 1:q, gcols] + new_states
            for j in range(hpg // 2):
                h0 = g * hpg + 2 * j
                ms = []
                for hh in (h0, h0 + 1):
                    seg = acs[:, hh:hh + 1] - src_t[hh:hh + 1, :]
                    ms.append((cb * jnp.exp(jnp.where(causal, seg, neg_inf))).astype(BF16))
                lhs = jnp.concatenate(ms, axis=1)
                xp = xg[:, 2 * j * hp:(2 * j + 2) * hp]
                rhs = jnp.concatenate([jnp.where(lane_lo, xp, 0.0), jnp.where(lane_lo, 0.0, xp)],
                                      axis=0).astype(BF16)
                pc = slice(g * gw + 2 * j * hp, g * gw + (2 * j + 2) * hp)
                y_ref[rows, pc] = _dot(lhs, rhs) + y_off[:, 2 * j * hp:(2 * j + 2) * hp]
        return carry

    lax.fori_loop(0, n_chunks, chunk_body, 0)

    out = None
    for g in range(grp):
        gcols = slice(g * gw, (g + 1) * gw)
        yg = (y_ref[:, gcols] + xs_ref[:, gcols] * dskip_ref[:, gcols]) * _silu(z_ref[:, gcols])
        yn = yg * lax.rsqrt(jnp.mean(yg * yg, axis=-1, keepdims=True) + NORM_EPS) * normw_ref[:, gcols]
        part = _dot(yn.astype(BF16), wout_ref[gcols, :])
        out = part if out is None else out + part
    xm = x + out
    xm_ref[...] = xm
    _route(xm, (b == 0) & (s == 0), nffn_ref, wr_hi_ref, wr_lo_ref, br_ref, tri_ref, base_ref,
           hp_ref, ids_ref, gate_ref, cnt_ref)


def _ssm_layer(prev, src, bsz, seq, ts, nmix, w_in, conv_w, conv_b, dt_bias, a_log, d_skip, norm_w, w_out,
               nffn, router):
    d = src.rows.shape[1]
    assert ts % SSM_CHUNK == 0
    nh = a_log.shape[0]
    di = norm_w.shape[0]
    cd = conv_w.shape[1]
    gn = (cd - di) // 2
    hp = di // nh
    width = conv_w.shape[0]
    assert width == SSM_CONV_WIDTH and (nh // SSM_GROUPS) % 2 == 0 and di % gn == 0
    st = gn // SSM_GROUPS
    gw = di // SSM_GROUPS
    q = SSM_CHUNK
    n_chunks = ts // q

    wz = w_in[:, :di].astype(BF16)
    wxbc = w_in[:, di:di + cd].astype(BF16)
    wdt_t = w_in[:, di + cd:].T
    wdt_hi = wdt_t.astype(BF16)
    wdt_lo = (wdt_t - wdt_hi.astype(F32)).astype(BF16)
    expand = (jnp.arange(di)[None, :] // hp == jnp.arange(nh)[:, None]).astype(BF16)
    dskip_e = jnp.repeat(d_skip, hp).reshape(1, di)
    idx = jnp.arange(q)
    triq = (idx[:, None] <= idx[None, :]).astype(BF16)

    ops = [nmix.reshape(1, d), wz, wxbc, wdt_hi, wdt_lo, conv_w, conv_b.reshape(1, cd),
           dt_bias.reshape(nh, 1), a_log.reshape(nh, 1), expand, dskip_e, norm_w.reshape(1, di),
           w_out.astype(BF16), triq]
    specs = [
        _const_spec((1, d)),
        _const_spec((d, di)),
        _const_spec((d, cd)),
        _const_spec((nh, d)),
        _const_spec((nh, d)),
        _const_spec((width, cd)),
        _const_spec((1, cd)),
        _const_spec((nh, 1)),
        _const_spec((nh, 1)),
        _const_spec((nh, di)),
        _const_spec((1, di)),
        _const_spec((1, di)),
        _const_spec((di, d)),
        _const_spec((q, q)),
    ]
    scratch = [
        pltpu.VMEM((CONV_HALO, cd), F32),
        pltpu.VMEM((ts, di), F32),
        pltpu.VMEM((ts, di), F32),
        pltpu.VMEM((ts, gn), F32),
        pltpu.VMEM((ts, gn), F32),
        pltpu.VMEM((ts, di), F32),
        pltpu.VMEM((n_chunks, nh, q), F32),
        pltpu.VMEM((n_chunks, nh, q), F32),
        pltpu.VMEM((SSM_GROUPS, st, gw), F32),
    ]
    return _mixer_call(_ssm_kernel, "ssm_mixer", prev, src, bsz, seq, ts, ops, specs, nffn, router, scratch)


def _sc_mesh():
    return plsc.VectorSubcoreMesh(core_axis_name="core", subcore_axis_name="subcore")


def _sc_worker_base(per_worker):
    return (lax.axis_index("subcore") * SC_CORES + lax.axis_index("core")) * per_worker


def _sc_scratch(d, dtype):
    win = SC_WINDOW
    return [pltpu.VMEM((win,), I32), pltpu.VMEM((win,), I32),
            pltpu.VMEM((win, d), dtype), pltpu.VMEM((win, d), dtype),
            pltpu.SemaphoreType.DMA, pltpu.SemaphoreType.DMA]


def _scatter_rows(src, idx, n_rows):
    n_src, d = src.shape
    m = idx.shape[0]
    win = SC_WINDOW
    per_worker = m // SC_WORKERS
    n_win = per_worker // win
    assert m == per_worker * SC_WORKERS and per_worker == n_win * win and n_win % 2 == 0
    assert n_src % per_worker == 0

    @functools.partial(pl.kernel, out_type=jax.ShapeDtypeStruct((n_rows, d), src.dtype), mesh=_sc_mesh(),
                       scratch_types=_sc_scratch(d, src.dtype), name="moe_scatter_rows")
    def scatter_kernel(x_hbm, i_hbm, o_hbm, idx_a, idx_b, rows_a, rows_b, sem_a, sem_b):
        base = _sc_worker_base(per_worker)
        src_base = lax.rem(base, n_src)

        def step(w, idx_v, rows_v, sem, first):
            if not first:
                pltpu.make_async_copy(rows_v, o_hbm.at[idx_v], sem).wait()
            pltpu.sync_copy(i_hbm.at[pl.ds(base + w * win, win)], idx_v)
            pltpu.sync_copy(x_hbm.at[pl.ds(src_base + w * win, win)], rows_v)
            pltpu.make_async_copy(rows_v, o_hbm.at[idx_v], sem).start()

        step(0, idx_a, rows_a, sem_a, True)
        step(1, idx_b, rows_b, sem_b, True)

        @pl.loop(2, n_win, step=2)
        def _(w):
            step(w, idx_a, rows_a, sem_a, False)
            step(w + 1, idx_b, rows_b, sem_b, False)

        pltpu.make_async_copy(rows_a, o_hbm.at[idx_a], sem_a).wait()
        pltpu.make_async_copy(rows_b, o_hbm.at[idx_b], sem_b).wait()

    return scatter_kernel(src, idx)


def _gather_rows(table, idx):
    d = table.shape[1]
    m = idx.shape[0]
    win = SC_WINDOW
    per_worker = m // SC_WORKERS
    n_win = per_worker // win
    assert m == per_worker * SC_WORKERS and per_worker == n_win * win and n_win % 2 == 0

    @functools.partial(pl.kernel, out_type=jax.ShapeDtypeStruct((m, d), table.dtype), mesh=_sc_mesh(),
                       scratch_types=_sc_scratch(d, table.dtype), name="moe_gather_rows")
    def gather_kernel(x_hbm, i_hbm, o_hbm, idx_a, idx_b, rows_a, rows_b, sem_a, sem_b):
        base = _sc_worker_base(per_worker)

        def out_copy(w, rows_v, sem):
            return pltpu.make_async_copy(rows_v, o_hbm.at[pl.ds(base + w * win, win)], sem)

        def step(w, idx_v, rows_v, sem, first):
            pltpu.sync_copy(i_hbm.at[pl.ds(base + w * win, win)], idx_v)
            if not first:
                out_copy(w, rows_v, sem).wait()
            pltpu.sync_copy(x_hbm.at[idx_v], rows_v)
            out_copy(w, rows_v, sem).start()

        step(0, idx_a, rows_a, sem_a, True)
        step(1, idx_b, rows_b, sem_b, True)

        @pl.loop(2, n_win, step=2)
        def _(w):
            step(w, idx_a, rows_a, sem_a, False)
            step(w + 1, idx_b, rows_b, sem_b, False)

        out_copy(0, rows_a, sem_a).wait()
        out_copy(0, rows_b, sem_b).wait()

    return gather_kernel(table, idx)


def _ffn_kernel(be_ref, nu_ref, nv_ref, slot_ref, nxt_ref, xs_ref, wgu_hbm, wdn_hbm, ys_ref,
                wgu_f32_ref, wdn_f32_ref, wgu_bf_ref, wdn_bf_ref, sem_ref, *, layer):
    i = pl.program_id(0)
    bm = xs_ref.shape[0]
    f = wdn_bf_ref.shape[0]

    def weight_copies(expert, slot):
        return (pltpu.make_async_copy(wgu_hbm.at[layer, expert], wgu_f32_ref.at[slot], sem_ref.at[slot, 0]),
                pltpu.make_async_copy(wdn_hbm.at[layer, expert], wdn_f32_ref.at[slot], sem_ref.at[slot, 1]))

    @pl.when(i == 0)
    def _():
        for cp in weight_copies(be_ref[0], slot_ref[0]):
            cp.start()

    first_of_expert = (i == 0) | (be_ref[i] != be_ref[jnp.maximum(i - 1, 0)])

    @pl.when(first_of_expert & (i < nu_ref[0]))
    def _():
        slot = slot_ref[i]
        for cp in weight_copies(be_ref[i], slot):
            cp.wait()

        @pl.when(nxt_ref[i] >= 0)
        def _():
            for cp in weight_copies(nxt_ref[i], 1 - slot):
                cp.start()

        wgu_bf_ref[...] = wgu_f32_ref[slot].astype(BF16)
        wdn_bf_ref[...] = wdn_f32_ref[slot].astype(BF16)

    @pl.when(i < nu_ref[0])
    def _():
        valid = lax.broadcasted_iota(I32, (bm, 1), 0) < nv_ref[i]
        h = _unpack_bf16(jnp.where(valid, xs_ref[...], 0)).astype(BF16)
        gu = _dot(h, wgu_bf_ref[...])
        act = _silu(gu[:, :f]) * gu[:, f:]
        ys_ref[...] = _pack_bf16(_dot(act.astype(BF16), wdn_bf_ref[...]).astype(BF16))

    @pl.when(i >= nu_ref[0])
    def _():
        ys_ref[...] = jnp.zeros_like(ys_ref)


def _block_schedule(cnt, n_blocks, bm):
    n_exp = cnt.shape[0]
    padded = (cnt + bm - 1) // bm * bm
    e_ids = jnp.arange(n_exp, dtype=I32)
    before = e_ids[None, :] < e_ids[:, None]
    pad_start = jnp.sum(jnp.where(before, padded[None, :], 0), axis=1).astype(I32)
    pad_end = pad_start + padded
    nonempty = cnt > 0
    seg_of = jnp.sum(before & nonempty[None, :], axis=1)
    after = (e_ids[None, :] > e_ids[:, None]) & nonempty[None, :]
    next_of = jnp.min(jnp.where(after, e_ids[None, :], n_exp), axis=1)
    next_of = jnp.where(next_of == n_exp, -1, next_of)
    blk_row = jnp.arange(n_blocks, dtype=I32) * bm
    block_e = jnp.minimum(jnp.sum(blk_row[:, None] >= pad_end[None, :], axis=1), n_exp - 1).astype(I32)
    of_block = block_e[:, None] == e_ids[None, :]

    def per_block(v):
        return jnp.sum(jnp.where(of_block, v[None, :], 0), axis=1).astype(I32)

    n_used = (jnp.sum(padded, keepdims=True) // bm).astype(I32)
    n_valid = jnp.clip(per_block(pad_start + cnt) - blk_row, 0, bm).astype(I32)
    return pad_start, (block_e, n_used, n_valid, per_block(seg_of % 2), per_block(next_of))


def _expert_ffn(xs, schedule, w_gu_all, w_dn_all, layer):
    n_rows, dh = xs.shape
    bm = FFN_BLOCK
    n_blocks = n_rows // bm
    d, f = w_gu_all.shape[2], w_dn_all.shape[2]
    assert d == 2 * dh

    def xs_map(i, be, nu, nv, sl, nx):
        return (jnp.minimum(i, jnp.maximum(nu[0] - 1, 0)), 0)

    return pl.pallas_call(
        functools.partial(_ffn_kernel, layer=layer),
        grid_spec=pltpu.PrefetchScalarGridSpec(
            num_scalar_prefetch=5,
            grid=(n_blocks,),
            in_specs=[
                pl.BlockSpec((bm, dh), xs_map),
                pl.BlockSpec(memory_space=pl.ANY),
                pl.BlockSpec(memory_space=pl.ANY),
            ],
            out_specs=pl.BlockSpec((bm, dh), lambda i, be, nu, nv, sl, nx: (i, 0)),
            scratch_shapes=[
                pltpu.VMEM((2, d, 2 * f), F32),
                pltpu.VMEM((2, f, d), F32),
                pltpu.VMEM((d, 2 * f), BF16),
                pltpu.VMEM((f, d), BF16),
                pltpu.SemaphoreType.DMA((2, 2)),
            ],
        ),
        out_shape=jax.ShapeDtypeStruct((n_rows, dh), I32),
        compiler_params=pltpu.CompilerParams(dimension_semantics=("arbitrary",),
                                             vmem_limit_bytes=VMEM_LIMIT_BYTES),
        name="expert_ffn",
    )(*schedule, xs, w_gu_all, w_dn_all)


def _dest_kernel(pad_start_ref, ids_ref, dest_ref):
    expert = ids_ref[0:MOE_TOP_K, :]
    dest = ids_ref[MOE_TOP_K:2 * MOE_TOP_K, :]
    for e in range(pad_start_ref.shape[0]):
        dest = dest + jnp.where(expert == e, pad_start_ref[e], 0)
    dest_ref[...] = dest


def _assignment_rows(ids, pad_start):
    t = ids.shape[1]
    tn = min(TS_DEST, t)
    assert t % tn == 0
    return pl.pallas_call(
        _dest_kernel,
        grid_spec=pltpu.PrefetchScalarGridSpec(
            num_scalar_prefetch=1,
            grid=(t // tn,),
            in_specs=[pl.BlockSpec((SUBLANES, tn), lambda i, ps: (0, i))],
            out_specs=pl.BlockSpec((MOE_TOP_K, tn), lambda i, ps: (0, i)),
        ),
        out_shape=jax.ShapeDtypeStruct((MOE_TOP_K, t), I32),
        compiler_params=pltpu.CompilerParams(dimension_semantics=("arbitrary",)),
        name="moe_dest",
    )(pad_start, ids)


def _moe(hp, ids, gate, counts, w_gu_all, w_dn_all, layer):
    t = hp.shape[0]
    n_exp = w_gu_all.shape[1]
    bm = FFN_BLOCK
    n_assign = t * MOE_TOP_K
    n_rows = (n_assign + bm - 1) // bm * bm + n_exp * bm
    n_blocks = n_rows // bm

    pad_start, schedule = _block_schedule(counts[:, 0], n_blocks, bm)
    dest = _assignment_rows(ids, pad_start).reshape(-1)
    xs = _scatter_rows(hp, dest, n_rows)
    ys = _expert_ffn(xs, schedule, w_gu_all, w_dn_all, layer)
    return _gather_rows(ys, dest), gate


def _final_kernel(xm_ref, y0_ref, y1_ref, g_ref, nfin_ref, *out_refs):
    out_refs[-1][...] = _rms(_combined(xm_ref, y0_ref, y1_ref, g_ref), nfin_ref[...])


def _final(xm, yg, gates, norm_final, out, group, n_groups):
    t, d = xm.shape
    ts = min(TS_FINAL, t)
    n_tiles = t // ts
    tile = pl.BlockSpec((ts, d), lambda i: (i, 0))
    in_specs = [tile, pl.BlockSpec((ts, d // 2), lambda i: (i, 0)),
                pl.BlockSpec((ts, d // 2), lambda i: (n_tiles + i, 0)),
                pl.BlockSpec((ts, SUBLANES), lambda i: (i, 0)), pl.BlockSpec((1, d), lambda i: (0, 0))]
    operands = [xm, yg, yg, gates, norm_final.reshape(1, d)]
    aliases = {}
    if out is not None:
        in_specs.append(pl.BlockSpec(memory_space=pl.ANY))
        operands.append(out)
        aliases = {len(operands) - 1: 0}
    return pl.pallas_call(
        _final_kernel,
        grid=(n_tiles,),
        in_specs=in_specs,
        out_specs=pl.BlockSpec((ts, d), lambda i: (group * n_tiles + i, 0)),
        out_shape=jax.ShapeDtypeStruct((n_groups * t, d), F32),
        input_output_aliases=aliases,
        compiler_params=pltpu.CompilerParams(dimension_semantics=("arbitrary",),
                                             vmem_limit_bytes=VMEM_LIMIT_BYTES),
        name="final_combine_norm",
    )(*operands)


def kernel(x, norm_mix, norm_ffn, norm_final, pool_w, pool_scale, sconv_in_w, sconv_taps, sconv_out_w, ssm_in_w, ssm_conv_w, ssm_conv_b, ssm_dt_bias, ssm_a_log, ssm_d, ssm_norm_w, ssm_out_w, router_group_w, router_group_b, router_expert_w, router_expert_b, expert_w_gu, expert_w_down):
    bsz, seq, d = x.shape
    depth = norm_mix.shape[0]
    assert (router_group_w.shape[2], router_expert_w.shape[2]) == ROUTER_GROUPS
    n_groups = TOKEN_GROUPS if bsz % TOKEN_GROUPS == 0 else 1
    gb = bsz // n_groups
    x2d = x.reshape(bsz * seq, d)
    src = [None] * n_groups
    prev = [None] * n_groups
    for i in range(depth):
        kind, j = i % N_MIXERS, i // N_MIXERS
        ts = min((TS_POOL, TS_SCONV, TS_SSM)[kind], seq)
        router = _router_operands(router_group_w[i], router_group_b[i], router_expert_w[i],
                                  router_expert_b[i], ts)
        for g in range(n_groups):
            s_in = Stream(x2d, g * gb * (seq // ts)) if i == 0 else src[g]
            if kind == 0:
                outs = _pool_layer(prev[g], s_in, gb, seq, ts, norm_mix[i], pool_w[j], pool_scale[j],
                                   norm_ffn[i], router)
            elif kind == 1:
                outs = _sconv_layer(prev[g], s_in, gb, seq, ts, norm_mix[i], sconv_in_w[j], sconv_taps[j],
                                    sconv_out_w[j], norm_ffn[i], router)
            else:
                outs = _ssm_layer(prev[g], s_in, gb, seq, ts, norm_mix[i], ssm_in_w[j], ssm_conv_w[j],
                                  ssm_conv_b[j], ssm_dt_bias[j], ssm_a_log[j], ssm_d[j], ssm_norm_w[j],
                                  ssm_out_w[j], norm_ffn[i], router)
            xm, hp, ids, gate, counts = outs
            src[g] = Stream(xm, 0)
            prev[g] = _moe(hp, ids, gate, counts, expert_w_gu, expert_w_down, i)
    out = None
    for g in range(n_groups):
        out = _final(src[g].rows, prev[g][0], prev[g][1], norm_final, out, g, n_groups)
    return out.reshape(bsz, seq, d)
```

```python
import functools
from typing import NamedTuple

import jax
import jax.numpy as jnp
from jax import lax
from jax.experimental import pallas as pl
from jax.experimental.pallas import tpu as pltpu
from jax.experimental.pallas import tpu_sc as plsc

F32 = jnp.float32
BF16 = jnp.bfloat16
I32 = jnp.int32

NORM_EPS = 1e-6
N_MIXERS = 3
POOL_WINDOWS = (2, 4, 8, 16)
POOL_HALO = 16
CONV_HALO = 8
SSM_GROUPS = 4
SSM_CHUNK = 128
SSM_CONV_WIDTH = 4
MOE_TOP_K = 2
TOKEN_GROUPS = 2
EXPERTS_PER_GROUP = 8
ROUTER_GROUPS = (4, 32)
ROUTER_ROWS = 48

SUBLANES = 8
LANES = 128
VMEM_LIMIT_BYTES = 56 * 1024 * 1024
SC_CORES = 2
SC_WORKERS = SC_CORES * 16
SC_WINDOW = 64

TS_POOL = 512
TS_SCONV = 512
TS_SSM = 256
TS_FINAL = 512
TS_DEST = 4096
FFN_BLOCK = 512

NT_DIMS = (((1,), (1,)), ((), ()))
TN_DIMS = (((0,), (0,)), ((), ()))


def _dot(a, b):
    return jnp.dot(a, b, preferred_element_type=F32)


def _dot_nt(a, b):
    return lax.dot_general(a, b, NT_DIMS, preferred_element_type=F32)


def _split3(a):
    a1 = a.astype(BF16)
    r1 = a - a1.astype(F32)
    a2 = r1.astype(BF16)
    a3 = (r1 - a2.astype(F32)).astype(BF16)
    return a1, a2, a3


def _dot3(a, b):
    a1, a2, a3 = _split3(a)
    return _dot(a1, b) + _dot(a2, b) + _dot(a3, b)


def _hilo(a):
    hi = a.astype(BF16)
    lo = (a - hi.astype(F32)).astype(BF16)
    return hi, lo


def _dot_nt_hilo(w_hi, w_lo, a_hi, a_lo):
    return _dot_nt(w_hi, a_hi) + _dot_nt(w_hi, a_lo) + _dot_nt(w_lo, a_hi)


def _rms(x, w):
    return x * lax.rsqrt(jnp.mean(x * x, axis=-1, keepdims=True) + NORM_EPS) * w


def _silu(x):
    half = 0.5 * x
    return half + half * jnp.tanh(half)


def _shift_rows(ext, k, halo):
    return pltpu.roll(ext, k, axis=0)[halo:]


HI_HALF_MASK = -65536


def _pack_bf16(a_bf):
    half = a_bf.shape[1] // 2
    lo = lax.bitcast_convert_type(a_bf[:, :half].astype(F32), I32)
    hi = lax.bitcast_convert_type(a_bf[:, half:].astype(F32), I32)
    return lax.shift_right_logical(lo, 16) | (hi & HI_HALF_MASK)


def _unpack_bf16(w):
    lo = lax.bitcast_convert_type(lax.shift_left(w, 16), F32)
    hi = lax.bitcast_convert_type(w & HI_HALF_MASK, F32)
    return jnp.concatenate([lo, hi], axis=1)


def _combined(xm_ref, y0_ref, y1_ref, g_ref):
    g = g_ref[...]
    return xm_ref[...] + g[:, 0:1] * _unpack_bf16(y0_ref[...]) + g[:, 1:2] * _unpack_bf16(y1_ref[...])


def _route(xm, first, nffn_ref, wr_hi_ref, wr_lo_ref, br_ref, tri_ref, base_ref, hp_ref, ids_ref, gate_ref,
           cnt_ref):
    n_groups, n_experts = ROUTER_GROUPS
    ts = xm.shape[0]

    @pl.when(first)
    def _():
        base_ref[...] = jnp.zeros_like(base_ref)

    h2 = _rms(xm, nffn_ref[...])
    h_hi, h_lo = _hilo(h2)
    hp_ref[...] = _pack_bf16(h_hi)
    lt = _dot_nt_hilo(wr_hi_ref[...], wr_lo_ref[...], h_hi, h_lo) + br_ref[...]

    row8 = lax.broadcasted_iota(I32, (SUBLANES, ts), 0)
    neg_inf = jnp.float32(-jnp.inf)
    g = jnp.where(row8 < n_groups, lt[0:SUBLANES], neg_inf)
    gmax = jnp.max(g, axis=0, keepdims=True)
    gsel = jnp.min(jnp.where(g == gmax, row8, SUBLANES), axis=0, keepdims=True)
    pgrp = 1.0 / jnp.sum(jnp.exp(g - gmax), axis=0, keepdims=True)

    sel = lt[SUBLANES:2 * SUBLANES]
    for j in range(1, n_groups):
        sel = jnp.where(gsel == j, lt[(j + 1) * SUBLANES:(j + 2) * SUBLANES], sel)
    m1 = jnp.max(sel, axis=0, keepdims=True)
    i1 = jnp.min(jnp.where(sel == m1, row8, SUBLANES), axis=0, keepdims=True)
    sel2 = jnp.where(row8 == i1, neg_inf, sel)
    m2 = jnp.max(sel2, axis=0, keepdims=True)
    i2 = jnp.min(jnp.where(sel2 == m2, row8, SUBLANES), axis=0, keepdims=True)
    r = jnp.exp(m2 - m1)
    den = 1.0 + r
    g0 = pgrp / den
    g1 = pgrp * r / den
    e0 = gsel * EXPERTS_PER_GROUP + i1
    e1 = gsel * EXPERTS_PER_GROUP + i2

    row_e = lax.broadcasted_iota(I32, (n_experts, ts), 0)
    oh0 = row_e == e0
    oh1 = row_e == e1
    oh = jnp.where(oh0 | oh1, 1.0, 0.0).astype(F32)
    prefix = _dot(oh.astype(BF16), tri_ref[...])
    tot = prefix + base_ref[:, 0:1]
    rank0 = jnp.sum(jnp.where(oh0, tot, 0.0), axis=0, keepdims=True).astype(I32)
    rank1 = jnp.sum(jnp.where(oh1, tot, 0.0), axis=0, keepdims=True).astype(I32)
    base_ref[...] = base_ref[...] + jnp.sum(oh, axis=1, keepdims=True)

    ids_ref[...] = jnp.where(row8 == 0, e0, jnp.where(row8 == 1, e1, jnp.where(row8 == 2, rank0, rank1)))
    eye = (lax.broadcasted_iota(I32, (SUBLANES, SUBLANES), 0)
           == lax.broadcasted_iota(I32, (SUBLANES, SUBLANES), 1)).astype(BF16)
    gate_ref[...] = sum(lax.dot_general(part, eye, TN_DIMS, preferred_element_type=F32)
                        for part in _split3(jnp.where(row8 == 0, g0, g1)))
    cnt_ref[...] = base_ref[...].astype(I32)


def _router_operands(w_rg, b_rg, w_re, b_re, ts):
    d, n_groups = w_rg.shape
    n_experts = w_re.shape[1]
    assert n_experts == n_groups * EXPERTS_PER_GROUP and n_groups <= SUBLANES
    assert SUBLANES + n_experts <= ROUTER_ROWS
    wt = jnp.zeros((ROUTER_ROWS, d), F32)
    wt = wt.at[0:n_groups].set(w_rg.T).at[SUBLANES:SUBLANES + n_experts].set(w_re.T)
    bt = jnp.zeros((ROUTER_ROWS, 1), F32)
    bt = bt.at[0:n_groups, 0].set(b_rg).at[SUBLANES:SUBLANES + n_experts, 0].set(b_re)
    w_hi = wt.astype(BF16)
    w_lo = (wt - w_hi.astype(F32)).astype(BF16)
    idx = jnp.arange(ts)
    tri = (idx[:, None] < idx[None, :]).astype(BF16)
    return w_hi, w_lo, bt, tri


def _const_spec(shape):
    zeros = (0,) * len(shape)
    return pl.BlockSpec(shape, lambda b, s: zeros, pipeline_mode=pl.Buffered(1))


class Stream(NamedTuple):
    rows: jax.Array
    tile0: int


def _mixer_in(prev, src, t, ts, n_s, d):
    n_tiles = t // ts
    tile0 = src.tile0
    tile = pl.BlockSpec((ts, d), lambda b, s: (tile0 + b * n_s + s, 0))
    if prev is None:
        return [src.rows], [tile]
    yg, gates = prev
    specs = [
        tile,
        pl.BlockSpec((ts, d // 2), lambda b, s: (b * n_s + s, 0)),
        pl.BlockSpec((ts, d // 2), lambda b, s: (n_tiles + b * n_s + s, 0)),
        pl.BlockSpec((ts, SUBLANES), lambda b, s: (b * n_s + s, 0)),
    ]
    return [src.rows, yg, yg, gates], specs


def _router_in(nffn, router, d, ts):
    w_hi, w_lo, bt, tri = router
    specs = [
        _const_spec((1, d)),
        _const_spec((ROUTER_ROWS, d)),
        _const_spec((ROUTER_ROWS, d)),
        _const_spec((ROUTER_ROWS, 1)),
        _const_spec((ts, ts)),
    ]
    return [nffn.reshape(1, d), w_hi, w_lo, bt, tri], specs


def _mixer_out(t, d, ts, n_s):
    n_tiles = t // ts
    n_experts = ROUTER_GROUPS[1]
    out_shape = [
        jax.ShapeDtypeStruct((t, d), F32),
        jax.ShapeDtypeStruct((t, d // 2), I32),
        jax.ShapeDtypeStruct((SUBLANES, t), I32),
        jax.ShapeDtypeStruct((t, SUBLANES), F32),
        jax.ShapeDtypeStruct((n_experts, LANES), I32),
    ]
    out_specs = [
        pl.BlockSpec((ts, d), lambda b, s: (b * n_s + s, 0)),
        pl.BlockSpec((ts, d // 2), lambda b, s: (b * n_s + s, 0)),
        pl.BlockSpec((SUBLANES, ts), lambda b, s: (0, b * n_s + s)),
        pl.BlockSpec((ts, SUBLANES), lambda b, s: (b * n_s + s, 0)),
        pl.BlockSpec((n_experts, LANES), lambda b, s: (0, 0)),
    ]
    return out_shape, out_specs


def _mixer_call(kern, name, prev, src, bsz, seq, ts, mixer_ops, mixer_specs, nffn, router, scratch):
    t, d = bsz * seq, src.rows.shape[1]
    n_s = seq // ts
    tok_ops, tok_specs = _mixer_in(prev, src, t, ts, n_s, d)
    r_ops, r_specs = _router_in(nffn, router, d, ts)
    out_shape, out_specs = _mixer_out(t, d, ts, n_s)
    return pl.pallas_call(
        functools.partial(kern, n_tok=len(tok_ops)),
        grid=(bsz, n_s),
        in_specs=tok_specs + mixer_specs + r_specs,
        out_specs=out_specs,
        out_shape=out_shape,
        scratch_shapes=scratch + [pltpu.VMEM((ROUTER_GROUPS[1], LANES), F32)],
        compiler_params=pltpu.CompilerParams(dimension_semantics=("arbitrary", "arbitrary"),
                                             vmem_limit_bytes=VMEM_LIMIT_BYTES),
        name=name,
    )(*tok_ops, *mixer_ops, *r_ops)


def _token_input(refs, n_tok):
    return refs[0][...] if n_tok == 1 else _combined(*refs[:n_tok])


def _pool_kernel(*refs, n_tok):
    x = _token_input(refs, n_tok)
    (nmix_ref, pw_ref, scale_ref,
     nffn_ref, wr_hi_ref, wr_lo_ref, br_ref, tri_ref,
     xm_ref, hp_ref, ids_ref, gate_ref, cnt_ref,
     carry_ref, base_ref) = refs[n_tok:]
    b = pl.program_id(0)
    s = pl.program_id(1)
    ts, d = x.shape
    ch = d // len(POOL_WINDOWS)

    @pl.when(s == 0)
    def _():
        carry_ref[...] = jnp.zeros_like(carry_ref)

    h = _rms(x, nmix_ref[...])
    ext = jnp.concatenate([carry_ref[...], h], axis=0)
    carry_ref[...] = h[ts - POOL_HALO:]

    pos = s * ts + lax.broadcasted_iota(I32, (ts, 1), 0) + 1
    ys = []
    for g, win in enumerate(POOL_WINDOWS):
        acc = ext[:, g * ch:(g + 1) * ch]
        k = 1
        while k < win:
            acc = acc + pltpu.roll(acc, k, axis=0)
            k *= 2
        inv_cnt = 1.0 / jnp.minimum(pos, win).astype(F32)
        pooled = acc[POOL_HALO:] * inv_cnt - h[:, g * ch:(g + 1) * ch]
        ys.append(_dot(pooled.astype(BF16), pw_ref[g]))
    y = jnp.concatenate(ys, axis=1) * scale_ref[...]
    xm = x + y
    xm_ref[...] = xm
    _route(xm, (b == 0) & (s == 0), nffn_ref, wr_hi_ref, wr_lo_ref, br_ref, tri_ref, base_ref,
           hp_ref, ids_ref, gate_ref, cnt_ref)


def _pool_layer(prev, src, bsz, seq, ts, nmix, pool_w, pool_scale, nffn, router):
    d = src.rows.shape[1]
    n_win, ch, _ = pool_w.shape
    assert n_win == len(POOL_WINDOWS) and ch * n_win == d
    ops = [nmix.reshape(1, d), pool_w.astype(BF16), pool_scale.reshape(1, d)]
    specs = [_const_spec((1, d)), _const_spec((n_win, ch, ch)), _const_spec((1, d))]
    return _mixer_call(_pool_kernel, "pool_mixer", prev, src, bsz, seq, ts, ops, specs, nffn, router,
                       [pltpu.VMEM((POOL_HALO, d), F32)])


def _sconv_kernel(*refs, n_tok):
    x = _token_input(refs, n_tok)
    (nmix_ref, win_ref, taps_ref, wout_ref,
     nffn_ref, wr_hi_ref, wr_lo_ref, br_ref, tri_ref,
     xm_ref, hp_ref, ids_ref, gate_ref, cnt_ref,
     carry_ref, base_ref) = refs[n_tok:]
    b = pl.program_id(0)
    s = pl.program_id(1)
    ts, d = x.shape

    @pl.when(s == 0)
    def _():
        carry_ref[...] = jnp.zeros_like(carry_ref)

    h = _rms(x, nmix_ref[...]).astype(BF16)
    b_gate = _dot(h, win_ref[:, 0:d])
    c_gate = _dot(h, win_ref[:, d:2 * d])
    v = _dot(h, win_ref[:, 2 * d:3 * d])
    u_pre = c_gate * v
    ext = jnp.concatenate([carry_ref[...], u_pre], axis=0)
    carry_ref[...] = u_pre[ts - CONV_HALO:]
    width = taps_ref.shape[0]
    u = taps_ref[width - 1:width, :] * u_pre
    for k in range(1, width):
        u = u + taps_ref[width - 1 - k:width - k, :] * _shift_rows(ext, k, CONV_HALO)
    mix = _dot((b_gate * u).astype(BF16), wout_ref[...])
    xm = x + mix
    xm_ref[...] = xm
    _route(xm, (b == 0) & (s == 0), nffn_ref, wr_hi_ref, wr_lo_ref, br_ref, tri_ref, base_ref,
           hp_ref, ids_ref, gate_ref, cnt_ref)


def _sconv_layer(prev, src, bsz, seq, ts, nmix, w_in, taps, w_out, nffn, router):
    d = src.rows.shape[1]
    width = taps.shape[0]
    assert width - 1 <= CONV_HALO
    ops = [nmix.reshape(1, d), w_in.astype(BF16), taps, w_out.astype(BF16)]
    specs = [_const_spec((1, d)), _const_spec((d, 3 * d)), _const_spec((width, d)), _const_spec((d, d))]
    return _mixer_call(_sconv_kernel, "sconv_mixer", prev, src, bsz, seq, ts, ops, specs, nffn, router,
                       [pltpu.VMEM((CONV_HALO, d), F32)])


def _ssm_kernel(*refs, n_tok):
    x = _token_input(refs, n_tok)
    (nmix_ref, wz_ref, wxbc_ref, wdt_hi_ref, wdt_lo_ref, convw_ref, convb_ref,
     dtb_ref, alog_ref, expand_ref, dskip_ref, normw_ref, wout_ref, triq_ref,
     nffn_ref, wr_hi_ref, wr_lo_ref, br_ref, tri_ref,
     xm_ref, hp_ref, ids_ref, gate_ref, cnt_ref,
     carry_ref, z_ref, xs_ref, bm_ref, cm_ref, y_ref, dtt_ref, at_ref, state_ref, base_ref) = refs[n_tok:]
    b = pl.program_id(0)
    s = pl.program_id(1)
    ts, d = x.shape
    di = z_ref.shape[1]
    gn = bm_ref.shape[1]
    cd = di + 2 * gn
    nh = alog_ref.shape[0]
    hp = di // nh
    grp = SSM_GROUPS
    st = gn // grp
    hpg = nh // grp
    gw = hpg * hp
    q = SSM_CHUNK
    n_chunks = ts // q

    @pl.when(s == 0)
    def _():
        carry_ref[...] = jnp.zeros_like(carry_ref)
        state_ref[...] = jnp.zeros_like(state_ref)

    h = _rms(x, nmix_ref[...])
    h_hi, h_lo = _hilo(h)

    col_blk = gn

    def project(c0):
        cols = slice(c0, c0 + col_blk)
        if c0 < di:
            z_ref[:, cols] = _dot(h_hi, wz_ref[:, cols])
        return _dot(h_hi, wxbc_ref[:, cols])

    pre_next = project(0)
    for c0 in range(0, cd, col_blk):
        cols = slice(c0, c0 + col_blk)
        pre = pre_next
        if c0 + col_blk < cd:
            pre_next = project(c0 + col_blk)
        ext = jnp.concatenate([carry_ref[:, cols], pre], axis=0)
        carry_ref[:, cols] = pre[ts - CONV_HALO:]
        ext1 = pltpu.roll(ext, 1, axis=0)
        pair = convw_ref[1:2, cols] * ext + convw_ref[0:1, cols] * ext1
        acc = (convw_ref[3:4, cols] * pre + convb_ref[:, cols] + convw_ref[2:3, cols] * ext1[CONV_HALO:]
               + _shift_rows(pair, 2, CONV_HALO))
        act = _silu(acc)
        if c0 < di:
            xs_ref[:, cols] = act
        elif c0 < di + gn:
            bm_ref[...] = act
        else:
            cm_ref[...] = act

    dt_t = _dot_nt_hilo(wdt_hi_ref[...], wdt_lo_ref[...], h_hi, h_lo) + dtb_ref[...]
    dt_t = jnp.maximum(dt_t, 0.0) + jnp.log(1.0 + jnp.exp(-jnp.abs(dt_t)))
    a_t = dt_t * (-jnp.exp(alog_ref[...]))
    for c in range(n_chunks):
        dtt_ref[c] = dt_t[:, c * q:(c + 1) * q]
        at_ref[c] = a_t[:, c * q:(c + 1) * q]

    row_q = lax.broadcasted_iota(I32, (q, q), 0)
    col_q = lax.broadcasted_iota(I32, (q, q), 1)
    causal = row_q >= col_q
    lane_lo = lax.broadcasted_iota(I32, (q, 2 * hp), 1) < hp
    neg_inf = jnp.float32(-jnp.inf)
    expand = expand_ref[...]

    def chunk_body(c, carry):
        r0 = pl.multiple_of(c * q, q)
        rows = pl.ds(r0, q)
        acs_t = _dot3(at_ref[c], triq_ref[...])
        dt_c = dtt_ref[c]
        w_t = dt_c * jnp.exp(acs_t[:, q - 1:q] - acs_t)
        src_t = acs_t - jnp.log(dt_c)
        acs = acs_t.T
        od_hi, od_lo = _hilo(jnp.exp(acs))
        out_decay_e = _dot(od_hi, expand) + _dot(od_lo, expand)
        w_e = _dot(w_t.T.astype(BF16), expand)
        xc = xs_ref[rows, :]
        xw = xc * w_e
        bc = bm_ref[rows, :]
        cc = cm_ref[rows, :]
        for g in range(grp):
            gcols = slice(g * gw, (g + 1) * gw)
            bg = bc[:, g * st:(g + 1) * st].astype(BF16)
            cg = cc[:, g * st:(g + 1) * st].astype(BF16)
            xg = xc[:, gcols]
            cb = _dot_nt(cg, bg)
            hprev = state_ref[g]
            y_off = _dot(cg, hprev.astype(BF16)) * out_decay_e[:, gcols]
            new_states = lax.dot_general(bg, xw[:, gcols].astype(BF16), TN_DIMS, preferred_element_type=F32)
            state_ref[g] = hprev * out_decay_e[q - 1:q, gcols] + new_states
            for j in range(hpg // 2):
                h0 = g * hpg + 2 * j
                ms = []
                for hh in (h0, h0 + 1):
                    seg = acs[:, hh:hh + 1] - src_t[hh:hh + 1, :]
                    ms.append((cb * jnp.exp(jnp.where(causal, seg, neg_inf))).astype(BF16))
                lhs = jnp.concatenate(ms, axis=1)
                xp = xg[:, 2 * j * hp:(2 * j + 2) * hp]
                rhs = jnp.concatenate([jnp.where(lane_lo, xp, 0.0), jnp.where(lane_lo, 0.0, xp)],
                                      axis=0).astype(BF16)
                pc = slice(g * gw + 2 * j * hp, g * gw + (2 * j + 2) * hp)
                y_ref[rows, pc] = _dot(lhs, rhs) + y_off[:, 2 * j * hp:(2 * j + 2) * hp]
        return carry

    lax.fori_loop(0, n_chunks, chunk_body, 0)

    out = None
    for g in range(grp):
        gcols = slice(g * gw, (g + 1) * gw)
        yg = (y_ref[:, gcols] + xs_ref[:, gcols] * dskip_ref[:, gcols]) * _silu(z_ref[:, gcols])
        yn = yg * lax.rsqrt(jnp.mean(yg * yg, axis=-1, keepdims=True) + NORM_EPS) * normw_ref[:, gcols]
        part = _dot(yn.astype(BF16), wout_ref[gcols, :])
        out = part if out is None else out + part
    xm = x + out
    xm_ref[...] = xm
    _route(xm, (b == 0) & (s == 0), nffn_ref, wr_hi_ref, wr_lo_ref, br_ref, tri_ref, base_ref,
           hp_ref, ids_ref, gate_ref, cnt_ref)


def _ssm_layer(prev, src, bsz, seq, ts, nmix, w_in, conv_w, conv_b, dt_bias, a_log, d_skip, norm_w, w_out,
               nffn, router):
    d = src.rows.shape[1]
    assert ts % SSM_CHUNK == 0
    nh = a_log.shape[0]
    di = norm_w.shape[0]
    cd = conv_w.shape[1]
    gn = (cd - di) // 2
    hp = di // nh
    width = conv_w.shape[0]
    assert width == SSM_CONV_WIDTH and (nh // SSM_GROUPS) % 2 == 0 and di % gn == 0
    st = gn // SSM_GROUPS
    gw = di // SSM_GROUPS
    q = SSM_CHUNK
    n_chunks = ts // q

    wz = w_in[:, :di].astype(BF16)
    wxbc = w_in[:, di:di + cd].astype(BF16)
    wdt_t = w_in[:, di + cd:].T
    wdt_hi = wdt_t.astype(BF16)
    wdt_lo = (wdt_t - wdt_hi.astype(F32)).astype(BF16)
    expand = (jnp.arange(di)[None, :] // hp == jnp.arange(nh)[:, None]).astype(BF16)
    dskip_e = jnp.repeat(d_skip, hp).reshape(1, di)
    idx = jnp.arange(q)
    triq = (idx[:, None] <= idx[None, :]).astype(BF16)

    ops = [nmix.reshape(1, d), wz, wxbc, wdt_hi, wdt_lo, conv_w, conv_b.reshape(1, cd),
           dt_bias.reshape(nh, 1), a_log.reshape(nh, 1), expand, dskip_e, norm_w.reshape(1, di),
           w_out.astype(BF16), triq]
    specs = [
        _const_spec((1, d)),
        _const_spec((d, di)),
        _const_spec((d, cd)),
        _const_spec((nh, d)),
        _const_spec((nh, d)),
        _const_spec((width, cd)),
        _const_spec((1, cd)),
        _const_spec((nh, 1)),
        _const_spec((nh, 1)),
        _const_spec((nh, di)),
        _const_spec((1, di)),
        _const_spec((1, di)),
        _const_spec((di, d)),
        _const_spec((q, q)),
    ]
    scratch = [
        pltpu.VMEM((CONV_HALO, cd), F32),
        pltpu.VMEM((ts, di), F32),
        pltpu.VMEM((ts, di), F32),
        pltpu.VMEM((ts, gn), F32),
        pltpu.VMEM((ts, gn), F32),
        pltpu.VMEM((ts, di), F32),
        pltpu.VMEM((n_chunks, nh, q), F32),
        pltpu.VMEM((n_chunks, nh, q), F32),
        pltpu.VMEM((SSM_GROUPS, st, gw), F32),
    ]
    return _mixer_call(_ssm_kernel, "ssm_mixer", prev, src, bsz, seq, ts, ops, specs, nffn, router, scratch)


def _sc_mesh():
    return plsc.VectorSubcoreMesh(core_axis_name="core", subcore_axis_name="subcore")


def _sc_worker_base(per_worker):
    return (lax.axis_index("subcore") * SC_CORES + lax.axis_index("core")) * per_worker


def _sc_scratch(d, dtype):
    win = SC_WINDOW
    return [pltpu.VMEM((win,), I32), pltpu.VMEM((win,), I32),
            pltpu.VMEM((win, d), dtype), pltpu.VMEM((win, d), dtype),
            pltpu.SemaphoreType.DMA, pltpu.SemaphoreType.DMA]


def _scatter_rows(src, idx, n_rows):
    n_src, d = src.shape
    win = SC_WINDOW
    per_worker = n_src // SC_WORKERS
    n_win = per_worker // win
    assert idx.shape[0] == MOE_TOP_K * n_src
    assert n_src == per_worker * SC_WORKERS and per_worker == n_win * win and n_win % 2 == 0
    scratch = ([pltpu.VMEM((win,), I32)] * (2 * MOE_TOP_K) + [pltpu.VMEM((win, d), src.dtype)] * 2
               + [pltpu.SemaphoreType.DMA] * (2 * MOE_TOP_K))

    @functools.partial(pl.kernel, out_type=jax.ShapeDtypeStruct((n_rows, d), src.dtype), mesh=_sc_mesh(),
                       scratch_types=scratch, name="moe_scatter_rows")
    def scatter_kernel(x_hbm, i_hbm, o_hbm, idx_a0, idx_a1, idx_b0, idx_b1, rows_a, rows_b,
                       sem_a0, sem_a1, sem_b0, sem_b1):
        base = _sc_worker_base(per_worker)

        def step(w, idx_vs, rows_v, sems, first):
            if not first:
                for idx_v, sem in zip(idx_vs, sems):
                    pltpu.make_async_copy(rows_v, o_hbm.at[idx_v], sem).wait()
            for k, idx_v in enumerate(idx_vs):
                pltpu.sync_copy(i_hbm.at[pl.ds(k * n_src + base + w * win, win)], idx_v)
            pltpu.sync_copy(x_hbm.at[pl.ds(base + w * win, win)], rows_v)
            for idx_v, sem in zip(idx_vs, sems):
                pltpu.make_async_copy(rows_v, o_hbm.at[idx_v], sem).start()

        buf_a = ((idx_a0, idx_a1), rows_a, (sem_a0, sem_a1))
        buf_b = ((idx_b0, idx_b1), rows_b, (sem_b0, sem_b1))
        step(0, *buf_a, True)
        step(1, *buf_b, True)

        @pl.loop(2, n_win, step=2)
        def _(w):
            step(w, *buf_a, False)
            step(w + 1, *buf_b, False)

        for idx_vs, rows_v, sems in (buf_a, buf_b):
            for idx_v, sem in zip(idx_vs, sems):
                pltpu.make_async_copy(rows_v, o_hbm.at[idx_v], sem).wait()

    return scatter_kernel(src, idx)


def _gather_rows(table, idx):
    d = table.shape[1]
    m = idx.shape[0]
    win = SC_WINDOW
    per_worker = m // SC_WORKERS
    n_win = per_worker // win
    assert m == per_worker * SC_WORKERS and per_worker == n_win * win and n_win % 2 == 0

    @functools.partial(pl.kernel, out_type=jax.ShapeDtypeStruct((m, d), table.dtype), mesh=_sc_mesh(),
                       scratch_types=_sc_scratch(d, table.dtype), name="moe_gather_rows")
    def gather_kernel(x_hbm, i_hbm, o_hbm, idx_a, idx_b, rows_a, rows_b, sem_a, sem_b):
        base = _sc_worker_base(per_worker)

        def out_copy(w, rows_v, sem):
            return pltpu.make_async_copy(rows_v, o_hbm.at[pl.ds(base + w * win, win)], sem)

        def step(w, idx_v, rows_v, sem, first):
            pltpu.sync_copy(i_hbm.at[pl.ds(base + w * win, win)], idx_v)
            if not first:
                out_copy(w, rows_v, sem).wait()
            pltpu.sync_copy(x_hbm.at[idx_v], rows_v)
            out_copy(w, rows_v, sem).start()

        step(0, idx_a, rows_a, sem_a, True)
        step(1, idx_b, rows_b, sem_b, True)

        @pl.loop(2, n_win, step=2)
        def _(w):
            step(w, idx_a, rows_a, sem_a, False)
            step(w + 1, idx_b, rows_b, sem_b, False)

        out_copy(0, rows_a, sem_a).wait()
        out_copy(0, rows_b, sem_b).wait()

    return gather_kernel(table, idx)


def _ffn_kernel(be_ref, nu_ref, nv_ref, slot_ref, nxt_ref, xs_ref, wgu_hbm, wdn_hbm, ys_ref,
                wgu_f32_ref, wdn_f32_ref, wgu_bf_ref, wdn_bf_ref, sem_ref, *, layer):
    i = pl.program_id(0)
    bm = xs_ref.shape[0]
    f = wdn_bf_ref.shape[0]

    def weight_copies(expert, slot):
        return (pltpu.make_async_copy(wgu_hbm.at[layer, expert], wgu_f32_ref.at[slot], sem_ref.at[slot, 0]),
                pltpu.make_async_copy(wdn_hbm.at[layer, expert], wdn_f32_ref.at[slot], sem_ref.at[slot, 1]))

    @pl.when(i == 0)
    def _():
        for cp in weight_copies(be_ref[0], slot_ref[0]):
            cp.start()

    first_of_expert = (i == 0) | (be_ref[i] != be_ref[jnp.maximum(i - 1, 0)])

    @pl.when(first_of_expert & (i < nu_ref[0]))
    def _():
        slot = slot_ref[i]
        for cp in weight_copies(be_ref[i], slot):
            cp.wait()

        @pl.when(nxt_ref[i] >= 0)
        def _():
            for cp in weight_copies(nxt_ref[i], 1 - slot):
                cp.start()

        wgu_bf_ref[...] = wgu_f32_ref[slot].astype(BF16)
        wdn_bf_ref[...] = wdn_f32_ref[slot].astype(BF16)

    def ffn_rows(n):
        valid = lax.broadcasted_iota(I32, (n, 1), 0) < nv_ref[i]
        h = _unpack_bf16(jnp.where(valid, xs_ref[0:n, :], 0)).astype(BF16)
        gu = _dot(h, wgu_bf_ref[...])
        act = _silu(gu[:, :f]) * gu[:, f:]
        ys_ref[0:n, :] = _pack_bf16(_dot(act.astype(BF16), wdn_bf_ref[...]).astype(BF16))

    used = i < nu_ref[0]
    half = bm // 2

    @pl.when(used & (nv_ref[i] > half))
    def _():
        ffn_rows(bm)

    @pl.when(used & (nv_ref[i] <= half))
    def _():
        ffn_rows(half)
        ys_ref[half:bm, :] = jnp.zeros((bm - half, ys_ref.shape[1]), ys_ref.dtype)

    @pl.when(i >= nu_ref[0])
    def _():
        ys_ref[...] = jnp.zeros_like(ys_ref)


def _block_schedule(cnt, n_blocks, bm):
    n_exp = cnt.shape[0]
    padded = (cnt + bm - 1) // bm * bm
    e_ids = jnp.arange(n_exp, dtype=I32)
    before = e_ids[None, :] < e_ids[:, None]
    pad_start = jnp.sum(jnp.where(before, padded[None, :], 0), axis=1).astype(I32)
    pad_end = pad_start + padded
    nonempty = cnt > 0
    seg_of = jnp.sum(before & nonempty[None, :], axis=1)
    after = (e_ids[None, :] > e_ids[:, None]) & nonempty[None, :]
    next_of = jnp.min(jnp.where(after, e_ids[None, :], n_exp), axis=1)
    next_of = jnp.where(next_of == n_exp, -1, next_of)
    blk_row = jnp.arange(n_blocks, dtype=I32) * bm
    block_e = jnp.minimum(jnp.sum(blk_row[:, None] >= pad_end[None, :], axis=1), n_exp - 1).astype(I32)
    of_block = block_e[:, None] == e_ids[None, :]

    def per_block(v):
        return jnp.sum(jnp.where(of_block, v[None, :], 0), axis=1).astype(I32)

    n_used = (jnp.sum(padded, keepdims=True) // bm).astype(I32)
    n_valid = jnp.clip(per_block(pad_start + cnt) - blk_row, 0, bm).astype(I32)
    return pad_start, (block_e, n_used, n_valid, per_block(seg_of % 2), per_block(next_of))


def _expert_ffn(xs, schedule, w_gu_all, w_dn_all, layer):
    n_rows, dh = xs.shape
    bm = FFN_BLOCK
    n_blocks = n_rows // bm
    d, f = w_gu_all.shape[2], w_dn_all.shape[2]
    assert d == 2 * dh

    def xs_map(i, be, nu, nv, sl, nx):
        return (jnp.minimum(i, jnp.maximum(nu[0] - 1, 0)), 0)

    return pl.pallas_call(
        functools.partial(_ffn_kernel, layer=layer),
        grid_spec=pltpu.PrefetchScalarGridSpec(
            num_scalar_prefetch=5,
            grid=(n_blocks,),
            in_specs=[
                pl.BlockSpec((bm, dh), xs_map),
                pl.BlockSpec(memory_space=pl.ANY),
                pl.BlockSpec(memory_space=pl.ANY),
            ],
            out_specs=pl.BlockSpec((bm, dh), lambda i, be, nu, nv, sl, nx: (i, 0)),
            scratch_shapes=[
                pltpu.VMEM((2, d, 2 * f), F32),
                pltpu.VMEM((2, f, d), F32),
                pltpu.VMEM((d, 2 * f), BF16),
                pltpu.VMEM((f, d), BF16),
                pltpu.SemaphoreType.DMA((2, 2)),
            ],
        ),
        out_shape=jax.ShapeDtypeStruct((n_rows, dh), I32),
        compiler_params=pltpu.CompilerParams(dimension_semantics=("arbitrary",),
                                             vmem_limit_bytes=VMEM_LIMIT_BYTES),
        name="expert_ffn",
    )(*schedule, xs, w_gu_all, w_dn_all)


def _dest_kernel(pad_start_ref, ids_ref, dest_ref):
    expert = ids_ref[0:MOE_TOP_K, :]
    dest = ids_ref[MOE_TOP_K:2 * MOE_TOP_K, :]
    for e in range(pad_start_ref.shape[0]):
        dest = dest + jnp.where(expert == e, pad_start_ref[e], 0)
    dest_ref[...] = dest


def _assignment_rows(ids, pad_start):
    t = ids.shape[1]
    tn = min(TS_DEST, t)
    assert t % tn == 0
    return pl.pallas_call(
        _dest_kernel,
        grid_spec=pltpu.PrefetchScalarGridSpec(
            num_scalar_prefetch=1,
            grid=(t // tn,),
            in_specs=[pl.BlockSpec((SUBLANES, tn), lambda i, ps: (0, i))],
            out_specs=pl.BlockSpec((MOE_TOP_K, tn), lambda i, ps: (0, i)),
        ),
        out_shape=jax.ShapeDtypeStruct((MOE_TOP_K, t), I32),
        compiler_params=pltpu.CompilerParams(dimension_semantics=("arbitrary",)),
        name="moe_dest",
    )(pad_start, ids)


def _moe(hp, ids, gate, counts, w_gu_all, w_dn_all, layer):
    t = hp.shape[0]
    n_exp = w_gu_all.shape[1]
    bm = FFN_BLOCK
    n_assign = t * MOE_TOP_K
    n_rows = (n_assign + bm - 1) // bm * bm + n_exp * bm
    n_blocks = n_rows // bm

    pad_start, schedule = _block_schedule(counts[:, 0], n_blocks, bm)
    dest = _assignment_rows(ids, pad_start).reshape(-1)
    xs = _scatter_rows(hp, dest, n_rows)
    ys = _expert_ffn(xs, schedule, w_gu_all, w_dn_all, layer)
    return _gather_rows(ys, dest), gate


def _final_kernel(xm_ref, y0_ref, y1_ref, g_ref, nfin_ref, *out_refs):
    out_refs[-1][...] = _rms(_combined(xm_ref, y0_ref, y1_ref, g_ref), nfin_ref[...])


def _final(xm, yg, gates, norm_final, out, group, n_groups):
    t, d = xm.shape
    ts = min(TS_FINAL, t)
    n_tiles = t // ts
    tile = pl.BlockSpec((ts, d), lambda i: (i, 0))
    in_specs = [tile, pl.BlockSpec((ts, d // 2), lambda i: (i, 0)),
                pl.BlockSpec((ts, d // 2), lambda i: (n_tiles + i, 0)),
                pl.BlockSpec((ts, SUBLANES), lambda i: (i, 0)), pl.BlockSpec((1, d), lambda i: (0, 0))]
    operands = [xm, yg, yg, gates, norm_final.reshape(1, d)]
    aliases = {}
    if out is not None:
        in_specs.append(pl.BlockSpec(memory_space=pl.ANY))
        operands.append(out)
        aliases = {len(operands) - 1: 0}
    return pl.pallas_call(
        _final_kernel,
        grid=(n_tiles,),
        in_specs=in_specs,
        out_specs=pl.BlockSpec((ts, d), lambda i: (group * n_tiles + i, 0)),
        out_shape=jax.ShapeDtypeStruct((n_groups * t, d), F32),
        input_output_aliases=aliases,
        compiler_params=pltpu.CompilerParams(dimension_semantics=("arbitrary",),
                                             vmem_limit_bytes=VMEM_LIMIT_BYTES),
        name="final_combine_norm",
    )(*operands)


def kernel(x, norm_mix, norm_ffn, norm_final, pool_w, pool_scale, sconv_in_w, sconv_taps, sconv_out_w, ssm_in_w, ssm_conv_w, ssm_conv_b, ssm_dt_bias, ssm_a_log, ssm_d, ssm_norm_w, ssm_out_w, router_group_w, router_group_b, router_expert_w, router_expert_b, expert_w_gu, expert_w_down):
    bsz, seq, d = x.shape
    depth = norm_mix.shape[0]
    assert (router_group_w.shape[2], router_expert_w.shape[2]) == ROUTER_GROUPS
    n_groups = TOKEN_GROUPS if bsz % TOKEN_GROUPS == 0 else 1
    gb = bsz // n_groups
    x2d = x.reshape(bsz * seq, d)
    src = [None] * n_groups
    prev = [None] * n_groups
    for i in range(depth):
        kind, j = i % N_MIXERS, i // N_MIXERS
        ts = min((TS_POOL, TS_SCONV, TS_SSM)[kind], seq)
        router = _router_operands(router_group_w[i], router_group_b[i], router_expert_w[i],
                                  router_expert_b[i], ts)
        for g in range(n_groups):
            s_in = Stream(x2d, g * gb * (seq // ts)) if i == 0 else src[g]
            if kind == 0:
                outs = _pool_layer(prev[g], s_in, gb, seq, ts, norm_mix[i], pool_w[j], pool_scale[j],
                                   norm_ffn[i], router)
            elif kind == 1:
                outs = _sconv_layer(prev[g], s_in, gb, seq, ts, norm_mix[i], sconv_in_w[j], sconv_taps[j],
                                    sconv_out_w[j], norm_ffn[i], router)
            else:
                outs = _ssm_layer(prev[g], s_in, gb, seq, ts, norm_mix[i], ssm_in_w[j], ssm_conv_w[j],
                                  ssm_conv_b[j], ssm_dt_bias[j], ssm_a_log[j], ssm_d[j], ssm_norm_w[j],
                                  ssm_out_w[j], norm_ffn[i], router)
            xm, hp, ids, gate, counts = outs
            src[g] = Stream(xm, 0)
            prev[g] = _moe(hp, ids, gate, counts, expert_w_gu, expert_w_down, i)
    out = None
    for g in range(n_groups):
        out = _final(src[g].rows, prev[g][0], prev[g][1], norm_final, out, g, n_groups)
    return out.reshape(bsz, seq, d)
```

```python
import functools
from typing import NamedTuple

import jax
import jax.numpy as jnp
from jax import lax
from jax.experimental import pallas as pl
from jax.experimental.pallas import tpu as pltpu
from jax.experimental.pallas import tpu_sc as plsc

F32 = jnp.float32
BF16 = jnp.bfloat16
I32 = jnp.int32

NORM_EPS = 1e-6
N_MIXERS = 3
POOL_WINDOWS = (2, 4, 8, 16)
POOL_HALO = 16
CONV_HALO = 8
SSM_GROUPS = 4
SSM_CHUNK = 128
SSM_CONV_WIDTH = 4
MOE_TOP_K = 2
TOKEN_GROUPS = 2
EXPERTS_PER_GROUP = 8
ROUTER_GROUPS = (4, 32)
ROUTER_ROWS = 48

SUBLANES = 8
LANES = 128
VMEM_LIMIT_BYTES = 56 * 1024 * 1024
SC_CORES = 2
SC_WORKERS = SC_CORES * 16
SC_WINDOW = 64

TS_POOL = 512
TS_SCONV = 512
TS_SSM = 256
TS_FINAL = 512
TS_DEST = 4096
FFN_BLOCK = 1024
FFN_SUB_BLOCK = 512

NT_DIMS = (((1,), (1,)), ((), ()))
TN_DIMS = (((0,), (0,)), ((), ()))


def _dot(a, b):
    return jnp.dot(a, b, preferred_element_type=F32)


def _dot_nt(a, b):
    return lax.dot_general(a, b, NT_DIMS, preferred_element_type=F32)


def _split3(a):
    a1 = a.astype(BF16)
    r1 = a - a1.astype(F32)
    a2 = r1.astype(BF16)
    a3 = (r1 - a2.astype(F32)).astype(BF16)
    return a1, a2, a3


def _dot3(a, b):
    a1, a2, a3 = _split3(a)
    return _dot(a1, b) + _dot(a2, b) + _dot(a3, b)


def _hilo(a):
    hi = a.astype(BF16)
    lo = (a - hi.astype(F32)).astype(BF16)
    return hi, lo


def _dot_nt_hilo(w_hi, w_lo, a_hi, a_lo):
    return _dot_nt(w_hi, a_hi) + _dot_nt(w_hi, a_lo) + _dot_nt(w_lo, a_hi)


def _rms(x, w):
    return x * lax.rsqrt(jnp.mean(x * x, axis=-1, keepdims=True) + NORM_EPS) * w


def _silu(x):
    half = 0.5 * x
    return half + half * jnp.tanh(half)


def _shift_rows(ext, k, halo):
    return pltpu.roll(ext, k, axis=0)[halo:]


HI_HALF_MASK = -65536


def _pack_bf16(a_bf):
    half = a_bf.shape[1] // 2
    lo = lax.bitcast_convert_type(a_bf[:, :half].astype(F32), I32)
    hi = lax.bitcast_convert_type(a_bf[:, half:].astype(F32), I32)
    return lax.shift_right_logical(lo, 16) | (hi & HI_HALF_MASK)


def _unpack_bf16(w):
    lo = lax.bitcast_convert_type(lax.shift_left(w, 16), F32)
    hi = lax.bitcast_convert_type(w & HI_HALF_MASK, F32)
    return jnp.concatenate([lo, hi], axis=1)


def _combined(xm_ref, y0_ref, y1_ref, g_ref):
    g = g_ref[...]
    return xm_ref[...] + g[:, 0:1] * _unpack_bf16(y0_ref[...]) + g[:, 1:2] * _unpack_bf16(y1_ref[...])


def _route(xm, first, nffn_ref, wr_hi_ref, wr_lo_ref, br_ref, tri_ref, base_ref, hp_ref, ids_ref, gate_ref,
           cnt_ref):
    n_groups, n_experts = ROUTER_GROUPS
    ts = xm.shape[0]

    @pl.when(first)
    def _():
        base_ref[...] = jnp.zeros_like(base_ref)

    h2 = _rms(xm, nffn_ref[...])
    h_hi, h_lo = _hilo(h2)
    hp_ref[...] = _pack_bf16(h_hi)
    lt = _dot_nt_hilo(wr_hi_ref[...], wr_lo_ref[...], h_hi, h_lo) + br_ref[...]

    row8 = lax.broadcasted_iota(I32, (SUBLANES, ts), 0)
    neg_inf = jnp.float32(-jnp.inf)
    g = jnp.where(row8 < n_groups, lt[0:SUBLANES], neg_inf)
    gmax = jnp.max(g, axis=0, keepdims=True)
    gsel = jnp.min(jnp.where(g == gmax, row8, SUBLANES), axis=0, keepdims=True)
    pgrp = 1.0 / jnp.sum(jnp.exp(g - gmax), axis=0, keepdims=True)

    sel = lt[SUBLANES:2 * SUBLANES]
    for j in range(1, n_groups):
        sel = jnp.where(gsel == j, lt[(j + 1) * SUBLANES:(j + 2) * SUBLANES], sel)
    m1 = jnp.max(sel, axis=0, keepdims=True)
    i1 = jnp.min(jnp.where(sel == m1, row8, SUBLANES), axis=0, keepdims=True)
    sel2 = jnp.where(row8 == i1, neg_inf, sel)
    m2 = jnp.max(sel2, axis=0, keepdims=True)
    i2 = jnp.min(jnp.where(sel2 == m2, row8, SUBLANES), axis=0, keepdims=True)
    r = jnp.exp(m2 - m1)
    den = 1.0 + r
    g0 = pgrp / den
    g1 = pgrp * r / den
    e0 = gsel * EXPERTS_PER_GROUP + i1
    e1 = gsel * EXPERTS_PER_GROUP + i2

    row_e = lax.broadcasted_iota(I32, (n_experts, ts), 0)
    oh0 = row_e == e0
    oh1 = row_e == e1
    oh = jnp.where(oh0 | oh1, 1.0, 0.0).astype(F32)
    prefix = _dot(oh.astype(BF16), tri_ref[...])
    tot = prefix + base_ref[:, 0:1]
    rank0 = jnp.sum(jnp.where(oh0, tot, 0.0), axis=0, keepdims=True).astype(I32)
    rank1 = jnp.sum(jnp.where(oh1, tot, 0.0), axis=0, keepdims=True).astype(I32)
    base_ref[...] = base_ref[...] + jnp.sum(oh, axis=1, keepdims=True)

    ids_ref[...] = jnp.where(row8 == 0, e0, jnp.where(row8 == 1, e1, jnp.where(row8 == 2, rank0, rank1)))
    eye = (lax.broadcasted_iota(I32, (SUBLANES, SUBLANES), 0)
           == lax.broadcasted_iota(I32, (SUBLANES, SUBLANES), 1)).astype(BF16)
    gate_ref[...] = sum(lax.dot_general(part, eye, TN_DIMS, preferred_element_type=F32)
                        for part in _split3(jnp.where(row8 == 0, g0, g1)))
    cnt_ref[...] = base_ref[...].astype(I32)


def _router_operands(w_rg, b_rg, w_re, b_re, ts):
    d, n_groups = w_rg.shape
    n_experts = w_re.shape[1]
    assert n_experts == n_groups * EXPERTS_PER_GROUP and n_groups <= SUBLANES
    assert SUBLANES + n_experts <= ROUTER_ROWS
    wt = jnp.zeros((ROUTER_ROWS, d), F32)
    wt = wt.at[0:n_groups].set(w_rg.T).at[SUBLANES:SUBLANES + n_experts].set(w_re.T)
    bt = jnp.zeros((ROUTER_ROWS, 1), F32)
    bt = bt.at[0:n_groups, 0].set(b_rg).at[SUBLANES:SUBLANES + n_experts, 0].set(b_re)
    w_hi = wt.astype(BF16)
    w_lo = (wt - w_hi.astype(F32)).astype(BF16)
    idx = jnp.arange(ts)
    tri = (idx[:, None] < idx[None, :]).astype(BF16)
    return w_hi, w_lo, bt, tri


def _const_spec(shape):
    zeros = (0,) * len(shape)
    return pl.BlockSpec(shape, lambda b, s: zeros, pipeline_mode=pl.Buffered(1))


class Stream(NamedTuple):
    rows: jax.Array
    tile0: int


def _mixer_in(prev, src, t, ts, n_s, d):
    n_tiles = t // ts
    tile0 = src.tile0
    tile = pl.BlockSpec((ts, d), lambda b, s: (tile0 + b * n_s + s, 0))
    if prev is None:
        return [src.rows], [tile]
    yg, gates = prev
    specs = [
        tile,
        pl.BlockSpec((ts, d // 2), lambda b, s: (b * n_s + s, 0)),
        pl.BlockSpec((ts, d // 2), lambda b, s: (n_tiles + b * n_s + s, 0)),
        pl.BlockSpec((ts, SUBLANES), lambda b, s: (b * n_s + s, 0)),
    ]
    return [src.rows, yg, yg, gates], specs


def _router_in(nffn, router, d, ts):
    w_hi, w_lo, bt, tri = router
    specs = [
        _const_spec((1, d)),
        _const_spec((ROUTER_ROWS, d)),
        _const_spec((ROUTER_ROWS, d)),
        _const_spec((ROUTER_ROWS, 1)),
        _const_spec((ts, ts)),
    ]
    return [nffn.reshape(1, d), w_hi, w_lo, bt, tri], specs


def _mixer_out(t, d, ts, n_s):
    n_tiles = t // ts
    n_experts = ROUTER_GROUPS[1]
    out_shape = [
        jax.ShapeDtypeStruct((t, d), F32),
        jax.ShapeDtypeStruct((t, d // 2), I32),
        jax.ShapeDtypeStruct((SUBLANES, t), I32),
        jax.ShapeDtypeStruct((t, SUBLANES), F32),
        jax.ShapeDtypeStruct((n_experts, LANES), I32),
    ]
    out_specs = [
        pl.BlockSpec((ts, d), lambda b, s: (b * n_s + s, 0)),
        pl.BlockSpec((ts, d // 2), lambda b, s: (b * n_s + s, 0)),
        pl.BlockSpec((SUBLANES, ts), lambda b, s: (0, b * n_s + s)),
        pl.BlockSpec((ts, SUBLANES), lambda b, s: (b * n_s + s, 0)),
        pl.BlockSpec((n_experts, LANES), lambda b, s: (0, 0)),
    ]
    return out_shape, out_specs


def _mixer_call(kern, name, prev, src, bsz, seq, ts, mixer_ops, mixer_specs, nffn, router, scratch):
    t, d = bsz * seq, src.rows.shape[1]
    n_s = seq // ts
    tok_ops, tok_specs = _mixer_in(prev, src, t, ts, n_s, d)
    r_ops, r_specs = _router_in(nffn, router, d, ts)
    out_shape, out_specs = _mixer_out(t, d, ts, n_s)
    return pl.pallas_call(
        functools.partial(kern, n_tok=len(tok_ops)),
        grid=(bsz, n_s),
        in_specs=tok_specs + mixer_specs + r_specs,
        out_specs=out_specs,
        out_shape=out_shape,
        scratch_shapes=scratch + [pltpu.VMEM((ROUTER_GROUPS[1], LANES), F32)],
        compiler_params=pltpu.CompilerParams(dimension_semantics=("arbitrary", "arbitrary"),
                                             vmem_limit_bytes=VMEM_LIMIT_BYTES),
        name=name,
    )(*tok_ops, *mixer_ops, *r_ops)


def _token_input(refs, n_tok):
    return refs[0][...] if n_tok == 1 else _combined(*refs[:n_tok])


def _pool_kernel(*refs, n_tok):
    x = _token_input(refs, n_tok)
    (nmix_ref, pw_ref, scale_ref,
     nffn_ref, wr_hi_ref, wr_lo_ref, br_ref, tri_ref,
     xm_ref, hp_ref, ids_ref, gate_ref, cnt_ref,
     carry_ref, base_ref) = refs[n_tok:]
    b = pl.program_id(0)
    s = pl.program_id(1)
    ts, d = x.shape
    ch = d // len(POOL_WINDOWS)

    @pl.when(s == 0)
    def _():
        carry_ref[...] = jnp.zeros_like(carry_ref)

    h = _rms(x, nmix_ref[...])
    ext = jnp.concatenate([carry_ref[...], h], axis=0)
    carry_ref[...] = h[ts - POOL_HALO:]

    pos = s * ts + lax.broadcasted_iota(I32, (ts, 1), 0) + 1
    ys = []
    for g, win in enumerate(POOL_WINDOWS):
        acc = ext[:, g * ch:(g + 1) * ch]
        k = 1
        while k < win:
            acc = acc + pltpu.roll(acc, k, axis=0)
            k *= 2
        inv_cnt = 1.0 / jnp.minimum(pos, win).astype(F32)
        pooled = acc[POOL_HALO:] * inv_cnt - h[:, g * ch:(g + 1) * ch]
        ys.append(_dot(pooled.astype(BF16), pw_ref[g]))
    y = jnp.concatenate(ys, axis=1) * scale_ref[...]
    xm = x + y
    xm_ref[...] = xm
    _route(xm, (b == 0) & (s == 0), nffn_ref, wr_hi_ref, wr_lo_ref, br_ref, tri_ref, base_ref,
           hp_ref, ids_ref, gate_ref, cnt_ref)


def _pool_layer(prev, src, bsz, seq, ts, nmix, pool_w, pool_scale, nffn, router):
    d = src.rows.shape[1]
    n_win, ch, _ = pool_w.shape
    assert n_win == len(POOL_WINDOWS) and ch * n_win == d
    ops = [nmix.reshape(1, d), pool_w.astype(BF16), pool_scale.reshape(1, d)]
    specs = [_const_spec((1, d)), _const_spec((n_win, ch, ch)), _const_spec((1, d))]
    return _mixer_call(_pool_kernel, "pool_mixer", prev, src, bsz, seq, ts, ops, specs, nffn, router,
                       [pltpu.VMEM((POOL_HALO, d), F32)])


def _sconv_kernel(*refs, n_tok):
    x = _token_input(refs, n_tok)
    (nmix_ref, win_ref, taps_ref, wout_ref,
     nffn_ref, wr_hi_ref, wr_lo_ref, br_ref, tri_ref,
     xm_ref, hp_ref, ids_ref, gate_ref, cnt_ref,
     carry_ref, base_ref) = refs[n_tok:]
    b = pl.program_id(0)
    s = pl.program_id(1)
    ts, d = x.shape

    @pl.when(s == 0)
    def _():
        carry_ref[...] = jnp.zeros_like(carry_ref)

    h = _rms(x, nmix_ref[...]).astype(BF16)
    b_gate = _dot(h, win_ref[:, 0:d])
    c_gate = _dot(h, win_ref[:, d:2 * d])
    v = _dot(h, win_ref[:, 2 * d:3 * d])
    u_pre = c_gate * v
    ext = jnp.concatenate([carry_ref[...], u_pre], axis=0)
    carry_ref[...] = u_pre[ts - CONV_HALO:]
    width = taps_ref.shape[0]
    u = taps_ref[width - 1:width, :] * u_pre
    for k in range(1, width):
        u = u + taps_ref[width - 1 - k:width - k, :] * _shift_rows(ext, k, CONV_HALO)
    mix = _dot((b_gate * u).astype(BF16), wout_ref[...])
    xm = x + mix
    xm_ref[...] = xm
    _route(xm, (b == 0) & (s == 0), nffn_ref, wr_hi_ref, wr_lo_ref, br_ref, tri_ref, base_ref,
           hp_ref, ids_ref, gate_ref, cnt_ref)


def _sconv_layer(prev, src, bsz, seq, ts, nmix, w_in, taps, w_out, nffn, router):
    d = src.rows.shape[1]
    width = taps.shape[0]
    assert width - 1 <= CONV_HALO
    ops = [nmix.reshape(1, d), w_in.astype(BF16), taps, w_out.astype(BF16)]
    specs = [_const_spec((1, d)), _const_spec((d, 3 * d)), _const_spec((width, d)), _const_spec((d, d))]
    return _mixer_call(_sconv_kernel, "sconv_mixer", prev, src, bsz, seq, ts, ops, specs, nffn, router,
                       [pltpu.VMEM((CONV_HALO, d), F32)])


def _ssm_kernel(*refs, n_tok):
    x = _token_input(refs, n_tok)
    (nmix_ref, wz_ref, wxbc_ref, wdt_hi_ref, wdt_lo_ref, convw_ref, convb_ref,
     dtb_ref, alog_ref, expand_ref, dskip_ref, normw_ref, wout_ref, triq_ref,
     nffn_ref, wr_hi_ref, wr_lo_ref, br_ref, tri_ref,
     xm_ref, hp_ref, ids_ref, gate_ref, cnt_ref,
     carry_ref, z_ref, xs_ref, bm_ref, cm_ref, y_ref, dtt_ref, at_ref, state_ref, base_ref) = refs[n_tok:]
    b = pl.program_id(0)
    s = pl.program_id(1)
    ts, d = x.shape
    di = z_ref.shape[1]
    gn = bm_ref.shape[1]
    cd = di + 2 * gn
    nh = alog_ref.shape[0]
    hp = di // nh
    grp = SSM_GROUPS
    st = gn // grp
    hpg = nh // grp
    gw = hpg * hp
    q = SSM_CHUNK
    n_chunks = ts // q

    @pl.when(s == 0)
    def _():
        carry_ref[...] = jnp.zeros_like(carry_ref)
        state_ref[...] = jnp.zeros_like(state_ref)

    h = _rms(x, nmix_ref[...])
    h_hi, h_lo = _hilo(h)

    col_blk = gn

    def project(c0):
        cols = slice(c0, c0 + col_blk)
        if c0 < di:
            z_ref[:, cols] = _dot(h_hi, wz_ref[:, cols])
        return _dot(h_hi, wxbc_ref[:, cols])

    pre_next = project(0)
    for c0 in range(0, cd, col_blk):
        cols = slice(c0, c0 + col_blk)
        pre = pre_next
        if c0 + col_blk < cd:
            pre_next = project(c0 + col_blk)
        ext = jnp.concatenate([carry_ref[:, cols], pre], axis=0)
        carry_ref[:, cols] = pre[ts - CONV_HALO:]
        ext1 = pltpu.roll(ext, 1, axis=0)
        pair = convw_ref[1:2, cols] * ext + convw_ref[0:1, cols] * ext1
        acc = (convw_ref[3:4, cols] * pre + convb_ref[:, cols] + convw_ref[2:3, cols] * ext1[CONV_HALO:]
               + _shift_rows(pair, 2, CONV_HALO))
        act = _silu(acc)
        if c0 < di:
            xs_ref[:, cols] = act
        elif c0 < di + gn:
            bm_ref[...] = act
        else:
            cm_ref[...] = act

    dt_t = _dot_nt_hilo(wdt_hi_ref[...], wdt_lo_ref[...], h_hi, h_lo) + dtb_ref[...]
    dt_t = jnp.maximum(dt_t, 0.0) + jnp.log(1.0 + jnp.exp(-jnp.abs(dt_t)))
    a_t = dt_t * (-jnp.exp(alog_ref[...]))
    for c in range(n_chunks):
        dtt_ref[c] = dt_t[:, c * q:(c + 1) * q]
        at_ref[c] = a_t[:, c * q:(c + 1) * q]

    row_q = lax.broadcasted_iota(I32, (q, q), 0)
    col_q = lax.broadcasted_iota(I32, (q, q), 1)
    causal = row_q >= col_q
    lane_lo = lax.broadcasted_iota(I32, (q, 2 * hp), 1) < hp
    neg_inf = jnp.float32(-jnp.inf)
    expand = expand_ref[...]

    def chunk_body(c, carry):
        r0 = pl.multiple_of(c * q, q)
        rows = pl.ds(r0, q)
        acs_t = _dot3(at_ref[c], triq_ref[...])
        dt_c = dtt_ref[c]
        w_t = dt_c * jnp.exp(acs_t[:, q - 1:q] - acs_t)
        src_t = acs_t - jnp.log(dt_c)
        acs = acs_t.T
        od_hi, od_lo = _hilo(jnp.exp(acs))
        out_decay_e = _dot(od_hi, expand) + _dot(od_lo, expand)
        w_e = _dot(w_t.T.astype(BF16), expand)
        xc = xs_ref[rows, :]
        xw = xc * w_e
        bc = bm_ref[rows, :]
        cc = cm_ref[rows, :]
        for g in range(grp):
            gcols = slice(g * gw, (g + 1) * gw)
            bg = bc[:, g * st:(g + 1) * st].astype(BF16)
            cg = cc[:, g * st:(g + 1) * st].astype(BF16)
            xg = xc[:, gcols]
            cb = _dot_nt(cg, bg)
            hprev = state_ref[g]
            y_off = _dot(cg, hprev.astype(BF16)) * out_decay_e[:, gcols]
            new_states = lax.dot_general(bg, xw[:, gcols].astype(BF16), TN_DIMS, preferred_element_type=F32)
            state_ref[g] = hprev * out_decay_e[q - 1:q, gcols] + new_states
            for j in range(hpg // 2):
                h0 = g * hpg + 2 * j
                ms = []
                for hh in (h0, h0 + 1):
                    seg = acs[:, hh:hh + 1] - src_t[hh:hh + 1, :]
                    ms.append((cb * jnp.exp(jnp.where(causal, seg, neg_inf))).astype(BF16))
                lhs = jnp.concatenate(ms, axis=1)
                xp = xg[:, 2 * j * hp:(2 * j + 2) * hp]
                rhs = jnp.concatenate([jnp.where(lane_lo, xp, 0.0), jnp.where(lane_lo, 0.0, xp)],
                                      axis=0).astype(BF16)
                pc = slice(g * gw + 2 * j * hp, g * gw + (2 * j + 2) * hp)
                y_ref[rows, pc] = _dot(lhs, rhs) + y_off[:, 2 * j * hp:(2 * j + 2) * hp]
        return carry

    lax.fori_loop(0, n_chunks, chunk_body, 0)

    out = None
    for g in range(grp):
        gcols = slice(g * gw, (g + 1) * gw)
        yg = (y_ref[:, gcols] + xs_ref[:, gcols] * dskip_ref[:, gcols]) * _silu(z_ref[:, gcols])
        yn = yg * lax.rsqrt(jnp.mean(yg * yg, axis=-1, keepdims=True) + NORM_EPS) * normw_ref[:, gcols]
        part = _dot(yn.astype(BF16), wout_ref[gcols, :])
        out = part if out is None else out + part
    xm = x + out
    xm_ref[...] = xm
    _route(xm, (b == 0) & (s == 0), nffn_ref, wr_hi_ref, wr_lo_ref, br_ref, tri_ref, base_ref,
           hp_ref, ids_ref, gate_ref, cnt_ref)


def _ssm_layer(prev, src, bsz, seq, ts, nmix, w_in, conv_w, conv_b, dt_bias, a_log, d_skip, norm_w, w_out,
               nffn, router):
    d = src.rows.shape[1]
    assert ts % SSM_CHUNK == 0
    nh = a_log.shape[0]
    di = norm_w.shape[0]
    cd = conv_w.shape[1]
    gn = (cd - di) // 2
    hp = di // nh
    width = conv_w.shape[0]
    assert width == SSM_CONV_WIDTH and (nh // SSM_GROUPS) % 2 == 0 and di % gn == 0
    st = gn // SSM_GROUPS
    gw = di // SSM_GROUPS
    q = SSM_CHUNK
    n_chunks = ts // q

    wz = w_in[:, :di].astype(BF16)
    wxbc = w_in[:, di:di + cd].astype(BF16)
    wdt_t = w_in[:, di + cd:].T
    wdt_hi = wdt_t.astype(BF16)
    wdt_lo = (wdt_t - wdt_hi.astype(F32)).astype(BF16)
    expand = (jnp.arange(di)[None, :] // hp == jnp.arange(nh)[:, None]).astype(BF16)
    dskip_e = jnp.repeat(d_skip, hp).reshape(1, di)
    idx = jnp.arange(q)
    triq = (idx[:, None] <= idx[None, :]).astype(BF16)

    ops = [nmix.reshape(1, d), wz, wxbc, wdt_hi, wdt_lo, conv_w, conv_b.reshape(1, cd),
           dt_bias.reshape(nh, 1), a_log.reshape(nh, 1), expand, dskip_e, norm_w.reshape(1, di),
           w_out.astype(BF16), triq]
    specs = [
        _const_spec((1, d)),
        _const_spec((d, di)),
        _const_spec((d, cd)),
        _const_spec((nh, d)),
        _const_spec((nh, d)),
        _const_spec((width, cd)),
        _const_spec((1, cd)),
        _const_spec((nh, 1)),
        _const_spec((nh, 1)),
        _const_spec((nh, di)),
        _const_spec((1, di)),
        _const_spec((1, di)),
        _const_spec((di, d)),
        _const_spec((q, q)),
    ]
    scratch = [
        pltpu.VMEM((CONV_HALO, cd), F32),
        pltpu.VMEM((ts, di), F32),
        pltpu.VMEM((ts, di), F32),
        pltpu.VMEM((ts, gn), F32),
        pltpu.VMEM((ts, gn), F32),
        pltpu.VMEM((ts, di), F32),
        pltpu.VMEM((n_chunks, nh, q), F32),
        pltpu.VMEM((n_chunks, nh, q), F32),
        pltpu.VMEM((SSM_GROUPS, st, gw), F32),
    ]
    return _mixer_call(_ssm_kernel, "ssm_mixer", prev, src, bsz, seq, ts, ops, specs, nffn, router, scratch)


def _sc_mesh():
    return plsc.VectorSubcoreMesh(core_axis_name="core", subcore_axis_name="subcore")


def _sc_worker_base(per_worker):
    return (lax.axis_index("subcore") * SC_CORES + lax.axis_index("core")) * per_worker


def _sc_scratch(d, dtype):
    win = SC_WINDOW
    return [pltpu.VMEM((win,), I32), pltpu.VMEM((win,), I32),
            pltpu.VMEM((win, d), dtype), pltpu.VMEM((win, d), dtype),
            pltpu.SemaphoreType.DMA, pltpu.SemaphoreType.DMA]


def _scatter_rows(src, idx, n_rows):
    n_src, d = src.shape
    win = SC_WINDOW
    per_worker = n_src // SC_WORKERS
    n_win = per_worker // win
    assert idx.shape[0] == MOE_TOP_K * n_src
    assert n_src == per_worker * SC_WORKERS and per_worker == n_win * win and n_win % 2 == 0
    scratch = ([pltpu.VMEM((win,), I32)] * (2 * MOE_TOP_K) + [pltpu.VMEM((win, d), src.dtype)] * 2
               + [pltpu.SemaphoreType.DMA] * (2 * MOE_TOP_K))

    @functools.partial(pl.kernel, out_type=jax.ShapeDtypeStruct((n_rows, d), src.dtype), mesh=_sc_mesh(),
                       scratch_types=scratch, name="moe_scatter_rows")
    def scatter_kernel(x_hbm, i_hbm, o_hbm, idx_a0, idx_a1, idx_b0, idx_b1, rows_a, rows_b,
                       sem_a0, sem_a1, sem_b0, sem_b1):
        base = _sc_worker_base(per_worker)

        def step(w, idx_vs, rows_v, sems, first):
            if not first:
                for idx_v, sem in zip(idx_vs, sems):
                    pltpu.make_async_copy(rows_v, o_hbm.at[idx_v], sem).wait()
            for k, idx_v in enumerate(idx_vs):
                pltpu.sync_copy(i_hbm.at[pl.ds(k * n_src + base + w * win, win)], idx_v)
            pltpu.sync_copy(x_hbm.at[pl.ds(base + w * win, win)], rows_v)
            for idx_v, sem in zip(idx_vs, sems):
                pltpu.make_async_copy(rows_v, o_hbm.at[idx_v], sem).start()

        buf_a = ((idx_a0, idx_a1), rows_a, (sem_a0, sem_a1))
        buf_b = ((idx_b0, idx_b1), rows_b, (sem_b0, sem_b1))
        step(0, *buf_a, True)
        step(1, *buf_b, True)

        @pl.loop(2, n_win, step=2)
        def _(w):
            step(w, *buf_a, False)
            step(w + 1, *buf_b, False)

        for idx_vs, rows_v, sems in (buf_a, buf_b):
            for idx_v, sem in zip(idx_vs, sems):
                pltpu.make_async_copy(rows_v, o_hbm.at[idx_v], sem).wait()

    return scatter_kernel(src, idx)


def _gather_rows(table, idx):
    d = table.shape[1]
    m = idx.shape[0]
    win = SC_WINDOW
    per_worker = m // SC_WORKERS
    n_win = per_worker // win
    assert m == per_worker * SC_WORKERS and per_worker == n_win * win and n_win % 2 == 0

    @functools.partial(pl.kernel, out_type=jax.ShapeDtypeStruct((m, d), table.dtype), mesh=_sc_mesh(),
                       scratch_types=_sc_scratch(d, table.dtype), name="moe_gather_rows")
    def gather_kernel(x_hbm, i_hbm, o_hbm, idx_a, idx_b, rows_a, rows_b, sem_a, sem_b):
        base = _sc_worker_base(per_worker)

        def out_copy(w, rows_v, sem):
            return pltpu.make_async_copy(rows_v, o_hbm.at[pl.ds(base + w * win, win)], sem)

        def step(w, idx_v, rows_v, sem, first):
            pltpu.sync_copy(i_hbm.at[pl.ds(base + w * win, win)], idx_v)
            if not first:
                out_copy(w, rows_v, sem).wait()
            pltpu.sync_copy(x_hbm.at[idx_v], rows_v)
            out_copy(w, rows_v, sem).start()

        step(0, idx_a, rows_a, sem_a, True)
        step(1, idx_b, rows_b, sem_b, True)

        @pl.loop(2, n_win, step=2)
        def _(w):
            step(w, idx_a, rows_a, sem_a, False)
            step(w + 1, idx_b, rows_b, sem_b, False)

        out_copy(0, rows_a, sem_a).wait()
        out_copy(0, rows_b, sem_b).wait()

    return gather_kernel(table, idx)


def _ffn_kernel(be_ref, nu_ref, nv_ref, slot_ref, nxt_ref, xs_ref, wgu_hbm, wdn_hbm, ys_ref,
                wgu_f32_ref, wdn_f32_ref, wgu_bf_ref, wdn_bf_ref, sem_ref, *, layer):
    i = pl.program_id(0)
    bm = xs_ref.shape[0]
    f = wdn_bf_ref.shape[0]

    def weight_copies(expert, slot):
        return (pltpu.make_async_copy(wgu_hbm.at[layer, expert], wgu_f32_ref.at[slot], sem_ref.at[slot, 0]),
                pltpu.make_async_copy(wdn_hbm.at[layer, expert], wdn_f32_ref.at[slot], sem_ref.at[slot, 1]))

    @pl.when(i == 0)
    def _():
        for cp in weight_copies(be_ref[0], slot_ref[0]):
            cp.start()

    first_of_expert = (i == 0) | (be_ref[i] != be_ref[jnp.maximum(i - 1, 0)])

    @pl.when(first_of_expert & (i < nu_ref[0]))
    def _():
        slot = slot_ref[i]
        for cp in weight_copies(be_ref[i], slot):
            cp.wait()

        @pl.when(nxt_ref[i] >= 0)
        def _():
            for cp in weight_copies(nxt_ref[i], 1 - slot):
                cp.start()

        wgu_bf_ref[...] = wgu_f32_ref[slot].astype(BF16)
        wdn_bf_ref[...] = wdn_f32_ref[slot].astype(BF16)

    def ffn_rows(r0, n, n_valid):
        valid = lax.broadcasted_iota(I32, (n, 1), 0) < n_valid
        h = _unpack_bf16(jnp.where(valid, xs_ref[r0:r0 + n, :], 0)).astype(BF16)
        gu = _dot(h, wgu_bf_ref[...])
        act = _silu(gu[:, :f]) * gu[:, f:]
        ys_ref[r0:r0 + n, :] = _pack_bf16(_dot(act.astype(BF16), wdn_bf_ref[...]).astype(BF16))

    def zero_rows(r0, n):
        ys_ref[r0:r0 + n, :] = jnp.zeros((n, ys_ref.shape[1]), ys_ref.dtype)

    used = i < nu_ref[0]
    sub = FFN_SUB_BLOCK
    for r0 in range(0, bm, sub):
        left = nv_ref[i] - r0

        @pl.when(used & (left > sub // 2))
        def _():
            ffn_rows(r0, sub, left)

        @pl.when(used & (left > 0) & (left <= sub // 2))
        def _():
            ffn_rows(r0, sub // 2, left)
            zero_rows(r0 + sub // 2, sub // 2)

        @pl.when(jnp.logical_not(used) | (left <= 0))
        def _():
            zero_rows(r0, sub)


def _block_schedule(cnt, n_blocks, bm):
    n_exp = cnt.shape[0]
    padded = (cnt + bm - 1) // bm * bm
    e_ids = jnp.arange(n_exp, dtype=I32)
    before = e_ids[None, :] < e_ids[:, None]
    pad_start = jnp.sum(jnp.where(before, padded[None, :], 0), axis=1).astype(I32)
    pad_end = pad_start + padded
    nonempty = cnt > 0
    seg_of = jnp.sum(before & nonempty[None, :], axis=1)
    after = (e_ids[None, :] > e_ids[:, None]) & nonempty[None, :]
    next_of = jnp.min(jnp.where(after, e_ids[None, :], n_exp), axis=1)
    next_of = jnp.where(next_of == n_exp, -1, next_of)
    blk_row = jnp.arange(n_blocks, dtype=I32) * bm
    block_e = jnp.minimum(jnp.sum(blk_row[:, None] >= pad_end[None, :], axis=1), n_exp - 1).astype(I32)
    of_block = block_e[:, None] == e_ids[None, :]

    def per_block(v):
        return jnp.sum(jnp.where(of_block, v[None, :], 0), axis=1).astype(I32)

    n_used = (jnp.sum(padded, keepdims=True) // bm).astype(I32)
    n_valid = jnp.clip(per_block(pad_start + cnt) - blk_row, 0, bm).astype(I32)
    return pad_start, (block_e, n_used, n_valid, per_block(seg_of % 2), per_block(next_of))


def _expert_ffn(xs, schedule, w_gu_all, w_dn_all, layer):
    n_rows, dh = xs.shape
    bm = FFN_BLOCK
    n_blocks = n_rows // bm
    d, f = w_gu_all.shape[2], w_dn_all.shape[2]
    assert d == 2 * dh

    def xs_map(i, be, nu, nv, sl, nx):
        return (jnp.minimum(i, jnp.maximum(nu[0] - 1, 0)), 0)

    return pl.pallas_call(
        functools.partial(_ffn_kernel, layer=layer),
        grid_spec=pltpu.PrefetchScalarGridSpec(
            num_scalar_prefetch=5,
            grid=(n_blocks,),
            in_specs=[
                pl.BlockSpec((bm, dh), xs_map),
                pl.BlockSpec(memory_space=pl.ANY),
                pl.BlockSpec(memory_space=pl.ANY),
            ],
            out_specs=pl.BlockSpec((bm, dh), lambda i, be, nu, nv, sl, nx: (i, 0)),
            scratch_shapes=[
                pltpu.VMEM((2, d, 2 * f), F32),
                pltpu.VMEM((2, f, d), F32),
                pltpu.VMEM((d, 2 * f), BF16),
                pltpu.VMEM((f, d), BF16),
                pltpu.SemaphoreType.DMA((2, 2)),
            ],
        ),
        out_shape=jax.ShapeDtypeStruct((n_rows, dh), I32),
        compiler_params=pltpu.CompilerParams(dimension_semantics=("arbitrary",),
                                             vmem_limit_bytes=VMEM_LIMIT_BYTES),
        name="expert_ffn",
    )(*schedule, xs, w_gu_all, w_dn_all)


def _dest_kernel(pad_start_ref, ids_ref, dest_ref):
    expert = ids_ref[0:MOE_TOP_K, :]
    dest = ids_ref[MOE_TOP_K:2 * MOE_TOP_K, :]
    for e in range(pad_start_ref.shape[0]):
        dest = dest + jnp.where(expert == e, pad_start_ref[e], 0)
    dest_ref[...] = dest


def _assignment_rows(ids, pad_start):
    t = ids.shape[1]
    tn = min(TS_DEST, t)
    assert t % tn == 0
    return pl.pallas_call(
        _dest_kernel,
        grid_spec=pltpu.PrefetchScalarGridSpec(
            num_scalar_prefetch=1,
            grid=(t // tn,),
            in_specs=[pl.BlockSpec((SUBLANES, tn), lambda i, ps: (0, i))],
            out_specs=pl.BlockSpec((MOE_TOP_K, tn), lambda i, ps: (0, i)),
        ),
        out_shape=jax.ShapeDtypeStruct((MOE_TOP_K, t), I32),
        compiler_params=pltpu.CompilerParams(dimension_semantics=("arbitrary",)),
        name="moe_dest",
    )(pad_start, ids)


def _moe(hp, ids, gate, counts, w_gu_all, w_dn_all, layer):
    t = hp.shape[0]
    n_exp = w_gu_all.shape[1]
    bm = FFN_BLOCK
    n_assign = t * MOE_TOP_K
    n_rows = (n_assign + bm - 1) // bm * bm + n_exp * bm
    n_blocks = n_rows // bm

    pad_start, schedule = _block_schedule(counts[:, 0], n_blocks, bm)
    dest = _assignment_rows(ids, pad_start).reshape(-1)
    xs = _scatter_rows(hp, dest, n_rows)
    ys = _expert_ffn(xs, schedule, w_gu_all, w_dn_all, layer)
    return _gather_rows(ys, dest), gate


def _final_kernel(xm_ref, y0_ref, y1_ref, g_ref, nfin_ref, *out_refs):
    out_refs[-1][...] = _rms(_combined(xm_ref, y0_ref, y1_ref, g_ref), nfin_ref[...])


def _final(xm, yg, gates, norm_final, out, group, n_groups):
    t, d = xm.shape
    ts = min(TS_FINAL, t)
    n_tiles = t // ts
    tile = pl.BlockSpec((ts, d), lambda i: (i, 0))
    in_specs = [tile, pl.BlockSpec((ts, d // 2), lambda i: (i, 0)),
                pl.BlockSpec((ts, d // 2), lambda i: (n_tiles + i, 0)),
                pl.BlockSpec((ts, SUBLANES), lambda i: (i, 0)), pl.BlockSpec((1, d), lambda i: (0, 0))]
    operands = [xm, yg, yg, gates, norm_final.reshape(1, d)]
    aliases = {}
    if out is not None:
        in_specs.append(pl.BlockSpec(memory_space=pl.ANY))
        operands.append(out)
        aliases = {len(operands) - 1: 0}
    return pl.pallas_call(
        _final_kernel,
        grid=(n_tiles,),
        in_specs=in_specs,
        out_specs=pl.BlockSpec((ts, d), lambda i: (group * n_tiles + i, 0)),
        out_shape=jax.ShapeDtypeStruct((n_groups * t, d), F32),
        input_output_aliases=aliases,
        compiler_params=pltpu.CompilerParams(dimension_semantics=("arbitrary",),
                                             vmem_limit_bytes=VMEM_LIMIT_BYTES),
        name="final_combine_norm",
    )(*operands)


def kernel(x, norm_mix, norm_ffn, norm_final, pool_w, pool_scale, sconv_in_w, sconv_taps, sconv_out_w, ssm_in_w, ssm_conv_w, ssm_conv_b, ssm_dt_bias, ssm_a_log, ssm_d, ssm_norm_w, ssm_out_w, router_group_w, router_group_b, router_expert_w, router_expert_b, expert_w_gu, expert_w_down):
    bsz, seq, d = x.shape
    depth = norm_mix.shape[0]
    assert (router_group_w.shape[2], router_expert_w.shape[2]) == ROUTER_GROUPS
    n_groups = TOKEN_GROUPS if bsz % TOKEN_GROUPS == 0 else 1
    gb = bsz // n_groups
    x2d = x.reshape(bsz * seq, d)
    src = [None] * n_groups
    prev = [None] * n_groups
    for i in range(depth):
        kind, j = i % N_MIXERS, i // N_MIXERS
        ts = min((TS_POOL, TS_SCONV, TS_SSM)[kind], seq)
        router = _router_operands(router_group_w[i], router_group_b[i], router_expert_w[i],
                                  router_expert_b[i], ts)
        for g in range(n_groups):
            s_in = Stream(x2d, g * gb * (seq // ts)) if i == 0 else src[g]
            if kind == 0:
                outs = _pool_layer(prev[g], s_in, gb, seq, ts, norm_mix[i], pool_w[j], pool_scale[j],
                                   norm_ffn[i], router)
            elif kind == 1:
                outs = _sconv_layer(prev[g], s_in, gb, seq, ts, norm_mix[i], sconv_in_w[j], sconv_taps[j],
                                    sconv_out_w[j], norm_ffn[i], router)
            else:
                outs = _ssm_layer(prev[g], s_in, gb, seq, ts, norm_mix[i], ssm_in_w[j], ssm_conv_w[j],
                                  ssm_conv_b[j], ssm_dt_bias[j], ssm_a_log[j], ssm_d[j], ssm_norm_w[j],
                                  ssm_out_w[j], norm_ffn[i], router)
            xm, hp, ids, gate, counts = outs
            src[g] = Stream(xm, 0)
            prev[g] = _moe(hp, ids, gate, counts, expert_w_gu, expert_w_down, i)
    out = None
    for g in range(n_groups):
        out = _final(src[g].rows, prev[g][0], prev[g][1], norm_final, out, g, n_groups)
    return out.reshape(bsz, seq, d)
```

```python
import functools
from typing import NamedTuple

import jax
import jax.numpy as jnp
from jax import lax
from jax.experimental import pallas as pl
from jax.experimental.pallas import tpu as pltpu
from jax.experimental.pallas import tpu_sc as plsc

F32 = jnp.float32
BF16 = jnp.bfloat16
I32 = jnp.int32

NORM_EPS = 1e-6
N_MIXERS = 3
POOL_WINDOWS = (2, 4, 8, 16)
POOL_HALO = 16
CONV_HALO = 8
SSM_GROUPS = 4
SSM_CHUNK = 128
SSM_CONV_WIDTH = 4
MOE_TOP_K = 2
TOKEN_GROUPS = 2
EXPERTS_PER_GROUP = 8
ROUTER_GROUPS = (4, 32)
ROUTER_ROWS = 48

SUBLANES = 8
LANES = 128
VMEM_LIMIT_BYTES = 56 * 1024 * 1024
SC_CORES = 2
SC_WORKERS = SC_CORES * 16
SC_WINDOW = 64

TS_POOL = 512
TS_SCONV = 512
TS_SSM = 512
TS_FINAL = 1024
TS_DEST = 4096
FFN_BLOCK = 512

NT_DIMS = (((1,), (1,)), ((), ()))
TN_DIMS = (((0,), (0,)), ((), ()))


def _dot(a, b):
    return jnp.dot(a, b, preferred_element_type=F32)


def _dot_nt(a, b):
    return lax.dot_general(a, b, NT_DIMS, preferred_element_type=F32)


def _split3(a):
    a1 = a.astype(BF16)
    r1 = a - a1.astype(F32)
    a2 = r1.astype(BF16)
    a3 = (r1 - a2.astype(F32)).astype(BF16)
    return a1, a2, a3


def _dot3(a, b):
    a1, a2, a3 = _split3(a)
    return _dot(a1, b) + _dot(a2, b) + _dot(a3, b)


def _hilo(a):
    hi = a.astype(BF16)
    lo = (a - hi.astype(F32)).astype(BF16)
    return hi, lo


def _dot_nt_hilo(w_hi, w_lo, a_hi, a_lo):
    return _dot_nt(w_hi, a_hi) + _dot_nt(w_hi, a_lo) + _dot_nt(w_lo, a_hi)


def _rms(x, w):
    return x * lax.rsqrt(jnp.mean(x * x, axis=-1, keepdims=True) + NORM_EPS) * w


def _silu(x):
    half = 0.5 * x
    return half + half * jnp.tanh(half)


def _shift_rows(ext, k, halo):
    return pltpu.roll(ext, k, axis=0)[halo:]


HI_HALF_MASK = -65536


def _pack_bf16(a_bf):
    half = a_bf.shape[1] // 2
    lo = lax.bitcast_convert_type(a_bf[:, :half].astype(F32), I32)
    hi = lax.bitcast_convert_type(a_bf[:, half:].astype(F32), I32)
    return lax.shift_right_logical(lo, 16) | (hi & HI_HALF_MASK)


def _unpack_bf16(w):
    lo = lax.bitcast_convert_type(lax.shift_left(w, 16), F32)
    hi = lax.bitcast_convert_type(w & HI_HALF_MASK, F32)
    return jnp.concatenate([lo, hi], axis=1)


def _combined(xm_ref, y0_ref, y1_ref, g_ref):
    g = g_ref[...]
    return xm_ref[...] + g[:, 0:1] * _unpack_bf16(y0_ref[...]) + g[:, 1:2] * _unpack_bf16(y1_ref[...])


def _route(xm, first, nffn_ref, wr_hi_ref, wr_lo_ref, br_ref, tri_ref, base_ref, hp_ref, ids_ref, gate_ref,
           cnt_ref):
    n_groups, n_experts = ROUTER_GROUPS
    ts = xm.shape[0]

    @pl.when(first)
    def _():
        base_ref[...] = jnp.zeros_like(base_ref)

    h2 = _rms(xm, nffn_ref[...])
    h_hi, h_lo = _hilo(h2)
    hp_ref[...] = _pack_bf16(h_hi)
    lt = _dot_nt_hilo(wr_hi_ref[...], wr_lo_ref[...], h_hi, h_lo) + br_ref[...]

    row8 = lax.broadcasted_iota(I32, (SUBLANES, ts), 0)
    neg_inf = jnp.float32(-jnp.inf)
    g = jnp.where(row8 < n_groups, lt[0:SUBLANES], neg_inf)
    gmax = jnp.max(g, axis=0, keepdims=True)
    gsel = jnp.min(jnp.where(g == gmax, row8, SUBLANES), axis=0, keepdims=True)
    pgrp = 1.0 / jnp.sum(jnp.exp(g - gmax), axis=0, keepdims=True)

    sel = lt[SUBLANES:2 * SUBLANES]
    for j in range(1, n_groups):
        sel = jnp.where(gsel == j, lt[(j + 1) * SUBLANES:(j + 2) * SUBLANES], sel)
    m1 = jnp.max(sel, axis=0, keepdims=True)
    i1 = jnp.min(jnp.where(sel == m1, row8, SUBLANES), axis=0, keepdims=True)
    sel2 = jnp.where(row8 == i1, neg_inf, sel)
    m2 = jnp.max(sel2, axis=0, keepdims=True)
    i2 = jnp.min(jnp.where(sel2 == m2, row8, SUBLANES), axis=0, keepdims=True)
    r = jnp.exp(m2 - m1)
    den = 1.0 + r
    g0 = pgrp / den
    g1 = pgrp * r / den
    e0 = gsel * EXPERTS_PER_GROUP + i1
    e1 = gsel * EXPERTS_PER_GROUP + i2

    row_e = lax.broadcasted_iota(I32, (n_experts, ts), 0)
    oh0 = row_e == e0
    oh1 = row_e == e1
    oh = jnp.where(oh0 | oh1, 1.0, 0.0).astype(F32)
    prefix = _dot(oh.astype(BF16), tri_ref[...])
    tot = prefix + base_ref[:, 0:1]
    rank0 = jnp.sum(jnp.where(oh0, tot, 0.0), axis=0, keepdims=True).astype(I32)
    rank1 = jnp.sum(jnp.where(oh1, tot, 0.0), axis=0, keepdims=True).astype(I32)
    base_ref[...] = base_ref[...] + jnp.sum(oh, axis=1, keepdims=True)

    ids_ref[...] = jnp.where(row8 == 0, e0, jnp.where(row8 == 1, e1, jnp.where(row8 == 2, rank0, rank1)))
    eye = (lax.broadcasted_iota(I32, (SUBLANES, SUBLANES), 0)
           == lax.broadcasted_iota(I32, (SUBLANES, SUBLANES), 1)).astype(BF16)
    gate_ref[...] = sum(lax.dot_general(part, eye, TN_DIMS, preferred_element_type=F32)
                        for part in _split3(jnp.where(row8 == 0, g0, g1)))
    cnt_ref[...] = base_ref[...].astype(I32)


def _router_operands(w_rg, b_rg, w_re, b_re, ts):
    d, n_groups = w_rg.shape
    n_experts = w_re.shape[1]
    assert n_experts == n_groups * EXPERTS_PER_GROUP and n_groups <= SUBLANES
    assert SUBLANES + n_experts <= ROUTER_ROWS
    wt = jnp.zeros((ROUTER_ROWS, d), F32)
    wt = wt.at[0:n_groups].set(w_rg.T).at[SUBLANES:SUBLANES + n_experts].set(w_re.T)
    bt = jnp.zeros((ROUTER_ROWS, 1), F32)
    bt = bt.at[0:n_groups, 0].set(b_rg).at[SUBLANES:SUBLANES + n_experts, 0].set(b_re)
    w_hi = wt.astype(BF16)
    w_lo = (wt - w_hi.astype(F32)).astype(BF16)
    idx = jnp.arange(ts)
    tri = (idx[:, None] < idx[None, :]).astype(BF16)
    return w_hi, w_lo, bt, tri


def _const_spec(shape):
    zeros = (0,) * len(shape)
    return pl.BlockSpec(shape, lambda b, s: zeros, pipeline_mode=pl.Buffered(1))


class Stream(NamedTuple):
    rows: jax.Array
    tile0: int


def _mixer_in(prev, src, t, ts, n_s, d):
    n_tiles = t // ts
    tile0 = src.tile0
    tile = pl.BlockSpec((ts, d), lambda b, s: (tile0 + b * n_s + s, 0))
    if prev is None:
        return [src.rows], [tile]
    yg, gates = prev
    specs = [
        tile,
        pl.BlockSpec((ts, d // 2), lambda b, s: (b * n_s + s, 0)),
        pl.BlockSpec((ts, d // 2), lambda b, s: (n_tiles + b * n_s + s, 0)),
        pl.BlockSpec((ts, SUBLANES), lambda b, s: (b * n_s + s, 0)),
    ]
    return [src.rows, yg, yg, gates], specs


def _router_in(nffn, router, d, ts):
    w_hi, w_lo, bt, tri = router
    specs = [
        _const_spec((1, d)),
        _const_spec((ROUTER_ROWS, d)),
        _const_spec((ROUTER_ROWS, d)),
        _const_spec((ROUTER_ROWS, 1)),
        _const_spec((ts, ts)),
    ]
    return [nffn.reshape(1, d), w_hi, w_lo, bt, tri], specs


def _mixer_out(t, d, ts, n_s):
    n_tiles = t // ts
    n_experts = ROUTER_GROUPS[1]
    out_shape = [
        jax.ShapeDtypeStruct((t, d), F32),
        jax.ShapeDtypeStruct((t, d // 2), I32),
        jax.ShapeDtypeStruct((SUBLANES, t), I32),
        jax.ShapeDtypeStruct((t, SUBLANES), F32),
        jax.ShapeDtypeStruct((n_experts, LANES), I32),
    ]
    out_specs = [
        pl.BlockSpec((ts, d), lambda b, s: (b * n_s + s, 0)),
        pl.BlockSpec((ts, d // 2), lambda b, s: (b * n_s + s, 0)),
        pl.BlockSpec((SUBLANES, ts), lambda b, s: (0, b * n_s + s)),
        pl.BlockSpec((ts, SUBLANES), lambda b, s: (b * n_s + s, 0)),
        pl.BlockSpec((n_experts, LANES), lambda b, s: (0, 0)),
    ]
    return out_shape, out_specs


def _mixer_call(kern, name, prev, src, bsz, seq, ts, mixer_ops, mixer_specs, nffn, router, scratch):
    t, d = bsz * seq, src.rows.shape[1]
    n_s = seq // ts
    tok_ops, tok_specs = _mixer_in(prev, src, t, ts, n_s, d)
    r_ops, r_specs = _router_in(nffn, router, d, ts)
    out_shape, out_specs = _mixer_out(t, d, ts, n_s)
    return pl.pallas_call(
        functools.partial(kern, n_tok=len(tok_ops)),
        grid=(bsz, n_s),
        in_specs=tok_specs + mixer_specs + r_specs,
        out_specs=out_specs,
        out_shape=out_shape,
        scratch_shapes=scratch + [pltpu.VMEM((ROUTER_GROUPS[1], LANES), F32)],
        compiler_params=pltpu.CompilerParams(dimension_semantics=("arbitrary", "arbitrary"),
                                             vmem_limit_bytes=VMEM_LIMIT_BYTES),
        name=name,
    )(*tok_ops, *mixer_ops, *r_ops)


def _token_input(refs, n_tok):
    return refs[0][...] if n_tok == 1 else _combined(*refs[:n_tok])


def _pool_kernel(*refs, n_tok):
    x = _token_input(refs, n_tok)
    (nmix_ref, pw_ref, scale_ref,
     nffn_ref, wr_hi_ref, wr_lo_ref, br_ref, tri_ref,
     xm_ref, hp_ref, ids_ref, gate_ref, cnt_ref,
     carry_ref, base_ref) = refs[n_tok:]
    b = pl.program_id(0)
    s = pl.program_id(1)
    ts, d = x.shape
    ch = d // len(POOL_WINDOWS)

    @pl.when(s == 0)
    def _():
        carry_ref[...] = jnp.zeros_like(carry_ref)

    h = _rms(x, nmix_ref[...])
    ext = jnp.concatenate([carry_ref[...], h], axis=0)
    carry_ref[...] = h[ts - POOL_HALO:]

    pos = s * ts + lax.broadcasted_iota(I32, (ts, 1), 0) + 1
    ys = []
    for g, win in enumerate(POOL_WINDOWS):
        acc = ext[:, g * ch:(g + 1) * ch]
        k = 1
        while k < win:
            acc = acc + pltpu.roll(acc, k, axis=0)
            k *= 2
        inv_cnt = 1.0 / jnp.minimum(pos, win).astype(F32)
        pooled = acc[POOL_HALO:] * inv_cnt - h[:, g * ch:(g + 1) * ch]
        ys.append(_dot(pooled.astype(BF16), pw_ref[g]))
    y = jnp.concatenate(ys, axis=1) * scale_ref[...]
    xm = x + y
    xm_ref[...] = xm
    _route(xm, (b == 0) & (s == 0), nffn_ref, wr_hi_ref, wr_lo_ref, br_ref, tri_ref, base_ref,
           hp_ref, ids_ref, gate_ref, cnt_ref)


def _pool_layer(prev, src, bsz, seq, ts, nmix, pool_w, pool_scale, nffn, router):
    d = src.rows.shape[1]
    n_win, ch, _ = pool_w.shape
    assert n_win == len(POOL_WINDOWS) and ch * n_win == d
    ops = [nmix.reshape(1, d), pool_w.astype(BF16), pool_scale.reshape(1, d)]
    specs = [_const_spec((1, d)), _const_spec((n_win, ch, ch)), _const_spec((1, d))]
    return _mixer_call(_pool_kernel, "pool_mixer", prev, src, bsz, seq, ts, ops, specs, nffn, router,
                       [pltpu.VMEM((POOL_HALO, d), F32)])


def _sconv_kernel(*refs, n_tok):
    x = _token_input(refs, n_tok)
    (nmix_ref, win_ref, taps_ref, wout_ref,
     nffn_ref, wr_hi_ref, wr_lo_ref, br_ref, tri_ref,
     xm_ref, hp_ref, ids_ref, gate_ref, cnt_ref,
     carry_ref, base_ref) = refs[n_tok:]
    b = pl.program_id(0)
    s = pl.program_id(1)
    ts, d = x.shape

    @pl.when(s == 0)
    def _():
        carry_ref[...] = jnp.zeros_like(carry_ref)

    h = _rms(x, nmix_ref[...]).astype(BF16)
    b_gate = _dot(h, win_ref[:, 0:d])
    c_gate = _dot(h, win_ref[:, d:2 * d])
    v = _dot(h, win_ref[:, 2 * d:3 * d])
    u_pre = c_gate * v
    ext = jnp.concatenate([carry_ref[...], u_pre], axis=0)
    carry_ref[...] = u_pre[ts - CONV_HALO:]
    width = taps_ref.shape[0]
    u = taps_ref[width - 1:width, :] * u_pre
    for k in range(1, width):
        u = u + taps_ref[width - 1 - k:width - k, :] * _shift_rows(ext, k, CONV_HALO)
    mix = _dot((b_gate * u).astype(BF16), wout_ref[...])
    xm = x + mix
    xm_ref[...] = xm
    _route(xm, (b == 0) & (s == 0), nffn_ref, wr_hi_ref, wr_lo_ref, br_ref, tri_ref, base_ref,
           hp_ref, ids_ref, gate_ref, cnt_ref)


def _sconv_layer(prev, src, bsz, seq, ts, nmix, w_in, taps, w_out, nffn, router):
    d = src.rows.shape[1]
    width = taps.shape[0]
    assert width - 1 <= CONV_HALO
    ops = [nmix.reshape(1, d), w_in.astype(BF16), taps, w_out.astype(BF16)]
    specs = [_const_spec((1, d)), _const_spec((d, 3 * d)), _const_spec((width, d)), _const_spec((d, d))]
    return _mixer_call(_sconv_kernel, "sconv_mixer", prev, src, bsz, seq, ts, ops, specs, nffn, router,
                       [pltpu.VMEM((CONV_HALO, d), F32)])


def _ssm_kernel(*refs, n_tok):
    x = _token_input(refs, n_tok)
    (nmix_ref, wz_ref, wxbc_ref, wdt_hi_ref, wdt_lo_ref, convw_ref, convb_ref,
     dtb_ref, alog_ref, expand_ref, dskip_ref, normw_ref, wout_ref, triq_ref,
     nffn_ref, wr_hi_ref, wr_lo_ref, br_ref, tri_ref,
     xm_ref, hp_ref, ids_ref, gate_ref, cnt_ref,
     carry_ref, z_ref, xs_ref, bm_ref, cm_ref, y_ref, dtt_ref, at_ref, state_ref, base_ref) = refs[n_tok:]
    b = pl.program_id(0)
    s = pl.program_id(1)
    ts, d = x.shape
    di = z_ref.shape[1]
    gn = bm_ref.shape[1]
    cd = di + 2 * gn
    nh = alog_ref.shape[0]
    hp = di // nh
    grp = SSM_GROUPS
    st = gn // grp
    hpg = nh // grp
    gw = hpg * hp
    q = SSM_CHUNK
    n_chunks = ts // q

    @pl.when(s == 0)
    def _():
        carry_ref[...] = jnp.zeros_like(carry_ref)
        state_ref[...] = jnp.zeros_like(state_ref)

    h = _rms(x, nmix_ref[...])
    h_hi, h_lo = _hilo(h)

    col_blk = gn

    def project(c0):
        cols = slice(c0, c0 + col_blk)
        if c0 < di:
            z_ref[:, cols] = _dot(h_hi, wz_ref[:, cols])
        return _dot(h_hi, wxbc_ref[:, cols])

    pre_next = project(0)
    for c0 in range(0, cd, col_blk):
        cols = slice(c0, c0 + col_blk)
        pre = pre_next
        if c0 + col_blk < cd:
            pre_next = project(c0 + col_blk)
        ext = jnp.concatenate([carry_ref[:, cols], pre], axis=0)
        carry_ref[:, cols] = pre[ts - CONV_HALO:]
        ext1 = pltpu.roll(ext, 1, axis=0)
        pair = convw_ref[1:2, cols] * ext + convw_ref[0:1, cols] * ext1
        acc = (convw_ref[3:4, cols] * pre + convb_ref[:, cols] + convw_ref[2:3, cols] * ext1[CONV_HALO:]
               + _shift_rows(pair, 2, CONV_HALO))
        act = _silu(acc)
        if c0 < di:
            xs_ref[:, cols] = act
        elif c0 < di + gn:
            bm_ref[...] = act
        else:
            cm_ref[...] = act

    dt_t = _dot_nt_hilo(wdt_hi_ref[...], wdt_lo_ref[...], h_hi, h_lo) + dtb_ref[...]
    dt_t = jnp.maximum(dt_t, 0.0) + jnp.log(1.0 + jnp.exp(-jnp.abs(dt_t)))
    a_t = dt_t * (-jnp.exp(alog_ref[...]))
    for c in range(n_chunks):
        dtt_ref[c] = dt_t[:, c * q:(c + 1) * q]
        at_ref[c] = a_t[:, c * q:(c + 1) * q]

    row_q = lax.broadcasted_iota(I32, (q, q), 0)
    col_q = lax.broadcasted_iota(I32, (q, q), 1)
    causal = row_q >= col_q
    lane_lo = lax.broadcasted_iota(I32, (q, 2 * hp), 1) < hp
    neg_inf = jnp.float32(-jnp.inf)
    expand = expand_ref[...]

    def chunk_body(c, carry):
        r0 = pl.multiple_of(c * q, q)
        rows = pl.ds(r0, q)
        acs_t = _dot3(at_ref[c], triq_ref[...])
        dt_c = dtt_ref[c]
        w_t = dt_c * jnp.exp(acs_t[:, q - 1:q] - acs_t)
        src_t = acs_t - jnp.log(dt_c)
        acs = acs_t.T
        od_hi, od_lo = _hilo(jnp.exp(acs))
        out_decay_e = _dot(od_hi, expand) + _dot(od_lo, expand)
        w_e = _dot(w_t.T.astype(BF16), expand)
        xc = xs_ref[rows, :]
        xw = xc * w_e
        bc = bm_ref[rows, :]
        cc = cm_ref[rows, :]
        for g in range(grp):
            gcols = slice(g * gw, (g + 1) * gw)
            bg = bc[:, g * st:(g + 1) * st].astype(BF16)
            cg = cc[:, g * st:(g + 1) * st].astype(BF16)
            xg = xc[:, gcols]
            cb = _dot_nt(cg, bg)
            hprev = state_ref[g]
            y_off = _dot(cg, hprev.astype(BF16)) * out_decay_e[:, gcols]
            new_states = lax.dot_general(bg, xw[:, gcols].astype(BF16), TN_DIMS, preferred_element_type=F32)
            state_ref[g] = hprev * out_decay_e[q - 1:q, gcols] + new_states
            for j in range(hpg // 2):
                h0 = g * hpg + 2 * j
                ms = []
                for hh in (h0, h0 + 1):
                    seg = acs[:, hh:hh + 1] - src_t[hh:hh + 1, :]
                    ms.append((cb * jnp.exp(jnp.where(causal, seg, neg_inf))).astype(BF16))
                lhs = jnp.concatenate(ms, axis=1)
                xp = xg[:, 2 * j * hp:(2 * j + 2) * hp]
                rhs = jnp.concatenate([jnp.where(lane_lo, xp, 0.0), jnp.where(lane_lo, 0.0, xp)],
                                      axis=0).astype(BF16)
                pc = slice(g * gw + 2 * j * hp, g * gw + (2 * j + 2) * hp)
                y_ref[rows, pc] = _dot(lhs, rhs) + y_off[:, 2 * j * hp:(2 * j + 2) * hp]
        return carry

    lax.fori_loop(0, n_chunks, chunk_body, 0)

    out = None
    for g in range(grp):
        gcols = slice(g * gw, (g + 1) * gw)
        yg = (y_ref[:, gcols] + xs_ref[:, gcols] * dskip_ref[:, gcols]) * _silu(z_ref[:, gcols])
        yn = yg * lax.rsqrt(jnp.mean(yg * yg, axis=-1, keepdims=True) + NORM_EPS) * normw_ref[:, gcols]
        part = _dot(yn.astype(BF16), wout_ref[gcols, :])
        out = part if out is None else out + part
    xm = x + out
    xm_ref[...] = xm
    _route(xm, (b == 0) & (s == 0), nffn_ref, wr_hi_ref, wr_lo_ref, br_ref, tri_ref, base_ref,
           hp_ref, ids_ref, gate_ref, cnt_ref)


def _ssm_layer(prev, src, bsz, seq, ts, nmix, w_in, conv_w, conv_b, dt_bias, a_log, d_skip, norm_w, w_out,
               nffn, router):
    d = src.rows.shape[1]
    assert ts % SSM_CHUNK == 0
    nh = a_log.shape[0]
    di = norm_w.shape[0]
    cd = conv_w.shape[1]
    gn = (cd - di) // 2
    hp = di // nh
    width = conv_w.shape[0]
    assert width == SSM_CONV_WIDTH and (nh // SSM_GROUPS) % 2 == 0 and di % gn == 0
    st = gn // SSM_GROUPS
    gw = di // SSM_GROUPS
    q = SSM_CHUNK
    n_chunks = ts // q

    wz = w_in[:, :di].astype(BF16)
    wxbc = w_in[:, di:di + cd].astype(BF16)
    wdt_t = w_in[:, di + cd:].T
    wdt_hi = wdt_t.astype(BF16)
    wdt_lo = (wdt_t - wdt_hi.astype(F32)).astype(BF16)
    expand = (jnp.arange(di)[None, :] // hp == jnp.arange(nh)[:, None]).astype(BF16)
    dskip_e = jnp.repeat(d_skip, hp).reshape(1, di)
    idx = jnp.arange(q)
    triq = (idx[:, None] <= idx[None, :]).astype(BF16)

    ops = [nmix.reshape(1, d), wz, wxbc, wdt_hi, wdt_lo, conv_w, conv_b.reshape(1, cd),
           dt_bias.reshape(nh, 1), a_log.reshape(nh, 1), expand, dskip_e, norm_w.reshape(1, di),
           w_out.astype(BF16), triq]
    specs = [
        _const_spec((1, d)),
        _const_spec((d, di)),
        _const_spec((d, cd)),
        _const_spec((nh, d)),
        _const_spec((nh, d)),
        _const_spec((width, cd)),
        _const_spec((1, cd)),
        _const_spec((nh, 1)),
        _const_spec((nh, 1)),
        _const_spec((nh, di)),
        _const_spec((1, di)),
        _const_spec((1, di)),
        _const_spec((di, d)),
        _const_spec((q, q)),
    ]
    scratch = [
        pltpu.VMEM((CONV_HALO, cd), F32),
        pltpu.VMEM((ts, di), F32),
        pltpu.VMEM((ts, di), F32),
        pltpu.VMEM((ts, gn), F32),
        pltpu.VMEM((ts, gn), F32),
        pltpu.VMEM((ts, di), F32),
        pltpu.VMEM((n_chunks, nh, q), F32),
        pltpu.VMEM((n_chunks, nh, q), F32),
        pltpu.VMEM((SSM_GROUPS, st, gw), F32),
    ]
    return _mixer_call(_ssm_kernel, "ssm_mixer", prev, src, bsz, seq, ts, ops, specs, nffn, router, scratch)


def _sc_mesh():
    return plsc.VectorSubcoreMesh(core_axis_name="core", subcore_axis_name="subcore")


def _sc_worker_base(per_worker):
    return (lax.axis_index("subcore") * SC_CORES + lax.axis_index("core")) * per_worker


def _sc_scratch(d, dtype):
    win = SC_WINDOW
    return [pltpu.VMEM((win,), I32), pltpu.VMEM((win,), I32),
            pltpu.VMEM((win, d), dtype), pltpu.VMEM((win, d), dtype),
            pltpu.SemaphoreType.DMA, pltpu.SemaphoreType.DMA]


def _scatter_rows(src, idx, n_rows):
    n_src, d = src.shape
    win = SC_WINDOW
    per_worker = n_src // SC_WORKERS
    n_win = per_worker // win
    assert idx.shape[0] == MOE_TOP_K * n_src
    assert n_src == per_worker * SC_WORKERS and per_worker == n_win * win and n_win % 2 == 0
    scratch = ([pltpu.VMEM((win,), I32)] * (2 * MOE_TOP_K) + [pltpu.VMEM((win, d), src.dtype)] * 2
               + [pltpu.SemaphoreType.DMA] * (2 * MOE_TOP_K))

    @functools.partial(pl.kernel, out_type=jax.ShapeDtypeStruct((n_rows, d), src.dtype), mesh=_sc_mesh(),
                       scratch_types=scratch, name="moe_scatter_rows")
    def scatter_kernel(x_hbm, i_hbm, o_hbm, idx_a0, idx_a1, idx_b0, idx_b1, rows_a, rows_b,
                       sem_a0, sem_a1, sem_b0, sem_b1):
        base = _sc_worker_base(per_worker)

        def step(w, idx_vs, rows_v, sems, first):
            if not first:
                for idx_v, sem in zip(idx_vs, sems):
                    pltpu.make_async_copy(rows_v, o_hbm.at[idx_v], sem).wait()
            for k, idx_v in enumerate(idx_vs):
                pltpu.sync_copy(i_hbm.at[pl.ds(k * n_src + base + w * win, win)], idx_v)
            pltpu.sync_copy(x_hbm.at[pl.ds(base + w * win, win)], rows_v)
            for idx_v, sem in zip(idx_vs, sems):
                pltpu.make_async_copy(rows_v, o_hbm.at[idx_v], sem).start()

        buf_a = ((idx_a0, idx_a1), rows_a, (sem_a0, sem_a1))
        buf_b = ((idx_b0, idx_b1), rows_b, (sem_b0, sem_b1))
        step(0, *buf_a, True)
        step(1, *buf_b, True)

        @pl.loop(2, n_win, step=2)
        def _(w):
            step(w, *buf_a, False)
            step(w + 1, *buf_b, False)

        for idx_vs, rows_v, sems in (buf_a, buf_b):
            for idx_v, sem in zip(idx_vs, sems):
                pltpu.make_async_copy(rows_v, o_hbm.at[idx_v], sem).wait()

    return scatter_kernel(src, idx)


def _gather_rows(table, idx):
    d = table.shape[1]
    m = idx.shape[0]
    win = SC_WINDOW
    per_worker = m // SC_WORKERS
    n_win = per_worker // win
    assert m == per_worker * SC_WORKERS and per_worker == n_win * win and n_win % 2 == 0

    @functools.partial(pl.kernel, out_type=jax.ShapeDtypeStruct((m, d), table.dtype), mesh=_sc_mesh(),
                       scratch_types=_sc_scratch(d, table.dtype), name="moe_gather_rows")
    def gather_kernel(x_hbm, i_hbm, o_hbm, idx_a, idx_b, rows_a, rows_b, sem_a, sem_b):
        base = _sc_worker_base(per_worker)

        def out_copy(w, rows_v, sem):
            return pltpu.make_async_copy(rows_v, o_hbm.at[pl.ds(base + w * win, win)], sem)

        def step(w, idx_v, rows_v, sem, first):
            pltpu.sync_copy(i_hbm.at[pl.ds(base + w * win, win)], idx_v)
            if not first:
                out_copy(w, rows_v, sem).wait()
            pltpu.sync_copy(x_hbm.at[idx_v], rows_v)
            out_copy(w, rows_v, sem).start()

        step(0, idx_a, rows_a, sem_a, True)
        step(1, idx_b, rows_b, sem_b, True)

        @pl.loop(2, n_win, step=2)
        def _(w):
            step(w, idx_a, rows_a, sem_a, False)
            step(w + 1, idx_b, rows_b, sem_b, False)

        out_copy(0, rows_a, sem_a).wait()
        out_copy(0, rows_b, sem_b).wait()

    return gather_kernel(table, idx)


def _ffn_kernel(be_ref, nu_ref, nv_ref, slot_ref, nxt_ref, xs_ref, wgu_hbm, wdn_hbm, ys_ref,
                wgu_f32_ref, wdn_f32_ref, wgu_bf_ref, wdn_bf_ref, sem_ref, *, layer):
    i = pl.program_id(0)
    bm = xs_ref.shape[0]
    f = wdn_bf_ref.shape[0]

    def weight_copies(expert, slot):
        return (pltpu.make_async_copy(wgu_hbm.at[layer, expert], wgu_f32_ref.at[slot], sem_ref.at[slot, 0]),
                pltpu.make_async_copy(wdn_hbm.at[layer, expert], wdn_f32_ref.at[slot], sem_ref.at[slot, 1]))

    @pl.when(i == 0)
    def _():
        for cp in weight_copies(be_ref[0], slot_ref[0]):
            cp.start()

    first_of_expert = (i == 0) | (be_ref[i] != be_ref[jnp.maximum(i - 1, 0)])

    @pl.when(first_of_expert & (i < nu_ref[0]))
    def _():
        slot = slot_ref[i]
        for cp in weight_copies(be_ref[i], slot):
            cp.wait()

        @pl.when(nxt_ref[i] >= 0)
        def _():
            for cp in weight_copies(nxt_ref[i], 1 - slot):
                cp.start()

        wgu_bf_ref[...] = wgu_f32_ref[slot].astype(BF16)
        wdn_bf_ref[...] = wdn_f32_ref[slot].astype(BF16)

    def ffn_rows(n):
        valid = lax.broadcasted_iota(I32, (n, 1), 0) < nv_ref[i]
        h = _unpack_bf16(jnp.where(valid, xs_ref[0:n, :], 0)).astype(BF16)
        gu = _dot(h, wgu_bf_ref[...])
        act = _silu(gu[:, :f]) * gu[:, f:]
        ys_ref[0:n, :] = _pack_bf16(_dot(act.astype(BF16), wdn_bf_ref[...]).astype(BF16))

    used = i < nu_ref[0]
    half = bm // 2

    @pl.when(used & (nv_ref[i] > half))
    def _():
        ffn_rows(bm)

    @pl.when(used & (nv_ref[i] <= half))
    def _():
        ffn_rows(half)
        ys_ref[half:bm, :] = jnp.zeros((bm - half, ys_ref.shape[1]), ys_ref.dtype)

    @pl.when(i >= nu_ref[0])
    def _():
        ys_ref[...] = jnp.zeros_like(ys_ref)


def _block_schedule(cnt, n_blocks, bm):
    n_exp = cnt.shape[0]
    padded = (cnt + bm - 1) // bm * bm
    e_ids = jnp.arange(n_exp, dtype=I32)
    before = e_ids[None, :] < e_ids[:, None]
    pad_start = jnp.sum(jnp.where(before, padded[None, :], 0), axis=1).astype(I32)
    pad_end = pad_start + padded
    nonempty = cnt > 0
    seg_of = jnp.sum(before & nonempty[None, :], axis=1)
    after = (e_ids[None, :] > e_ids[:, None]) & nonempty[None, :]
    next_of = jnp.min(jnp.where(after, e_ids[None, :], n_exp), axis=1)
    next_of = jnp.where(next_of == n_exp, -1, next_of)
    blk_row = jnp.arange(n_blocks, dtype=I32) * bm
    block_e = jnp.minimum(jnp.sum(blk_row[:, None] >= pad_end[None, :], axis=1), n_exp - 1).astype(I32)
    of_block = block_e[:, None] == e_ids[None, :]

    def per_block(v):
        return jnp.sum(jnp.where(of_block, v[None, :], 0), axis=1).astype(I32)

    n_used = (jnp.sum(padded, keepdims=True) // bm).astype(I32)
    n_valid = jnp.clip(per_block(pad_start + cnt) - blk_row, 0, bm).astype(I32)
    return pad_start, (block_e, n_used, n_valid, per_block(seg_of % 2), per_block(next_of))


def _expert_ffn(xs, schedule, w_gu_all, w_dn_all, layer):
    n_rows, dh = xs.shape
    bm = FFN_BLOCK
    n_blocks = n_rows // bm
    d, f = w_gu_all.shape[2], w_dn_all.shape[2]
    assert d == 2 * dh

    def xs_map(i, be, nu, nv, sl, nx):
        return (jnp.minimum(i, jnp.maximum(nu[0] - 1, 0)), 0)

    return pl.pallas_call(
        functools.partial(_ffn_kernel, layer=layer),
        grid_spec=pltpu.PrefetchScalarGridSpec(
            num_scalar_prefetch=5,
            grid=(n_blocks,),
            in_specs=[
                pl.BlockSpec((bm, dh), xs_map),
                pl.BlockSpec(memory_space=pl.ANY),
                pl.BlockSpec(memory_space=pl.ANY),
            ],
            out_specs=pl.BlockSpec((bm, dh), lambda i, be, nu, nv, sl, nx: (i, 0)),
            scratch_shapes=[
                pltpu.VMEM((2, d, 2 * f), F32),
                pltpu.VMEM((2, f, d), F32),
                pltpu.VMEM((d, 2 * f), BF16),
                pltpu.VMEM((f, d), BF16),
                pltpu.SemaphoreType.DMA((2, 2)),
            ],
        ),
        out_shape=jax.ShapeDtypeStruct((n_rows, dh), I32),
        compiler_params=pltpu.CompilerParams(dimension_semantics=("arbitrary",),
                                             vmem_limit_bytes=VMEM_LIMIT_BYTES),
        name="expert_ffn",
    )(*schedule, xs, w_gu_all, w_dn_all)


def _dest_kernel(pad_start_ref, ids_ref, dest_ref):
    expert = ids_ref[0:MOE_TOP_K, :]
    dest = ids_ref[MOE_TOP_K:2 * MOE_TOP_K, :]
    for e in range(pad_start_ref.shape[0]):
        dest = dest + jnp.where(expert == e, pad_start_ref[e], 0)
    dest_ref[...] = dest


def _assignment_rows(ids, pad_start):
    t = ids.shape[1]
    tn = min(TS_DEST, t)
    assert t % tn == 0
    return pl.pallas_call(
        _dest_kernel,
        grid_spec=pltpu.PrefetchScalarGridSpec(
            num_scalar_prefetch=1,
            grid=(t // tn,),
            in_specs=[pl.BlockSpec((SUBLANES, tn), lambda i, ps: (0, i))],
            out_specs=pl.BlockSpec((MOE_TOP_K, tn), lambda i, ps: (0, i)),
        ),
        out_shape=jax.ShapeDtypeStruct((MOE_TOP_K, t), I32),
        compiler_params=pltpu.CompilerParams(dimension_semantics=("arbitrary",)),
        name="moe_dest",
    )(pad_start, ids)


def _moe(hp, ids, gate, counts, w_gu_all, w_dn_all, layer):
    t = hp.shape[0]
    n_exp = w_gu_all.shape[1]
    bm = FFN_BLOCK
    n_assign = t * MOE_TOP_K
    n_rows = (n_assign + bm - 1) // bm * bm + n_exp * bm
    n_blocks = n_rows // bm

    pad_start, schedule = _block_schedule(counts[:, 0], n_blocks, bm)
    dest = _assignment_rows(ids, pad_start).reshape(-1)
    xs = _scatter_rows(hp, dest, n_rows)
    ys = _expert_ffn(xs, schedule, w_gu_all, w_dn_all, layer)
    return _gather_rows(ys, dest), gate


def _final_kernel(xm_ref, y0_ref, y1_ref, g_ref, nfin_ref, *out_refs):
    out_refs[-1][...] = _rms(_combined(xm_ref, y0_ref, y1_ref, g_ref), nfin_ref[...])


def _final(xm, yg, gates, norm_final, out, group, n_groups):
    t, d = xm.shape
    ts = min(TS_FINAL, t)
    n_tiles = t // ts
    tile = pl.BlockSpec((ts, d), lambda i: (i, 0))
    in_specs = [tile, pl.BlockSpec((ts, d // 2), lambda i: (i, 0)),
                pl.BlockSpec((ts, d // 2), lambda i: (n_tiles + i, 0)),
                pl.BlockSpec((ts, SUBLANES), lambda i: (i, 0)), pl.BlockSpec((1, d), lambda i: (0, 0))]
    operands = [xm, yg, yg, gates, norm_final.reshape(1, d)]
    aliases = {}
    if out is not None:
        in_specs.append(pl.BlockSpec(memory_space=pl.ANY))
        operands.append(out)
        aliases = {len(operands) - 1: 0}
    return pl.pallas_call(
        _final_kernel,
        grid=(n_tiles,),
        in_specs=in_specs,
        out_specs=pl.BlockSpec((ts, d), lambda i: (group * n_tiles + i, 0)),
        out_shape=jax.ShapeDtypeStruct((n_groups * t, d), F32),
        input_output_aliases=aliases,
        compiler_params=pltpu.CompilerParams(dimension_semantics=("arbitrary",),
                                             vmem_limit_bytes=VMEM_LIMIT_BYTES),
        name="final_combine_norm",
    )(*operands)


def kernel(x, norm_mix, norm_ffn, norm_final, pool_w, pool_scale, sconv_in_w, sconv_taps, sconv_out_w, ssm_in_w, ssm_conv_w, ssm_conv_b, ssm_dt_bias, ssm_a_log, ssm_d, ssm_norm_w, ssm_out_w, router_group_w, router_group_b, router_expert_w, router_expert_b, expert_w_gu, expert_w_down):
    bsz, seq, d = x.shape
    depth = norm_mix.shape[0]
    assert (router_group_w.shape[2], router_expert_w.shape[2]) == ROUTER_GROUPS
    n_groups = TOKEN_GROUPS if bsz % TOKEN_GROUPS == 0 else 1
    gb = bsz // n_groups
    x2d = x.reshape(bsz * seq, d)
    src = [None] * n_groups
    prev = [None] * n_groups
    for i in range(depth):
        kind, j = i % N_MIXERS, i // N_MIXERS
        ts = min((TS_POOL, TS_SCONV, TS_SSM)[kind], seq)
        router = _router_operands(router_group_w[i], router_group_b[i], router_expert_w[i],
                                  router_expert_b[i], ts)
        for g in range(n_groups):
            s_in = Stream(x2d, g * gb * (seq // ts)) if i == 0 else src[g]
            if kind == 0:
                outs = _pool_layer(prev[g], s_in, gb, seq, ts, norm_mix[i], pool_w[j], pool_scale[j],
                                   norm_ffn[i], router)
            elif kind == 1:
                outs = _sconv_layer(prev[g], s_in, gb, seq, ts, norm_mix[i], sconv_in_w[j], sconv_taps[j],
                                    sconv_out_w[j], norm_ffn[i], router)
            else:
                outs = _ssm_layer(prev[g], s_in, gb, seq, ts, norm_mix[i], ssm_in_w[j], ssm_conv_w[j],
                                  ssm_conv_b[j], ssm_dt_bias[j], ssm_a_log[j], ssm_d[j], ssm_norm_w[j],
                                  ssm_out_w[j], norm_ffn[i], router)
            xm, hp, ids, gate, counts = outs
            src[g] = Stream(xm, 0)
            prev[g] = _moe(hp, ids, gate, counts, expert_w_gu, expert_w_down, i)
    out = None
    for g in range(n_groups):
        out = _final(src[g].rows, prev[g][0], prev[g][1], norm_final, out, g, n_groups)
    return out.reshape(bsz, seq, d)
```

```python
import functools
from typing import NamedTuple

import jax
import jax.numpy as jnp
from jax import lax
from jax.experimental import pallas as pl
from jax.experimental.pallas import tpu as pltpu
from jax.experimental.pallas import tpu_sc as plsc

F32 = jnp.float32
BF16 = jnp.bfloat16
I32 = jnp.int32

NORM_EPS = 1e-6
N_MIXERS = 3
POOL_WINDOWS = (2, 4, 8, 16)
POOL_HALO = 16
CONV_HALO = 8
SSM_GROUPS = 4
SSM_CHUNK = 128
SSM_CONV_WIDTH = 4
MOE_TOP_K = 2
TOKEN_GROUPS = 2
EXPERTS_PER_GROUP = 8
ROUTER_GROUPS = (4, 32)
ROUTER_ROWS = 48

SUBLANES = 8
LANES = 128
VMEM_LIMIT_BYTES = 56 * 1024 * 1024
SC_CORES = 2
SC_WORKERS = SC_CORES * 16
SC_WINDOW = 64

TS_POOL = 512
TS_SCONV = 512
TS_SSM = 512
TS_FINAL = 1024
TS_DEST = 4096
FFN_BLOCK = 768

NT_DIMS = (((1,), (1,)), ((), ()))
TN_DIMS = (((0,), (0,)), ((), ()))


def _dot(a, b):
    return jnp.dot(a, b, preferred_element_type=F32)


def _dot_nt(a, b):
    return lax.dot_general(a, b, NT_DIMS, preferred_element_type=F32)


def _split3(a):
    a1 = a.astype(BF16)
    r1 = a - a1.astype(F32)
    a2 = r1.astype(BF16)
    a3 = (r1 - a2.astype(F32)).astype(BF16)
    return a1, a2, a3


def _dot3(a, b):
    a1, a2, a3 = _split3(a)
    return _dot(a1, b) + _dot(a2, b) + _dot(a3, b)


def _hilo(a):
    hi = a.astype(BF16)
    lo = (a - hi.astype(F32)).astype(BF16)
    return hi, lo


def _dot_nt_hilo(w_hi, w_lo, a_hi, a_lo):
    return _dot_nt(w_hi, a_hi) + _dot_nt(w_hi, a_lo) + _dot_nt(w_lo, a_hi)


def _rms(x, w):
    return x * lax.rsqrt(jnp.mean(x * x, axis=-1, keepdims=True) + NORM_EPS) * w


def _silu(x):
    half = 0.5 * x
    return half + half * jnp.tanh(half)


def _shift_rows(ext, k, halo):
    return pltpu.roll(ext, k, axis=0)[halo:]


HI_HALF_MASK = -65536


def _pack_bf16(a_bf):
    half = a_bf.shape[1] // 2
    lo = lax.bitcast_convert_type(a_bf[:, :half].astype(F32), I32)
    hi = lax.bitcast_convert_type(a_bf[:, half:].astype(F32), I32)
    return lax.shift_right_logical(lo, 16) | (hi & HI_HALF_MASK)


def _unpack_bf16(w):
    lo = lax.bitcast_convert_type(lax.shift_left(w, 16), F32)
    hi = lax.bitcast_convert_type(w & HI_HALF_MASK, F32)
    return jnp.concatenate([lo, hi], axis=1)


def _combined(xm_ref, y0_ref, y1_ref, g_ref):
    g = g_ref[...]
    return xm_ref[...] + g[:, 0:1] * _unpack_bf16(y0_ref[...]) + g[:, 1:2] * _unpack_bf16(y1_ref[...])


def _route(xm, first, nffn_ref, wr_hi_ref, wr_lo_ref, br_ref, tri_ref, base_ref, hp_ref, ids_ref, gate_ref,
           cnt_ref):
    n_groups, n_experts = ROUTER_GROUPS
    ts = xm.shape[0]

    @pl.when(first)
    def _():
        base_ref[...] = jnp.zeros_like(base_ref)

    h2 = _rms(xm, nffn_ref[...])
    h_hi, h_lo = _hilo(h2)
    hp_ref[...] = _pack_bf16(h_hi)
    lt = _dot_nt_hilo(wr_hi_ref[...], wr_lo_ref[...], h_hi, h_lo) + br_ref[...]

    row8 = lax.broadcasted_iota(I32, (SUBLANES, ts), 0)
    neg_inf = jnp.float32(-jnp.inf)
    g = jnp.where(row8 < n_groups, lt[0:SUBLANES], neg_inf)
    gmax = jnp.max(g, axis=0, keepdims=True)
    gsel = jnp.min(jnp.where(g == gmax, row8, SUBLANES), axis=0, keepdims=True)
    pgrp = 1.0 / jnp.sum(jnp.exp(g - gmax), axis=0, keepdims=True)

    sel = lt[SUBLANES:2 * SUBLANES]
    for j in range(1, n_groups):
        sel = jnp.where(gsel == j, lt[(j + 1) * SUBLANES:(j + 2) * SUBLANES], sel)
    m1 = jnp.max(sel, axis=0, keepdims=True)
    i1 = jnp.min(jnp.where(sel == m1, row8, SUBLANES), axis=0, keepdims=True)
    sel2 = jnp.where(row8 == i1, neg_inf, sel)
    m2 = jnp.max(sel2, axis=0, keepdims=True)
    i2 = jnp.min(jnp.where(sel2 == m2, row8, SUBLANES), axis=0, keepdims=True)
    r = jnp.exp(m2 - m1)
    den = 1.0 + r
    g0 = pgrp / den
    g1 = pgrp * r / den
    e0 = gsel * EXPERTS_PER_GROUP + i1
    e1 = gsel * EXPERTS_PER_GROUP + i2

    row_e = lax.broadcasted_iota(I32, (n_experts, ts), 0)
    oh0 = row_e == e0
    oh1 = row_e == e1
    oh = jnp.where(oh0 | oh1, 1.0, 0.0).astype(F32)
    prefix = _dot(oh.astype(BF16), tri_ref[...])
    tot = prefix + base_ref[:, 0:1]
    rank0 = jnp.sum(jnp.where(oh0, tot, 0.0), axis=0, keepdims=True).astype(I32)
    rank1 = jnp.sum(jnp.where(oh1, tot, 0.0), axis=0, keepdims=True).astype(I32)
    base_ref[...] = base_ref[...] + jnp.sum(oh, axis=1, keepdims=True)

    ids_ref[...] = jnp.where(row8 == 0, e0, jnp.where(row8 == 1, e1, jnp.where(row8 == 2, rank0, rank1)))
    eye = (lax.broadcasted_iota(I32, (SUBLANES, SUBLANES), 0)
           == lax.broadcasted_iota(I32, (SUBLANES, SUBLANES), 1)).astype(BF16)
    gate_ref[...] = sum(lax.dot_general(part, eye, TN_DIMS, preferred_element_type=F32)
                        for part in _split3(jnp.where(row8 == 0, g0, g1)))
    cnt_ref[...] = base_ref[...].astype(I32)


def _router_operands(w_rg, b_rg, w_re, b_re, ts):
    d, n_groups = w_rg.shape
    n_experts = w_re.shape[1]
    assert n_experts == n_groups * EXPERTS_PER_GROUP and n_groups <= SUBLANES
    assert SUBLANES + n_experts <= ROUTER_ROWS
    wt = jnp.zeros((ROUTER_ROWS, d), F32)
    wt = wt.at[0:n_groups].set(w_rg.T).at[SUBLANES:SUBLANES + n_experts].set(w_re.T)
    bt = jnp.zeros((ROUTER_ROWS, 1), F32)
    bt = bt.at[0:n_groups, 0].set(b_rg).at[SUBLANES:SUBLANES + n_experts, 0].set(b_re)
    w_hi = wt.astype(BF16)
    w_lo = (wt - w_hi.astype(F32)).astype(BF16)
    idx = jnp.arange(ts)
    tri = (idx[:, None] < idx[None, :]).astype(BF16)
    return w_hi, w_lo, bt, tri


def _const_spec(shape):
    zeros = (0,) * len(shape)
    return pl.BlockSpec(shape, lambda b, s: zeros, pipeline_mode=pl.Buffered(1))


class Stream(NamedTuple):
    rows: jax.Array
    tile0: int


def _mixer_in(prev, src, t, ts, n_s, d):
    n_tiles = t // ts
    tile0 = src.tile0
    tile = pl.BlockSpec((ts, d), lambda b, s: (tile0 + b * n_s + s, 0))
    if prev is None:
        return [src.rows], [tile]
    yg, gates = prev
    specs = [
        tile,
        pl.BlockSpec((ts, d // 2), lambda b, s: (b * n_s + s, 0)),
        pl.BlockSpec((ts, d // 2), lambda b, s: (n_tiles + b * n_s + s, 0)),
        pl.BlockSpec((ts, SUBLANES), lambda b, s: (b * n_s + s, 0)),
    ]
    return [src.rows, yg, yg, gates], specs


def _router_in(nffn, router, d, ts):
    w_hi, w_lo, bt, tri = router
    specs = [
        _const_spec((1, d)),
        _const_spec((ROUTER_ROWS, d)),
        _const_spec((ROUTER_ROWS, d)),
        _const_spec((ROUTER_ROWS, 1)),
        _const_spec((ts, ts)),
    ]
    return [nffn.reshape(1, d), w_hi, w_lo, bt, tri], specs


def _mixer_out(t, d, ts, n_s):
    n_tiles = t // ts
    n_experts = ROUTER_GROUPS[1]
    out_shape = [
        jax.ShapeDtypeStruct((t, d), F32),
        jax.ShapeDtypeStruct((t, d // 2), I32),
        jax.ShapeDtypeStruct((SUBLANES, t), I32),
        jax.ShapeDtypeStruct((t, SUBLANES), F32),
        jax.ShapeDtypeStruct((n_experts, LANES), I32),
    ]
    out_specs = [
        pl.BlockSpec((ts, d), lambda b, s: (b * n_s + s, 0)),
        pl.BlockSpec((ts, d // 2), lambda b, s: (b * n_s + s, 0)),
        pl.BlockSpec((SUBLANES, ts), lambda b, s: (0, b * n_s + s)),
        pl.BlockSpec((ts, SUBLANES), lambda b, s: (b * n_s + s, 0)),
        pl.BlockSpec((n_experts, LANES), lambda b, s: (0, 0)),
    ]
    return out_shape, out_specs


def _mixer_call(kern, name, prev, src, bsz, seq, ts, mixer_ops, mixer_specs, nffn, router, scratch):
    t, d = bsz * seq, src.rows.shape[1]
    n_s = seq // ts
    tok_ops, tok_specs = _mixer_in(prev, src, t, ts, n_s, d)
    r_ops, r_specs = _router_in(nffn, router, d, ts)
    out_shape, out_specs = _mixer_out(t, d, ts, n_s)
    return pl.pallas_call(
        functools.partial(kern, n_tok=len(tok_ops)),
        grid=(bsz, n_s),
        in_specs=tok_specs + mixer_specs + r_specs,
        out_specs=out_specs,
        out_shape=out_shape,
        scratch_shapes=scratch + [pltpu.VMEM((ROUTER_GROUPS[1], LANES), F32)],
        compiler_params=pltpu.CompilerParams(dimension_semantics=("arbitrary", "arbitrary"),
                                             vmem_limit_bytes=VMEM_LIMIT_BYTES),
        name=name,
    )(*tok_ops, *mixer_ops, *r_ops)


def _token_input(refs, n_tok):
    return refs[0][...] if n_tok == 1 else _combined(*refs[:n_tok])


def _pool_kernel(*refs, n_tok):
    x = _token_input(refs, n_tok)
    (nmix_ref, pw_ref, scale_ref,
     nffn_ref, wr_hi_ref, wr_lo_ref, br_ref, tri_ref,
     xm_ref, hp_ref, ids_ref, gate_ref, cnt_ref,
     carry_ref, base_ref) = refs[n_tok:]
    b = pl.program_id(0)
    s = pl.program_id(1)
    ts, d = x.shape
    ch = d // len(POOL_WINDOWS)

    @pl.when(s == 0)
    def _():
        carry_ref[...] = jnp.zeros_like(carry_ref)

    h = _rms(x, nmix_ref[...])
    ext = jnp.concatenate([carry_ref[...], h], axis=0)
    carry_ref[...] = h[ts - POOL_HALO:]

    pos = s * ts + lax.broadcasted_iota(I32, (ts, 1), 0) + 1
    ys = []
    for g, win in enumerate(POOL_WINDOWS):
        acc = ext[:, g * ch:(g + 1) * ch]
        k = 1
        while k < win:
            acc = acc + pltpu.roll(acc, k, axis=0)
            k *= 2
        inv_cnt = 1.0 / jnp.minimum(pos, win).astype(F32)
        pooled = acc[POOL_HALO:] * inv_cnt - h[:, g * ch:(g + 1) * ch]
        ys.append(_dot(pooled.astype(BF16), pw_ref[g]))
    y = jnp.concatenate(ys, axis=1) * scale_ref[...]
    xm = x + y
    xm_ref[...] = xm
    _route(xm, (b == 0) & (s == 0), nffn_ref, wr_hi_ref, wr_lo_ref, br_ref, tri_ref, base_ref,
           hp_ref, ids_ref, gate_ref, cnt_ref)


def _pool_layer(prev, src, bsz, seq, ts, nmix, pool_w, pool_scale, nffn, router):
    d = src.rows.shape[1]
    n_win, ch, _ = pool_w.shape
    assert n_win == len(POOL_WINDOWS) and ch * n_win == d
    ops = [nmix.reshape(1, d), pool_w.astype(BF16), pool_scale.reshape(1, d)]
    specs = [_const_spec((1, d)), _const_spec((n_win, ch, ch)), _const_spec((1, d))]
    return _mixer_call(_pool_kernel, "pool_mixer", prev, src, bsz, seq, ts, ops, specs, nffn, router,
                       [pltpu.VMEM((POOL_HALO, d), F32)])


def _sconv_kernel(*refs, n_tok):
    x = _token_input(refs, n_tok)
    (nmix_ref, win_ref, taps_ref, wout_ref,
     nffn_ref, wr_hi_ref, wr_lo_ref, br_ref, tri_ref,
     xm_ref, hp_ref, ids_ref, gate_ref, cnt_ref,
     carry_ref, base_ref) = refs[n_tok:]
    b = pl.program_id(0)
    s = pl.program_id(1)
    ts, d = x.shape

    @pl.when(s == 0)
    def _():
        carry_ref[...] = jnp.zeros_like(carry_ref)

    h = _rms(x, nmix_ref[...]).astype(BF16)
    b_gate = _dot(h, win_ref[:, 0:d])
    c_gate = _dot(h, win_ref[:, d:2 * d])
    v = _dot(h, win_ref[:, 2 * d:3 * d])
    u_pre = c_gate * v
    ext = jnp.concatenate([carry_ref[...], u_pre], axis=0)
    carry_ref[...] = u_pre[ts - CONV_HALO:]
    width = taps_ref.shape[0]
    u = taps_ref[width - 1:width, :] * u_pre
    for k in range(1, width):
        u = u + taps_ref[width - 1 - k:width - k, :] * _shift_rows(ext, k, CONV_HALO)
    mix = _dot((b_gate * u).astype(BF16), wout_ref[...])
    xm = x + mix
    xm_ref[...] = xm
    _route(xm, (b == 0) & (s == 0), nffn_ref, wr_hi_ref, wr_lo_ref, br_ref, tri_ref, base_ref,
           hp_ref, ids_ref, gate_ref, cnt_ref)


def _sconv_layer(prev, src, bsz, seq, ts, nmix, w_in, taps, w_out, nffn, router):
    d = src.rows.shape[1]
    width = taps.shape[0]
    assert width - 1 <= CONV_HALO
    ops = [nmix.reshape(1, d), w_in.astype(BF16), taps, w_out.astype(BF16)]
    specs = [_const_spec((1, d)), _const_spec((d, 3 * d)), _const_spec((width, d)), _const_spec((d, d))]
    return _mixer_call(_sconv_kernel, "sconv_mixer", prev, src, bsz, seq, ts, ops, specs, nffn, router,
                       [pltpu.VMEM((CONV_HALO, d), F32)])


def _ssm_kernel(*refs, n_tok):
    x = _token_input(refs, n_tok)
    (nmix_ref, wz_ref, wxbc_ref, wdt_hi_ref, wdt_lo_ref, convw_ref, convb_ref,
     dtb_ref, alog_ref, expand_ref, dskip_ref, normw_ref, wout_ref, triq_ref,
     nffn_ref, wr_hi_ref, wr_lo_ref, br_ref, tri_ref,
     xm_ref, hp_ref, ids_ref, gate_ref, cnt_ref,
     carry_ref, z_ref, xs_ref, bm_ref, cm_ref, y_ref, dtt_ref, at_ref, state_ref, base_ref) = refs[n_tok:]
    b = pl.program_id(0)
    s = pl.program_id(1)
    ts, d = x.shape
    di = z_ref.shape[1]
    gn = bm_ref.shape[1]
    cd = di + 2 * gn
    nh = alog_ref.shape[0]
    hp = di // nh
    grp = SSM_GROUPS
    st = gn // grp
    hpg = nh // grp
    gw = hpg * hp
    q = SSM_CHUNK
    n_chunks = ts // q

    @pl.when(s == 0)
    def _():
        carry_ref[...] = jnp.zeros_like(carry_ref)
        state_ref[...] = jnp.zeros_like(state_ref)

    h = _rms(x, nmix_ref[...])
    h_hi, h_lo = _hilo(h)

    col_blk = gn

    def project(c0):
        cols = slice(c0, c0 + col_blk)
        if c0 < di:
            z_ref[:, cols] = _dot(h_hi, wz_ref[:, cols])
        return _dot(h_hi, wxbc_ref[:, cols])

    pre_next = project(0)
    for c0 in range(0, cd, col_blk):
        cols = slice(c0, c0 + col_blk)
        pre = pre_next
        if c0 + col_blk < cd:
            pre_next = project(c0 + col_blk)
        ext = jnp.concatenate([carry_ref[:, cols], pre], axis=0)
        carry_ref[:, cols] = pre[ts - CONV_HALO:]
        ext1 = pltpu.roll(ext, 1, axis=0)
        pair = convw_ref[1:2, cols] * ext + convw_ref[0:1, cols] * ext1
        acc = (convw_ref[3:4, cols] * pre + convb_ref[:, cols] + convw_ref[2:3, cols] * ext1[CONV_HALO:]
               + _shift_rows(pair, 2, CONV_HALO))
        act = _silu(acc)
        if c0 < di:
            xs_ref[:, cols] = act
        elif c0 < di + gn:
            bm_ref[...] = act
        else:
            cm_ref[...] = act

    dt_t = _dot_nt_hilo(wdt_hi_ref[...], wdt_lo_ref[...], h_hi, h_lo) + dtb_ref[...]
    dt_t = jnp.maximum(dt_t, 0.0) + jnp.log(1.0 + jnp.exp(-jnp.abs(dt_t)))
    a_t = dt_t * (-jnp.exp(alog_ref[...]))
    for c in range(n_chunks):
        dtt_ref[c] = dt_t[:, c * q:(c + 1) * q]
        at_ref[c] = a_t[:, c * q:(c + 1) * q]

    row_q = lax.broadcasted_iota(I32, (q, q), 0)
    col_q = lax.broadcasted_iota(I32, (q, q), 1)
    causal = row_q >= col_q
    lane_lo = lax.broadcasted_iota(I32, (q, 2 * hp), 1) < hp
    neg_inf = jnp.float32(-jnp.inf)
    expand = expand_ref[...]

    def chunk_body(c, carry):
        r0 = pl.multiple_of(c * q, q)
        rows = pl.ds(r0, q)
        acs_t = _dot3(at_ref[c], triq_ref[...])
        dt_c = dtt_ref[c]
        w_t = dt_c * jnp.exp(acs_t[:, q - 1:q] - acs_t)
        src_t = acs_t - jnp.log(dt_c)
        acs = acs_t.T
        od_hi, od_lo = _hilo(jnp.exp(acs))
        out_decay_e = _dot(od_hi, expand) + _dot(od_lo, expand)
        w_e = _dot(w_t.T.astype(BF16), expand)
        xc = xs_ref[rows, :]
        xw = xc * w_e
        bc = bm_ref[rows, :]
        cc = cm_ref[rows, :]
        for g in range(grp):
            gcols = slice(g * gw, (g + 1) * gw)
            bg = bc[:, g * st:(g + 1) * st].astype(BF16)
            cg = cc[:, g * st:(g + 1) * st].astype(BF16)
            xg = xc[:, gcols]
            cb = _dot_nt(cg, bg)
            hprev = state_ref[g]
            y_off = _dot(cg, hprev.astype(BF16)) * out_decay_e[:, gcols]
            new_states = lax.dot_general(bg, xw[:, gcols].astype(BF16), TN_DIMS, preferred_element_type=F32)
            state_ref[g] = hprev * out_decay_e[q - 1:q, gcols] + new_states
            for j in range(hpg // 2):
                h0 = g * hpg + 2 * j
                ms = []
                for hh in (h0, h0 + 1):
                    seg = acs[:, hh:hh + 1] - src_t[hh:hh + 1, :]
                    ms.append((cb * jnp.exp(jnp.where(causal, seg, neg_inf))).astype(BF16))
                lhs = jnp.concatenate(ms, axis=1)
                xp = xg[:, 2 * j * hp:(2 * j + 2) * hp]
                rhs = jnp.concatenate([jnp.where(lane_lo, xp, 0.0), jnp.where(lane_lo, 0.0, xp)],
                                      axis=0).astype(BF16)
                pc = slice(g * gw + 2 * j * hp, g * gw + (2 * j + 2) * hp)
                y_ref[rows, pc] = _dot(lhs, rhs) + y_off[:, 2 * j * hp:(2 * j + 2) * hp]
        return carry

    lax.fori_loop(0, n_chunks, chunk_body, 0)

    out = None
    for g in range(grp):
        gcols = slice(g * gw, (g + 1) * gw)
        yg = (y_ref[:, gcols] + xs_ref[:, gcols] * dskip_ref[:, gcols]) * _silu(z_ref[:, gcols])
        yn = yg * lax.rsqrt(jnp.mean(yg * yg, axis=-1, keepdims=True) + NORM_EPS) * normw_ref[:, gcols]
        part = _dot(yn.astype(BF16), wout_ref[gcols, :])
        out = part if out is None else out + part
    xm = x + out
    xm_ref[...] = xm
    _route(xm, (b == 0) & (s == 0), nffn_ref, wr_hi_ref, wr_lo_ref, br_ref, tri_ref, base_ref,
           hp_ref, ids_ref, gate_ref, cnt_ref)


def _ssm_layer(prev, src, bsz, seq, ts, nmix, w_in, conv_w, conv_b, dt_bias, a_log, d_skip, norm_w, w_out,
               nffn, router):
    d = src.rows.shape[1]
    assert ts % SSM_CHUNK == 0
    nh = a_log.shape[0]
    di = norm_w.shape[0]
    cd = conv_w.shape[1]
    gn = (cd - di) // 2
    hp = di // nh
    width = conv_w.shape[0]
    assert width == SSM_CONV_WIDTH and (nh // SSM_GROUPS) % 2 == 0 and di % gn == 0
    st = gn // SSM_GROUPS
    gw = di // SSM_GROUPS
    q = SSM_CHUNK
    n_chunks = ts // q

    wz = w_in[:, :di].astype(BF16)
    wxbc = w_in[:, di:di + cd].astype(BF16)
    wdt_t = w_in[:, di + cd:].T
    wdt_hi = wdt_t.astype(BF16)
    wdt_lo = (wdt_t - wdt_hi.astype(F32)).astype(BF16)
    expand = (jnp.arange(di)[None, :] // hp == jnp.arange(nh)[:, None]).astype(BF16)
    dskip_e = jnp.repeat(d_skip, hp).reshape(1, di)
    idx = jnp.arange(q)
    triq = (idx[:, None] <= idx[None, :]).astype(BF16)

    ops = [nmix.reshape(1, d), wz, wxbc, wdt_hi, wdt_lo, conv_w, conv_b.reshape(1, cd),
           dt_bias.reshape(nh, 1), a_log.reshape(nh, 1), expand, dskip_e, norm_w.reshape(1, di),
           w_out.astype(BF16), triq]
    specs = [
        _const_spec((1, d)),
        _const_spec((d, di)),
        _const_spec((d, cd)),
        _const_spec((nh, d)),
        _const_spec((nh, d)),
        _const_spec((width, cd)),
        _const_spec((1, cd)),
        _const_spec((nh, 1)),
        _const_spec((nh, 1)),
        _const_spec((nh, di)),
        _const_spec((1, di)),
        _const_spec((1, di)),
        _const_spec((di, d)),
        _const_spec((q, q)),
    ]
    scratch = [
        pltpu.VMEM((CONV_HALO, cd), F32),
        pltpu.VMEM((ts, di), F32),
        pltpu.VMEM((ts, di), F32),
        pltpu.VMEM((ts, gn), F32),
        pltpu.VMEM((ts, gn), F32),
        pltpu.VMEM((ts, di), F32),
        pltpu.VMEM((n_chunks, nh, q), F32),
        pltpu.VMEM((n_chunks, nh, q), F32),
        pltpu.VMEM((SSM_GROUPS, st, gw), F32),
    ]
    return _mixer_call(_ssm_kernel, "ssm_mixer", prev, src, bsz, seq, ts, ops, specs, nffn, router, scratch)


def _sc_mesh():
    return plsc.VectorSubcoreMesh(core_axis_name="core", subcore_axis_name="subcore")


def _sc_worker_base(per_worker):
    return (lax.axis_index("subcore") * SC_CORES + lax.axis_index("core")) * per_worker


def _sc_scratch(d, dtype):
    win = SC_WINDOW
    return [pltpu.VMEM((win,), I32), pltpu.VMEM((win,), I32),
            pltpu.VMEM((win, d), dtype), pltpu.VMEM((win, d), dtype),
            pltpu.SemaphoreType.DMA, pltpu.SemaphoreType.DMA]


def _scatter_rows(src, idx, n_rows):
    n_src, d = src.shape
    win = SC_WINDOW
    per_worker = n_src // SC_WORKERS
    n_win = per_worker // win
    assert idx.shape[0] == MOE_TOP_K * n_src
    assert n_src == per_worker * SC_WORKERS and per_worker == n_win * win and n_win % 2 == 0
    scratch = ([pltpu.VMEM((win,), I32)] * (2 * MOE_TOP_K) + [pltpu.VMEM((win, d), src.dtype)] * 2
               + [pltpu.SemaphoreType.DMA] * (2 * MOE_TOP_K))

    @functools.partial(pl.kernel, out_type=jax.ShapeDtypeStruct((n_rows, d), src.dtype), mesh=_sc_mesh(),
                       scratch_types=scratch, name="moe_scatter_rows")
    def scatter_kernel(x_hbm, i_hbm, o_hbm, idx_a0, idx_a1, idx_b0, idx_b1, rows_a, rows_b,
                       sem_a0, sem_a1, sem_b0, sem_b1):
        base = _sc_worker_base(per_worker)

        def step(w, idx_vs, rows_v, sems, first):
            if not first:
                for idx_v, sem in zip(idx_vs, sems):
                    pltpu.make_async_copy(rows_v, o_hbm.at[idx_v], sem).wait()
            for k, idx_v in enumerate(idx_vs):
                pltpu.sync_copy(i_hbm.at[pl.ds(k * n_src + base + w * win, win)], idx_v)
            pltpu.sync_copy(x_hbm.at[pl.ds(base + w * win, win)], rows_v)
            for idx_v, sem in zip(idx_vs, sems):
                pltpu.make_async_copy(rows_v, o_hbm.at[idx_v], sem).start()

        buf_a = ((idx_a0, idx_a1), rows_a, (sem_a0, sem_a1))
        buf_b = ((idx_b0, idx_b1), rows_b, (sem_b0, sem_b1))
        step(0, *buf_a, True)
        step(1, *buf_b, True)

        @pl.loop(2, n_win, step=2)
        def _(w):
            step(w, *buf_a, False)
            step(w + 1, *buf_b, False)

        for idx_vs, rows_v, sems in (buf_a, buf_b):
            for idx_v, sem in zip(idx_vs, sems):
                pltpu.make_async_copy(rows_v, o_hbm.at[idx_v], sem).wait()

    return scatter_kernel(src, idx)


def _gather_rows(table, idx):
    d = table.shape[1]
    m = idx.shape[0]
    win = SC_WINDOW
    per_worker = m // SC_WORKERS
    n_win = per_worker // win
    assert m == per_worker * SC_WORKERS and per_worker == n_win * win and n_win % 2 == 0

    @functools.partial(pl.kernel, out_type=jax.ShapeDtypeStruct((m, d), table.dtype), mesh=_sc_mesh(),
                       scratch_types=_sc_scratch(d, table.dtype), name="moe_gather_rows")
    def gather_kernel(x_hbm, i_hbm, o_hbm, idx_a, idx_b, rows_a, rows_b, sem_a, sem_b):
        base = _sc_worker_base(per_worker)

        def out_copy(w, rows_v, sem):
            return pltpu.make_async_copy(rows_v, o_hbm.at[pl.ds(base + w * win, win)], sem)

        def step(w, idx_v, rows_v, sem, first):
            pltpu.sync_copy(i_hbm.at[pl.ds(base + w * win, win)], idx_v)
            if not first:
                out_copy(w, rows_v, sem).wait()
            pltpu.sync_copy(x_hbm.at[idx_v], rows_v)
            out_copy(w, rows_v, sem).start()

        step(0, idx_a, rows_a, sem_a, True)
        step(1, idx_b, rows_b, sem_b, True)

        @pl.loop(2, n_win, step=2)
        def _(w):
            step(w, idx_a, rows_a, sem_a, False)
            step(w + 1, idx_b, rows_b, sem_b, False)

        out_copy(0, rows_a, sem_a).wait()
        out_copy(0, rows_b, sem_b).wait()

    return gather_kernel(table, idx)


def _ffn_kernel(be_ref, nu_ref, nv_ref, slot_ref, nxt_ref, xs_ref, wgu_hbm, wdn_hbm, ys_ref,
                wgu_f32_ref, wdn_f32_ref, wgu_bf_ref, wdn_bf_ref, sem_ref, *, layer):
    i = pl.program_id(0)
    bm = xs_ref.shape[0]
    f = wdn_bf_ref.shape[0]

    def weight_copies(expert, slot):
        return (pltpu.make_async_copy(wgu_hbm.at[layer, expert], wgu_f32_ref.at[slot], sem_ref.at[slot, 0]),
                pltpu.make_async_copy(wdn_hbm.at[layer, expert], wdn_f32_ref.at[slot], sem_ref.at[slot, 1]))

    @pl.when(i == 0)
    def _():
        for cp in weight_copies(be_ref[0], slot_ref[0]):
            cp.start()

    first_of_expert = (i == 0) | (be_ref[i] != be_ref[jnp.maximum(i - 1, 0)])

    @pl.when(first_of_expert & (i < nu_ref[0]))
    def _():
        slot = slot_ref[i]
        for cp in weight_copies(be_ref[i], slot):
            cp.wait()

        @pl.when(nxt_ref[i] >= 0)
        def _():
            for cp in weight_copies(nxt_ref[i], 1 - slot):
                cp.start()

        wgu_bf_ref[...] = wgu_f32_ref[slot].astype(BF16)
        wdn_bf_ref[...] = wdn_f32_ref[slot].astype(BF16)

    def ffn_rows(n):
        valid = lax.broadcasted_iota(I32, (n, 1), 0) < nv_ref[i]
        h = _unpack_bf16(jnp.where(valid, xs_ref[0:n, :], 0)).astype(BF16)
        gu = _dot(h, wgu_bf_ref[...])
        act = _silu(gu[:, :f]) * gu[:, f:]
        ys_ref[0:n, :] = _pack_bf16(_dot(act.astype(BF16), wdn_bf_ref[...]).astype(BF16))

    used = i < nu_ref[0]
    half = bm // 2

    @pl.when(used & (nv_ref[i] > half))
    def _():
        ffn_rows(bm)

    @pl.when(used & (nv_ref[i] <= half))
    def _():
        ffn_rows(half)
        ys_ref[half:bm, :] = jnp.zeros((bm - half, ys_ref.shape[1]), ys_ref.dtype)

    @pl.when(i >= nu_ref[0])
    def _():
        ys_ref[...] = jnp.zeros_like(ys_ref)


def _block_schedule(cnt, n_blocks, bm):
    n_exp = cnt.shape[0]
    padded = (cnt + bm - 1) // bm * bm
    e_ids = jnp.arange(n_exp, dtype=I32)
    before = e_ids[None, :] < e_ids[:, None]
    pad_start = jnp.sum(jnp.where(before, padded[None, :], 0), axis=1).astype(I32)
    pad_end = pad_start + padded
    nonempty = cnt > 0
    seg_of = jnp.sum(before & nonempty[None, :], axis=1)
    after = (e_ids[None, :] > e_ids[:, None]) & nonempty[None, :]
    next_of = jnp.min(jnp.where(after, e_ids[None, :], n_exp), axis=1)
    next_of = jnp.where(next_of == n_exp, -1, next_of)
    blk_row = jnp.arange(n_blocks, dtype=I32) * bm
    block_e = jnp.minimum(jnp.sum(blk_row[:, None] >= pad_end[None, :], axis=1), n_exp - 1).astype(I32)
    of_block = block_e[:, None] == e_ids[None, :]

    def per_block(v):
        return jnp.sum(jnp.where(of_block, v[None, :], 0), axis=1).astype(I32)

    n_used = (jnp.sum(padded, keepdims=True) // bm).astype(I32)
    n_valid = jnp.clip(per_block(pad_start + cnt) - blk_row, 0, bm).astype(I32)
    return pad_start, (block_e, n_used, n_valid, per_block(seg_of % 2), per_block(next_of))


def _expert_ffn(xs, schedule, w_gu_all, w_dn_all, layer):
    n_rows, dh = xs.shape
    bm = FFN_BLOCK
    n_blocks = n_rows // bm
    d, f = w_gu_all.shape[2], w_dn_all.shape[2]
    assert d == 2 * dh

    def xs_map(i, be, nu, nv, sl, nx):
        return (jnp.minimum(i, jnp.maximum(nu[0] - 1, 0)), 0)

    return pl.pallas_call(
        functools.partial(_ffn_kernel, layer=layer),
        grid_spec=pltpu.PrefetchScalarGridSpec(
            num_scalar_prefetch=5,
            grid=(n_blocks,),
            in_specs=[
                pl.BlockSpec((bm, dh), xs_map),
                pl.BlockSpec(memory_space=pl.ANY),
                pl.BlockSpec(memory_space=pl.ANY),
            ],
            out_specs=pl.BlockSpec((bm, dh), lambda i, be, nu, nv, sl, nx: (i, 0)),
            scratch_shapes=[
                pltpu.VMEM((2, d, 2 * f), F32),
                pltpu.VMEM((2, f, d), F32),
                pltpu.VMEM((d, 2 * f), BF16),
                pltpu.VMEM((f, d), BF16),
                pltpu.SemaphoreType.DMA((2, 2)),
            ],
        ),
        out_shape=jax.ShapeDtypeStruct((n_rows, dh), I32),
        compiler_params=pltpu.CompilerParams(dimension_semantics=("arbitrary",),
                                             vmem_limit_bytes=VMEM_LIMIT_BYTES),
        name="expert_ffn",
    )(*schedule, xs, w_gu_all, w_dn_all)


def _dest_kernel(pad_start_ref, ids_ref, dest_ref):
    expert = ids_ref[0:MOE_TOP_K, :]
    dest = ids_ref[MOE_TOP_K:2 * MOE_TOP_K, :]
    for e in range(pad_start_ref.shape[0]):
        dest = dest + jnp.where(expert == e, pad_start_ref[e], 0)
    dest_ref[...] = dest


def _assignment_rows(ids, pad_start):
    t = ids.shape[1]
    tn = min(TS_DEST, t)
    assert t % tn == 0
    return pl.pallas_call(
        _dest_kernel,
        grid_spec=pltpu.PrefetchScalarGridSpec(
            num_scalar_prefetch=1,
            grid=(t // tn,),
            in_specs=[pl.BlockSpec((SUBLANES, tn), lambda i, ps: (0, i))],
            out_specs=pl.BlockSpec((MOE_TOP_K, tn), lambda i, ps: (0, i)),
        ),
        out_shape=jax.ShapeDtypeStruct((MOE_TOP_K, t), I32),
        compiler_params=pltpu.CompilerParams(dimension_semantics=("arbitrary",)),
        name="moe_dest",
    )(pad_start, ids)


def _moe(hp, ids, gate, counts, w_gu_all, w_dn_all, layer):
    t = hp.shape[0]
    n_exp = w_gu_all.shape[1]
    bm = FFN_BLOCK
    n_assign = t * MOE_TOP_K
    n_rows = (n_assign + bm - 1) // bm * bm + n_exp * bm
    n_blocks = n_rows // bm

    pad_start, schedule = _block_schedule(counts[:, 0], n_blocks, bm)
    dest = _assignment_rows(ids, pad_start).reshape(-1)
    xs = _scatter_rows(hp, dest, n_rows)
    ys = _expert_ffn(xs, schedule, w_gu_all, w_dn_all, layer)
    return _gather_rows(ys, dest), gate


def _final_kernel(xm_ref, y0_ref, y1_ref, g_ref, nfin_ref, *out_refs):
    out_refs[-1][...] = _rms(_combined(xm_ref, y0_ref, y1_ref, g_ref), nfin_ref[...])


def _final(xm, yg, gates, norm_final, out, group, n_groups):
    t, d = xm.shape
    ts = min(TS_FINAL, t)
    n_tiles = t // ts
    tile = pl.BlockSpec((ts, d), lambda i: (i, 0))
    in_specs = [tile, pl.BlockSpec((ts, d // 2), lambda i: (i, 0)),
                pl.BlockSpec((ts, d // 2), lambda i: (n_tiles + i, 0)),
                pl.BlockSpec((ts, SUBLANES), lambda i: (i, 0)), pl.BlockSpec((1, d), lambda i: (0, 0))]
    operands = [xm, yg, yg, gates, norm_final.reshape(1, d)]
    aliases = {}
    if out is not None:
        in_specs.append(pl.BlockSpec(memory_space=pl.ANY))
        operands.append(out)
        aliases = {len(operands) - 1: 0}
    return pl.pallas_call(
        _final_kernel,
        grid=(n_tiles,),
        in_specs=in_specs,
        out_specs=pl.BlockSpec((ts, d), lambda i: (group * n_tiles + i, 0)),
        out_shape=jax.ShapeDtypeStruct((n_groups * t, d), F32),
        input_output_aliases=aliases,
        compiler_params=pltpu.CompilerParams(dimension_semantics=("arbitrary",),
                                             vmem_limit_bytes=VMEM_LIMIT_BYTES),
        name="final_combine_norm",
    )(*operands)


def kernel(x, norm_mix, norm_ffn, norm_final, pool_w, pool_scale, sconv_in_w, sconv_taps, sconv_out_w, ssm_in_w, ssm_conv_w, ssm_conv_b, ssm_dt_bias, ssm_a_log, ssm_d, ssm_norm_w, ssm_out_w, router_group_w, router_group_b, router_expert_w, router_expert_b, expert_w_gu, expert_w_down):
    bsz, seq, d = x.shape
    depth = norm_mix.shape[0]
    assert (router_group_w.shape[2], router_expert_w.shape[2]) == ROUTER_GROUPS
    n_groups = TOKEN_GROUPS if bsz % TOKEN_GROUPS == 0 else 1
    gb = bsz // n_groups
    x2d = x.reshape(bsz * seq, d)
    src = [None] * n_groups
    prev = [None] * n_groups
    for i in range(depth):
        kind, j = i % N_MIXERS, i // N_MIXERS
        ts = min((TS_POOL, TS_SCONV, TS_SSM)[kind], seq)
        router = _router_operands(router_group_w[i], router_group_b[i], router_expert_w[i],
                                  router_expert_b[i], ts)
        for g in range(n_groups):
            s_in = Stream(x2d, g * gb * (seq // ts)) if i == 0 else src[g]
            if kind == 0:
                outs = _pool_layer(prev[g], s_in, gb, seq, ts, norm_mix[i], pool_w[j], pool_scale[j],
                                   norm_ffn[i], router)
            elif kind == 1:
                outs = _sconv_layer(prev[g], s_in, gb, seq, ts, norm_mix[i], sconv_in_w[j], sconv_taps[j],
                                    sconv_out_w[j], norm_ffn[i], router)
            else:
                outs = _ssm_layer(prev[g], s_in, gb, seq, ts, norm_mix[i], ssm_in_w[j], ssm_conv_w[j],
                                  ssm_conv_b[j], ssm_dt_bias[j], ssm_a_log[j], ssm_d[j], ssm_norm_w[j],
                                  ssm_out_w[j], norm_ffn[i], router)
            xm, hp, ids, gate, counts = outs
            src[g] = Stream(xm, 0)
            prev[g] = _moe(hp, ids, gate, counts, expert_w_gu, expert_w_down, i)
    out = None
    for g in range(n_groups):
        out = _final(src[g].rows, prev[g][0], prev[g][1], norm_final, out, g, n_groups)
    return out.reshape(bsz, seq, d)
```
